```python
import math
import jax, jax.numpy as jnp
from jax import lax
import numpy as np

D_MODEL = 1024
BATCH = 32
SEQ = 2048
DEPTH = 1

HEAD_DIM = 64
DIFF_HEADS = D_MODEL // (4 * HEAD_DIM)
DIFF_QK_WIDTH = DIFF_HEADS * 2 * HEAD_DIM
DIFF_V_DIM = 2 * HEAD_DIM
DIFF_WIDTH = DIFF_HEADS * DIFF_V_DIM
SWA_Q_HEADS = D_MODEL // (2 * HEAD_DIM)
SWA_KV_HEADS = 2
SWA_GROUP = SWA_Q_HEADS // SWA_KV_HEADS
SWA_WIDTH = SWA_Q_HEADS * HEAD_DIM
SWA_KV_WIDTH = SWA_KV_HEADS * HEAD_DIM
WINDOW = 128
BLOCK = 128
MIX_WIDTH = DIFF_WIDTH + SWA_WIDTH
IN_WIDTH = 2 * DIFF_QK_WIDTH + DIFF_WIDTH + SWA_WIDTH + 2 * SWA_KV_WIDTH
SPLIT_POINTS = [DIFF_QK_WIDTH,
                2 * DIFF_QK_WIDTH,
                2 * DIFF_QK_WIDTH + DIFF_WIDTH,
                2 * DIFF_QK_WIDTH + DIFF_WIDTH + SWA_WIDTH,
                2 * DIFF_QK_WIDTH + DIFF_WIDTH + SWA_WIDTH + SWA_KV_WIDTH]
ROPE_THETA = 10000.0
N_GROUPS = 4
EXPERTS_PER_GROUP = 4
N_EXPERTS = N_GROUPS * EXPERTS_PER_GROUP
TOP_K = 2
D_EXPERT = 512
EPS = 1e-6

kernel_name = "hymba_diffattn_swa_sink_hmoe_adaln"


def rmsnorm(x, g):
    xf = x.astype(jnp.float32)
    y = xf * lax.rsqrt(jnp.mean(xf * xf, axis=-1, keepdims=True) + EPS)
    return (y * g.astype(jnp.float32)).astype(x.dtype)


def rope_tables(seq, dtype):
    inv = ROPE_THETA ** (-jnp.arange(0, HEAD_DIM, 2, dtype=jnp.float32) / HEAD_DIM)
    ang = jnp.arange(seq, dtype=jnp.float32)[:, None] * inv[None, :]
    ang = jnp.concatenate([ang, ang], axis=-1)
    return jnp.cos(ang).astype(dtype), jnp.sin(ang).astype(dtype)


def apply_rope(x, cos, sin):
    shape = (1, x.shape[1]) + (1,) * (x.ndim - 3) + (HEAD_DIM,)
    cs, sn = cos.reshape(shape), sin.reshape(shape)
    x1, x2 = jnp.split(x, 2, axis=-1)
    return x * cs + jnp.concatenate([-x2, x1], axis=-1) * sn


def diff_attention(q, k, v, lam):
    B, S, H = q.shape[:3]
    nb = S // BLOCK
    scale = HEAD_DIM ** -0.5
    qb = q.reshape(B, nb, BLOCK, H, 2, HEAD_DIM).transpose(1, 0, 2, 3, 4, 5)
    kpos = jnp.arange(S)

    def one_block(args):
        qblk, n = args
        s = jnp.einsum('bqhmd,bkhmd->bhmqk', qblk, k).astype(jnp.float32) * scale
        qpos = n * BLOCK + jnp.arange(BLOCK)
        causal = kpos[None, :] <= qpos[:, None]
        p = jax.nn.softmax(jnp.where(causal, s, -jnp.inf), axis=-1)
        a = (p[:, :, 0] - lam * p[:, :, 1]).astype(v.dtype)
        return jnp.einsum('bhqk,bkhe->bqhe', a, v)

    out = lax.map(one_block, (qb, jnp.arange(nb)))
    return out.transpose(1, 0, 2, 3, 4).reshape(B, S, H, DIFF_V_DIM)


def swa_attention(q, k, v, sinks):
    B, S = q.shape[:2]
    nb = S // BLOCK
    scale = HEAD_DIM ** -0.5
    qb = q.reshape(B, nb, BLOCK, SWA_KV_HEADS, SWA_GROUP, HEAD_DIM)

    def band(t):
        tb = t.reshape(B, nb, BLOCK, SWA_KV_HEADS, HEAD_DIM)
        prev = jnp.concatenate([jnp.zeros_like(tb[:, :1]), tb[:, :-1]], axis=1)
        return jnp.concatenate([prev, tb], axis=2)

    kb, vb = band(k), band(v)
    s = jnp.einsum('bnqhgd,bnkhd->bnhgqk', qb, kb).astype(jnp.float32) * scale
    i = jnp.arange(BLOCK)[:, None]
    j = jnp.arange(2 * BLOCK)[None, :]
    dist = i + BLOCK - j
    in_window = (dist >= 0) & (dist < WINDOW)
    blk = jnp.arange(nb)[:, None, None]
    mask = in_window[None] & ((blk > 0) | (j >= BLOCK)[None])
    s = jnp.where(mask[None, :, None, None], s, -jnp.inf)
    sink = sinks.astype(jnp.float32).reshape(SWA_KV_HEADS, SWA_GROUP)[None, None, :, :, None, None]
    s_full = jnp.concatenate([s, jnp.broadcast_to(sink, s.shape[:-1] + (1,))], axis=-1)
    p = jax.nn.softmax(s_full, axis=-1)[..., :-1].astype(v.dtype)
    out = jnp.einsum('bnhgqk,bnkhd->bnqhgd', p, vb)
    return out.reshape(B, S, SWA_WIDTH)


def hierarchical_moe(h, w_rg, b_rg, w_re, b_re, w_gate, w_up, w_down):
    B, S, _ = h.shape
    grp_logits = (h @ w_rg).astype(jnp.float32) + b_rg.astype(jnp.float32)
    grp_prob = jax.nn.softmax(grp_logits, axis=-1)
    grp_p, grp_idx = lax.top_k(grp_prob, 1)
    exp_logits = ((h @ w_re).astype(jnp.float32) + b_re.astype(jnp.float32)).reshape(
        B, S, N_GROUPS, EXPERTS_PER_GROUP)
    in_grp = jnp.take_along_axis(exp_logits, grp_idx[..., None], axis=2)[:, :, 0]
    exp_prob = jax.nn.softmax(in_grp, axis=-1)
    top_p, top_i = lax.top_k(exp_prob, TOP_K)
    top_p = top_p / jnp.sum(top_p, axis=-1, keepdims=True)
    within = jnp.sum(jax.nn.one_hot(top_i, EXPERTS_PER_GROUP, dtype=jnp.float32) * top_p[..., None], axis=-2)
    grp_w = jax.nn.one_hot(grp_idx[..., 0], N_GROUPS, dtype=jnp.float32) * grp_p
    combine = (grp_w[..., :, None] * within[..., None, :]).reshape(B, S, N_EXPERTS).astype(h.dtype)
    y = jnp.zeros_like(h)
    for e in range(N_EXPERTS):
        a = jax.nn.silu(h @ w_gate[e]) * (h @ w_up[e])
        y = y + combine[..., e:e + 1] * (a @ w_down[e])
    return y


def setup_inputs(seed: int = 0) -> dict:
    key = jax.random.key(seed)
    ks = jax.random.split(key, 20)
    f32 = jnp.float32
    D = D_MODEL
    nrm = lambda k, shape, scale: jax.random.normal(k, shape, f32) * scale
    return {
        "x": nrm(ks[0], (BATCH, SEQ, D), 1.0),
        "c": nrm(ks[1], (BATCH, D), 1.0),
        "w_ada": nrm(ks[2], (DEPTH, D, 6 * D), D ** -0.5),
        "b_ada": nrm(ks[3], (DEPTH, 6 * D), 0.02),
        "g_mix": 1.0 + nrm(ks[4], (DEPTH, D), 0.02),
        "w_in": nrm(ks[5], (DEPTH, D, IN_WIDTH), D ** -0.5),
        "diff_lambda": nrm(ks[6], (DEPTH, 4, HEAD_DIM), 0.1),
        "g_diff_sub": 1.0 + nrm(ks[7], (DEPTH, DIFF_V_DIM), 0.02),
        "swa_sinks": nrm(ks[8], (DEPTH, SWA_Q_HEADS), 0.5),
        "w_out": nrm(ks[9], (DEPTH, MIX_WIDTH, D), MIX_WIDTH ** -0.5),
        "g_ffn": 1.0 + nrm(ks[10], (DEPTH, D), 0.02),
        "w_route_group": nrm(ks[11], (DEPTH, D, N_GROUPS), D ** -0.5),
        "b_route_group": nrm(ks[12], (DEPTH, N_GROUPS), 0.01),
        "w_route_expert": nrm(ks[13], (DEPTH, D, N_EXPERTS), D ** -0.5),
        "b_route_expert": nrm(ks[14], (DEPTH, N_EXPERTS), 0.01),
        "w_gate": nrm(ks[15], (DEPTH, N_EXPERTS, D, D_EXPERT), D ** -0.5),
        "w_up": nrm(ks[16], (DEPTH, N_EXPERTS, D, D_EXPERT), D ** -0.5),
        "w_down": nrm(ks[17], (DEPTH, N_EXPERTS, D_EXPERT, D), D_EXPERT ** -0.5),
        "g_final": 1.0 + nrm(ks[18], (D,), 0.02),
    }


def reference(x, c, w_ada, b_ada, g_mix, w_in, diff_lambda, g_diff_sub, swa_sinks, w_out,
              g_ffn, w_route_group, b_route_group, w_route_expert, b_route_expert,
              w_gate, w_up, w_down, g_final):
    B, S, _ = x.shape
    cos, sin = rope_tables(S, x.dtype)
    c_act = jax.nn.silu(c)
    for l in range(DEPTH):
        mod = c_act @ w_ada[l] + b_ada[l]
        sh1, sc1, gt1, sh2, sc2, gt2 = [m[:, None, :] for m in jnp.split(mod, 6, axis=-1)]

        h = rmsnorm(x, g_mix[l]) * (1.0 + sc1) + sh1
        proj = h @ w_in[l]
        dq, dk, dv, sq, sk, sv = jnp.split(proj, SPLIT_POINTS, axis=-1)

        lambda_init = 0.8 - 0.6 * math.exp(-0.3 * l)
        lam_p = diff_lambda[l].astype(jnp.float32)
        lam = jnp.exp(jnp.sum(lam_p[0] * lam_p[1])) - jnp.exp(jnp.sum(lam_p[2] * lam_p[3])) + lambda_init
        dq = apply_rope(dq.reshape(B, S, DIFF_HEADS, 2, HEAD_DIM), cos, sin)
        dk = apply_rope(dk.reshape(B, S, DIFF_HEADS, 2, HEAD_DIM), cos, sin)
        dv = dv.reshape(B, S, DIFF_HEADS, DIFF_V_DIM)
        o_diff = diff_attention(dq, dk, dv, lam)
        o_diff = (rmsnorm(o_diff, g_diff_sub[l]) * (1.0 - lambda_init)).reshape(B, S, DIFF_WIDTH)

        sq = apply_rope(sq.reshape(B, S, SWA_Q_HEADS, HEAD_DIM), cos, sin)
        sk = apply_rope(sk.reshape(B, S, SWA_KV_HEADS, HEAD_DIM), cos, sin)
        sv = sv.reshape(B, S, SWA_KV_HEADS, HEAD_DIM)
        o_swa = swa_attention(sq, sk, sv, swa_sinks[l])

        mix = jnp.concatenate([o_diff, o_swa], axis=-1) @ w_out[l]
        x = x + gt1 * mix

        h2 = rmsnorm(x, g_ffn[l]) * (1.0 + sc2) + sh2
        y = hierarchical_moe(h2, w_route_group[l], b_route_group[l], w_route_expert[l],
                             b_route_expert[l], w_gate[l], w_up[l], w_down[l])
        x = x + gt2 * y
    return rmsnorm(x, g_final)
```

```python
import functools
import math

import jax
import jax.numpy as jnp
from jax import lax
from jax.experimental import pallas as pl
from jax.experimental.pallas import tpu as pltpu

F32 = jnp.float32
BF16 = jnp.bfloat16
I32 = jnp.int32

D = 1024
HD = 64
EPS = 1e-6
ROPE_THETA = 10000.0
LAMBDA_INIT = 0.8 - 0.6 * math.exp(-0.3 * 0)
N_GROUPS = 4
EPG = 4
N_BUCKETS = 24
BUCKET_ROWS = 32
D_EXPERT = 512
AUG = 128

LANES = 128
VMEM_LIMIT = 48 * 1024 * 1024

TM_PROJ = 512
TQ = 512
BLK = 128
TM_MOE = 512
CH = 1024

PAIR_LO = (0, 0, 0, 1, 1, 2)
PAIR_HI = (1, 2, 3, 2, 3, 3)


def _cparams(sem):
    return pltpu.CompilerParams(dimension_semantics=sem, vmem_limit_bytes=VMEM_LIMIT)


def _ada_kernel(c_ref, w_ref, b_ref, o_ref):
    c = c_ref[...]
    ca = c * (1.0 / (1.0 + jnp.exp(-c)))
    o_ref[...] = jnp.dot(ca.astype(BF16), w_ref[...].astype(BF16),
                         preferred_element_type=F32) + b_ref[...]


def _ada_mod(c, w, b):
    bsz = c.shape[0]
    n = w.shape[1]
    tn = 512
    return pl.pallas_call(
        _ada_kernel,
        grid=(n // tn,),
        in_specs=[pl.BlockSpec((bsz, D), lambda j: (0, 0)),
                  pl.BlockSpec((D, tn), lambda j: (0, j)),
                  pl.BlockSpec((1, tn), lambda j: (0, j))],
        out_specs=pl.BlockSpec((bsz, tn), lambda j: (0, j)),
        out_shape=jax.ShapeDtypeStruct((bsz, n), F32),
        compiler_params=_cparams(("arbitrary",)),
        name="ada_mod",
    )(c, w, b)


def _rope(a, cos, sin, first_half):
    rolled = jnp.where(first_half, pltpu.roll(a, 96, 1), pltpu.roll(a, 32, 1))
    return a * cos + rolled * sin


def _inproj_kernel(x_ref, mod_ref, g_ref, w_ref, cos_ref, sin_ref,
                   dq_ref, dk_ref, dv_ref, sq_ref, sk_ref, sv_ref, h_scr):
    tm = x_ref.shape[0]
    x = x_ref[...]
    ms = jnp.mean(x * x, axis=-1, keepdims=True)
    y = x * lax.rsqrt(ms + EPS) * g_ref[...]
    m = mod_ref[0]
    h = y * (1.0 + m[1:2]) + m[0:1]
    h_scr[...] = h.astype(BF16)

    cos = cos_ref[...]
    sin = sin_ref[...]
    lane = lax.broadcasted_iota(I32, (tm, LANES), 1)
    first_half = (lane & 32) == 0
    low64 = lane < HD
    scale = HD ** -0.5

    def chunk(j):
        return jnp.dot(h_scr[...], w_ref[:, 256 * j:256 * (j + 1)], preferred_element_type=F32)

    def halves(a):
        return a[:, :LANES], a[:, LANES:]

    for j in range(2):
        for t, r in enumerate(halves(chunk(j))):
            c0 = 256 * j + LANES * t
            dq_ref[:, c0:c0 + LANES] = (_rope(r, cos, sin, first_half) * scale).astype(BF16)
    for j in range(2):
        for t, r in enumerate(halves(chunk(2 + j))):
            c0 = 256 * j + LANES * t
            dk_ref[:, c0:c0 + LANES] = _rope(r, cos, sin, first_half).astype(BF16)
    for j in range(2):
        dv_ref[:, 256 * j:256 * (j + 1)] = chunk(4 + j).astype(BF16)
    for j in range(2):
        for t, r in enumerate(halves(chunk(6 + j))):
            rp = _rope(r, cos, sin, first_half) * scale
            hd0 = 4 * j + 2 * t
            sq_ref[:, LANES * hd0:LANES * (hd0 + 1)] = jnp.where(low64, rp, 0.0).astype(BF16)
            sq_ref[:, LANES * (hd0 + 1):LANES * (hd0 + 2)] = jnp.where(
                low64, pltpu.roll(rp, HD, 1), 0.0).astype(BF16)
    kv = chunk(8)
    kk, vv = halves(kv)
    kr = _rope(kk, cos, sin, first_half)
    sk_ref[:, :LANES] = jnp.where(low64, kr, 0.0).astype(BF16)
    sk_ref[:, LANES:] = jnp.where(low64, pltpu.roll(kr, HD, 1), 0.0).astype(BF16)
    sv_ref[:, :LANES] = jnp.where(low64, vv, 0.0).astype(BF16)
    sv_ref[:, LANES:] = jnp.where(low64, pltpu.roll(vv, HD, 1), 0.0).astype(BF16)


def _in_proj(x2, mod3, g_mix, w_in_bf, cos, sin, seq):
    t = x2.shape[0]
    tm = TM_PROJ
    per_b = seq // tm
    n_in = w_in_bf.shape[1]
    row = lambda i: (i, 0)
    return pl.pallas_call(
        _inproj_kernel,
        grid=(t // tm,),
        in_specs=[pl.BlockSpec((tm, D), row),
                  pl.BlockSpec((1, 6, D), lambda i: (i // per_b, 0, 0)),
                  pl.BlockSpec((1, D), lambda i: (0, 0)),
                  pl.BlockSpec((D, n_in), lambda i: (0, 0)),
                  pl.BlockSpec((tm, LANES), lambda i: (i % per_b, 0)),
                  pl.BlockSpec((tm, LANES), lambda i: (i % per_b, 0))],
        out_specs=[pl.BlockSpec((tm, 512), row), pl.BlockSpec((tm, 512), row),
                   pl.BlockSpec((tm, 512), row), pl.BlockSpec((tm, 1024), row),
                   pl.BlockSpec((tm, 256), row), pl.BlockSpec((tm, 256), row)],
        out_shape=[jax.ShapeDtypeStruct((t, 512), BF16), jax.ShapeDtypeStruct((t, 512), BF16),
                   jax.ShapeDtypeStruct((t, 512), BF16), jax.ShapeDtypeStruct((t, 1024), BF16),
                   jax.ShapeDtypeStruct((t, 256), BF16), jax.ShapeDtypeStruct((t, 256), BF16)],
        scratch_shapes=[pltpu.VMEM((tm, D), BF16)],
        compiler_params=_cparams(("arbitrary",)),
        name="in_proj",
    )(x2, mod3, g_mix, w_in_bf, cos, sin)


_NT_DIMS = (((1,), (1,)), ((), ()))


def _diff_kernel(lam_ref, g_ref, q_ref, k_ref, v_ref, o_ref):
    seq = q_ref.shape[0]
    tq = TQ
    lp = lam_ref[...]
    lam = (jnp.exp(jnp.sum(lp[0:1] * lp[1:2], axis=-1, keepdims=True))
           - jnp.exp(jnp.sum(lp[2:3] * lp[3:4], axis=-1, keepdims=True)) + LAMBDA_INIT)
    lane = lax.broadcasted_iota(I32, (tq, LANES), 1)
    low64 = lane < HD
    causal = (lax.broadcasted_iota(I32, (tq, tq), 1) <= lax.broadcasted_iota(I32, (tq, tq), 0))
    gain = g_ref[...] * (1.0 - LAMBDA_INIT)

    for i in range(seq // tq):
        lo, hi = i * tq, (i + 1) * tq
        q = q_ref[lo:hi, :]
        zero = jnp.zeros_like(q)
        probs = []
        for qm in (jnp.where(low64, q, zero), jnp.where(low64, zero, q)):
            s_dg = lax.dot_general(qm, k_ref[lo:hi, :], _NT_DIMS, preferred_element_type=F32)
            s_dg = jnp.where(causal, s_dg, -jnp.inf)
            mx = jnp.max(s_dg, axis=-1, keepdims=True)
            if i > 0:
                s_off = lax.dot_general(qm, k_ref[0:lo, :], _NT_DIMS, preferred_element_type=F32)
                mx = jnp.maximum(mx, jnp.max(s_off, axis=-1, keepdims=True))
                p_off = jnp.exp(s_off - mx)
            p_dg = jnp.exp(s_dg - mx)
            l = jnp.sum(p_dg, axis=-1, keepdims=True)
            if i > 0:
                l = l + jnp.sum(p_off, axis=-1, keepdims=True)
                probs.append((p_off, p_dg, l))
            else:
                probs.append((None, p_dg, l))
        (p0_off, p0_dg, l0), (p1_off, p1_dg, l1) = probs
        c = lam * l0 / l1
        o = jnp.dot((p0_dg - c * p1_dg).astype(BF16), v_ref[lo:hi, :], preferred_element_type=F32)
        if i > 0:
            o = o + jnp.dot((p0_off - c * p1_off).astype(BF16), v_ref[0:lo, :],
                            preferred_element_type=F32)
        o = o / l0
        ms = jnp.mean(o * o, axis=-1, keepdims=True)
        o_ref[lo:hi, :] = (o * lax.rsqrt(ms + EPS) * gain).astype(BF16)


def _diff_attn(dq, dk, dv, lam_p, g_sub, bsz, seq):
    t = dq.shape[0]
    blk = lambda b, h: (b, h)
    return pl.pallas_call(
        _diff_kernel,
        grid=(bsz, 4),
        in_specs=[pl.BlockSpec((4, HD), lambda b, h: (0, 0)),
                  pl.BlockSpec((1, LANES), lambda b, h: (0, 0)),
                  pl.BlockSpec((seq, LANES), blk),
                  pl.BlockSpec((seq, LANES), blk),
                  pl.BlockSpec((seq, LANES), blk)],
        out_specs=pl.BlockSpec((seq, LANES), blk),
        out_shape=jax.ShapeDtypeStruct((t, 512), BF16),
        compiler_params=_cparams(("arbitrary", "arbitrary")),
        name="diff_attn",
    )(lam_p, g_sub, dq, dk, dv)


def _swa_kernel(sink_ref, q_ref, k_ref, v_ref, o_ref):
    seq = q_ref.shape[0]
    g = pl.program_id(1)
    rows = 4 * BLK
    hsel = jnp.right_shift(lax.broadcasted_iota(I32, (rows, 8), 0), 7) + 4 * g
    sink_col = jnp.sum(jnp.where(lax.broadcasted_iota(I32, (rows, 8), 1) == hsel,
                                 sink_ref[...], 0.0), axis=-1, keepdims=True)
    qi = lax.broadcasted_iota(I32, (rows, 2 * BLK), 0) & (BLK - 1)
    kj = lax.broadcasted_iota(I32, (rows, 2 * BLK), 1)
    dist = qi + BLK - kj
    band = (dist >= 0) & (dist < BLK)
    first = (lax.broadcasted_iota(I32, (rows, BLK), 1)
             <= (lax.broadcasted_iota(I32, (rows, BLK), 0) & (BLK - 1)))
    low64 = lax.broadcasted_iota(I32, (BLK, LANES), 1) < HD

    def attend(q_rows, k_rows, mask):
        qs = jnp.concatenate([q_ref[q_rows, LANES * j:LANES * (j + 1)] for j in range(4)], axis=0)
        s = lax.dot_general(qs, k_ref[k_rows, :], _NT_DIMS, preferred_element_type=F32)
        s = jnp.where(mask, s, -jnp.inf)
        mx = jnp.maximum(jnp.max(s, axis=-1, keepdims=True), sink_col)
        p = jnp.exp(s - mx)
        l = jnp.sum(p, axis=-1, keepdims=True) + jnp.exp(sink_col - mx)
        o = jnp.dot(p.astype(BF16), v_ref[k_rows, :], preferred_element_type=F32) / l
        for c in range(2):
            even = o[(2 * c) * BLK:(2 * c + 1) * BLK, :]
            odd = o[(2 * c + 1) * BLK:(2 * c + 2) * BLK, :]
            o_ref[q_rows, LANES * c:LANES * (c + 1)] = jnp.where(
                low64, even, pltpu.roll(odd, HD, 1)).astype(BF16)

    attend(pl.ds(0, BLK), pl.ds(0, BLK), first)

    def body(n, carry):
        q0 = pl.multiple_of(n * BLK, BLK)
        attend(pl.ds(q0, BLK), pl.ds(pl.multiple_of(q0 - BLK, BLK), 2 * BLK), band)
        return carry

    lax.fori_loop(1, seq // BLK, body, 0)


def _swa_attn(sq, sk, sv, sinks, bsz, seq):
    t = sq.shape[0]
    return pl.pallas_call(
        _swa_kernel,
        grid=(bsz, 2),
        in_specs=[pl.BlockSpec((1, 8), lambda b, g: (0, 0)),
                  pl.BlockSpec((seq, 512), lambda b, g: (b, g)),
                  pl.BlockSpec((seq, LANES), lambda b, g: (b, g)),
                  pl.BlockSpec((seq, LANES), lambda b, g: (b, g))],
        out_specs=pl.BlockSpec((seq, 256), lambda b, g: (b, g)),
        out_shape=jax.ShapeDtypeStruct((t, 512), BF16),
        compiler_params=_cparams(("arbitrary", "arbitrary")),
        name="swa_attn",
    )(sinks, sq, sk, sv)


def _outproj_kernel(od_ref, os_ref, x_ref, mod_ref, g_ref, wo_ref, wr_ref, br_ref, tri_ref,
                    x1_ref, h2_ref, bucket_ref, rank_ref, cnt_ref, base_scr):
    tm = x_ref.shape[0]

    @pl.when(pl.program_id(0) == 0)
    def _():
        base_scr[...] = jnp.zeros_like(base_scr)

    m = mod_ref[0]
    mix = (jnp.dot(od_ref[...], wo_ref[0:512, :], preferred_element_type=F32)
           + jnp.dot(os_ref[...], wo_ref[512:1024, :], preferred_element_type=F32))
    x1 = x_ref[...] + m[2:3] * mix
    x1_ref[...] = x1
    ms = jnp.mean(x1 * x1, axis=-1, keepdims=True)
    h2 = x1 * lax.rsqrt(ms + EPS) * g_ref[...] * (1.0 + m[4:5]) + m[3:4]
    h2_ref[:, 0:D] = h2

    logits = jnp.dot(h2.astype(BF16), wr_ref[...], preferred_element_type=F32) + br_ref[...]
    lt = logits.T
    r = [lt[i:i + 1, :] for i in range(N_GROUPS + N_GROUPS * EPG)]

    gl = r[0:N_GROUPS]
    gmax = jnp.maximum(jnp.maximum(gl[0], gl[1]), jnp.maximum(gl[2], gl[3]))
    gidx = jnp.where(gl[0] == gmax, 0, jnp.where(gl[1] == gmax, 1, jnp.where(gl[2] == gmax, 2, 3)))
    gz = (jnp.exp(gl[0] - gmax) + jnp.exp(gl[1] - gmax)
          + jnp.exp(gl[2] - gmax) + jnp.exp(gl[3] - gmax))
    grp_p = 1.0 / gz

    el = []
    for i in range(EPG):
        e = r[N_GROUPS + 3 * EPG + i]
        for gg in (2, 1, 0):
            e = jnp.where(gidx == gg, r[N_GROUPS + gg * EPG + i], e)
        el.append(e)
    emax = jnp.maximum(jnp.maximum(el[0], el[1]), jnp.maximum(el[2], el[3]))
    ex = [jnp.exp(e - emax) for e in el]
    ez = ex[0] + ex[1] + ex[2] + ex[3]
    pr = [e / ez for e in ex]
    p1 = jnp.maximum(jnp.maximum(pr[0], pr[1]), jnp.maximum(pr[2], pr[3]))
    a = jnp.where(pr[0] == p1, 0, jnp.where(pr[1] == p1, 1, jnp.where(pr[2] == p1, 2, 3)))
    rest = [jnp.where(a == i, -1.0, pr[i]) for i in range(EPG)]
    p2 = jnp.maximum(jnp.maximum(rest[0], rest[1]), jnp.maximum(rest[2], rest[3]))
    b = jnp.where(rest[0] == p2, 0, jnp.where(rest[1] == p2, 1, jnp.where(rest[2] == p2, 2, 3)))
    psum = p1 + p2
    wa = grp_p * (p1 / psum)
    wb = grp_p * (p2 / psum)
    lo = jnp.minimum(a, b)
    hi = jnp.maximum(a, b)
    w_lo = jnp.where(a < b, wa, wb)
    w_hi = jnp.where(a < b, wb, wa)
    pair = jnp.where(lo == 0, hi - 1, jnp.where(lo == 1, hi + 1, 5))
    bucket = gidx * 6 + pair

    rid = lax.broadcasted_iota(I32, (AUG, tm), 0)
    h2_ref[:, D:D + AUG] = jnp.where(rid == 0, w_lo, jnp.where(rid == 1, w_hi, 0.0)).T

    oh = (lax.broadcasted_iota(I32, (BUCKET_ROWS, tm), 0) == bucket)
    oh_f = jnp.where(oh, 1.0, 0.0)
    before = jnp.dot(oh_f.astype(BF16), tri_ref[...], preferred_element_type=F32)
    base = base_scr[:, 0:1]
    rank = jnp.sum(oh_f * (base + before), axis=0, keepdims=True)
    new_base = base_scr[...] + jnp.sum(oh_f, axis=1, keepdims=True)
    base_scr[...] = new_base
    cnt_ref[...] = new_base
    bucket_ref[0] = bucket
    rank_ref[0] = rank.astype(I32)


def _out_proj(o_diff, o_swa, x2, mod3, g_ffn, w_out_bf, w_r, b_r, tri, seq):
    t = x2.shape[0]
    tm = TM_PROJ
    per_b = seq // tm
    row = lambda i: (i, 0)
    const = lambda i: (0, 0)
    return pl.pallas_call(
        _outproj_kernel,
        grid=(t // tm,),
        in_specs=[pl.BlockSpec((tm, 512), row), pl.BlockSpec((tm, 512), row),
                  pl.BlockSpec((tm, D), row),
                  pl.BlockSpec((1, 6, D), lambda i: (i // per_b, 0, 0)),
                  pl.BlockSpec((1, D), const),
                  pl.BlockSpec((D, D), const),
                  pl.BlockSpec((D, LANES), const),
                  pl.BlockSpec((1, LANES), const),
                  pl.BlockSpec((tm, tm), const)],
        out_specs=[pl.BlockSpec((tm, D), row),
                   pl.BlockSpec((tm, D + AUG), row),
                   pl.BlockSpec((1, 1, tm), lambda i: (i, 0, 0)),
                   pl.BlockSpec((1, 1, tm), lambda i: (i, 0, 0)),
                   pl.BlockSpec((BUCKET_ROWS, LANES), const)],
        out_shape=[jax.ShapeDtypeStruct((t, D), F32),
                   jax.ShapeDtypeStruct((t, D + AUG), F32),
                   jax.ShapeDtypeStruct((t // tm, 1, tm), I32),
                   jax.ShapeDtypeStruct((t // tm, 1, tm), I32),
                   jax.ShapeDtypeStruct((BUCKET_ROWS, LANES), F32)],
        scratch_shapes=[pltpu.VMEM((BUCKET_ROWS, LANES), F32)],
        compiler_params=_cparams(("arbitrary",)),
        name="out_proj",
    )(o_diff, o_swa, x2, mod3, g_ffn, w_out_bf, w_r, b_r, tri)


def _row_copy(src_hbm, dst_hbm, src_row, dst_row, sem):
    return pltpu.make_async_copy(src_hbm.at[pl.ds(src_row, 1)], dst_hbm.at[pl.ds(dst_row, 1)], sem)


def _dispatch_kernel(pos_ref, src_hbm, init_hbm, dst_hbm, sem):
    del init_hbm
    base = pl.program_id(0) * CH

    def start(r, carry):
        _row_copy(src_hbm, dst_hbm, base + r, pos_ref[0, 0, r], sem).start()
        return carry

    lax.fori_loop(0, CH, start, 0)

    def wait(r, carry):
        _row_copy(src_hbm, dst_hbm, 0, 0, sem).wait()
        return carry

    lax.fori_loop(0, CH, wait, 0)


def _dispatch(pos3, h2aug, n_rows):
    t, width = h2aug.shape
    zeros = jnp.zeros((n_rows, width), F32)
    return pl.pallas_call(
        _dispatch_kernel,
        grid=(t // CH,),
        in_specs=[pl.BlockSpec((1, 1, CH), lambda i: (i, 0, 0), memory_space=pltpu.SMEM),
                  pl.BlockSpec(memory_space=pl.ANY),
                  pl.BlockSpec(memory_space=pl.ANY)],
        out_specs=pl.BlockSpec(memory_space=pl.ANY),
        out_shape=jax.ShapeDtypeStruct((n_rows, width), F32),
        scratch_shapes=[pltpu.SemaphoreType.DMA(())],
        input_output_aliases={2: 0},
        compiler_params=_cparams(("arbitrary",)),
        name="dispatch",
    )(pos3, h2aug, zeros)


def _moe_kernel(lo_ref, hi_ref, valid_ref, xs_ref, wg_lo, wu_lo, wd_lo, wg_hi, wu_hi, wd_hi, y_ref):
    del lo_ref, hi_ref
    t = pl.program_id(0)

    @pl.when(valid_ref[t] != 0)
    def _():
        xb = xs_ref[:, 0:D].astype(BF16)
        y = None
        for k, (wg, wu, wd) in enumerate(((wg_lo, wu_lo, wd_lo), (wg_hi, wu_hi, wd_hi))):
            gt = jnp.dot(xb, wg[0], preferred_element_type=F32)
            up = jnp.dot(xb, wu[0], preferred_element_type=F32)
            act = (gt * (1.0 / (1.0 + jnp.exp(-gt))) * up).astype(BF16)
            dn = jnp.dot(act, wd[0], preferred_element_type=F32)
            term = xs_ref[:, D + k:D + k + 1] * dn
            y = term if y is None else y + term
        y_ref[...] = y

    @pl.when(valid_ref[t] == 0)
    def _():
        y_ref[...] = jnp.zeros_like(y_ref)


def _moe(tile_lo, tile_hi, tile_valid, xs, wg, wu, wd):
    n_rows, width = xs.shape
    n_tiles = n_rows // TM_MOE
    up_lo = lambda t, lo, hi, v: (lo[t], 0, 0)
    up_hi = lambda t, lo, hi, v: (hi[t], 0, 0)
    grid_spec = pltpu.PrefetchScalarGridSpec(
        num_scalar_prefetch=3,
        grid=(n_tiles,),
        in_specs=[pl.BlockSpec((TM_MOE, width), lambda t, lo, hi, v: (t, 0)),
                  pl.BlockSpec((1, D, D_EXPERT), up_lo), pl.BlockSpec((1, D, D_EXPERT), up_lo),
                  pl.BlockSpec((1, D_EXPERT, D), up_lo),
                  pl.BlockSpec((1, D, D_EXPERT), up_hi), pl.BlockSpec((1, D, D_EXPERT), up_hi),
                  pl.BlockSpec((1, D_EXPERT, D), up_hi)],
        out_specs=pl.BlockSpec((TM_MOE, D), lambda t, lo, hi, v: (t, 0)),
    )
    return pl.pallas_call(
        _moe_kernel,
        grid_spec=grid_spec,
        out_shape=jax.ShapeDtypeStruct((n_rows, D), F32),
        compiler_params=_cparams(("arbitrary",)),
        name="moe",
    )(tile_lo, tile_hi, tile_valid, xs, wg, wu, wd, wg, wu, wd)


def _final_kernel(pos_ref, ys_hbm, x1_ref, mod_ref, g_ref, o_ref, y_scr, sem):
    def start(r, carry):
        pltpu.make_async_copy(ys_hbm.at[pl.ds(pos_ref[0, 0, r], 1)], y_scr.at[pl.ds(r, 1)], sem).start()
        return carry

    lax.fori_loop(0, CH, start, 0)

    def wait(r, carry):
        pltpu.make_async_copy(ys_hbm.at[pl.ds(0, 1)], y_scr.at[pl.ds(0, 1)], sem).wait()
        return carry

    lax.fori_loop(0, CH, wait, 0)

    m = mod_ref[0]
    x2 = x1_ref[...] + m[5:6] * y_scr[...]
    ms = jnp.mean(x2 * x2, axis=-1, keepdims=True)
    o_ref[...] = x2 * lax.rsqrt(ms + EPS) * g_ref[...]


def _final(pos3, ys, x1, mod3, g_final, seq):
    t = x1.shape[0]
    per_b = seq // CH
    return pl.pallas_call(
        _final_kernel,
        grid=(t // CH,),
        in_specs=[pl.BlockSpec((1, 1, CH), lambda i: (i, 0, 0), memory_space=pltpu.SMEM),
                  pl.BlockSpec(memory_space=pl.ANY),
                  pl.BlockSpec((CH, D), lambda i: (i, 0)),
                  pl.BlockSpec((1, 6, D), lambda i: (i // per_b, 0, 0)),
                  pl.BlockSpec((1, D), lambda i: (0, 0))],
        out_specs=pl.BlockSpec((CH, D), lambda i: (i, 0)),
        out_shape=jax.ShapeDtypeStruct((t, D), F32),
        scratch_shapes=[pltpu.VMEM((CH, D), F32), pltpu.SemaphoreType.DMA(())],
        compiler_params=_cparams(("arbitrary",)),
        name="final",
    )(pos3, ys, x1, mod3, g_final)


def _rope_tables(seq):
    inv = ROPE_THETA ** (-jnp.arange(0, HD, 2, dtype=F32) / HD)
    ang = jnp.arange(seq, dtype=F32)[:, None] * inv[None, :]
    ang = jnp.concatenate([ang, ang], axis=-1)
    cos, sin = jnp.cos(ang), jnp.sin(ang)
    sin = jnp.concatenate([-sin[:, :HD // 2], sin[:, HD // 2:]], axis=-1)
    return jnp.tile(cos, (1, 2)), jnp.tile(sin, (1, 2))


def kernel(x, c, w_ada, b_ada, g_mix, w_in, diff_lambda, g_diff_sub, swa_sinks, w_out, g_ffn,
           w_route_group, b_route_group, w_route_expert, b_route_expert, w_gate, w_up, w_down,
           g_final):
    bsz, seq, _ = x.shape
    t = bsz * seq
    x2 = x.reshape(t, D)
    cos, sin = _rope_tables(seq)

    mod3 = _ada_mod(c, w_ada[0], b_ada[0].reshape(1, -1)).reshape(bsz, 6, D)

    dq, dk, dv, sq, sk, sv = _in_proj(x2, mod3, g_mix[0].reshape(1, D), w_in[0].astype(BF16),
                                      cos, sin, seq)
    o_diff = _diff_attn(dq, dk, dv, diff_lambda[0], g_diff_sub[0].reshape(1, LANES), bsz, seq)
    o_swa = _swa_attn(sq, sk, sv, swa_sinks[0].reshape(1, 8), bsz, seq)

    n_r = N_GROUPS + N_GROUPS * EPG
    w_r = jnp.concatenate([w_route_group[0], w_route_expert[0],
                           jnp.zeros((D, LANES - n_r), F32)], axis=1).astype(BF16)
    b_r = jnp.concatenate([b_route_group[0], b_route_expert[0],
                           jnp.zeros((LANES - n_r,), F32)]).reshape(1, LANES)
    tri = (jnp.arange(TM_PROJ)[:, None] < jnp.arange(TM_PROJ)[None, :]).astype(BF16)
    x1, h2aug, bucket, rank, counts = _out_proj(o_diff, o_swa, x2, mod3, g_ffn[0].reshape(1, D),
                                                w_out[0].astype(BF16), w_r, b_r, tri, seq)

    n_tiles = t // TM_MOE + N_BUCKETS
    cnt = counts[:N_BUCKETS, 0].astype(I32)
    tiles_per = (cnt + TM_MOE - 1) // TM_MOE
    tile_end = jnp.cumsum(tiles_per)
    offsets = (tile_end - tiles_per) * TM_MOE
    pos = offsets[bucket.reshape(t)] + rank.reshape(t)
    pos3 = pos.reshape(t // CH, 1, CH)
    tid = jnp.arange(n_tiles, dtype=I32)
    tile_bucket = jnp.minimum(jnp.sum(tid[:, None] >= tile_end[None, :], axis=1), N_BUCKETS - 1)
    tile_valid = (tid < tile_end[-1]).astype(I32)
    grp, pair = tile_bucket // 6, tile_bucket % 6
    tile_lo = (grp * EPG + jnp.asarray(PAIR_LO, I32)[pair]).astype(I32)
    tile_hi = (grp * EPG + jnp.asarray(PAIR_HI, I32)[pair]).astype(I32)

    xs = _dispatch(pos3, h2aug, n_tiles * TM_MOE)
    ys = _moe(tile_lo, tile_hi, tile_valid, xs, w_gate[0].astype(BF16), w_up[0].astype(BF16),
              w_down[0].astype(BF16))
    out = _final(pos3, ys, x1, mod3, g_final.reshape(1, D), seq)
    return out.reshape(bsz, seq, D)
```

```python
import functools
import math

import jax
import jax.numpy as jnp
from jax import lax
from jax.experimental import pallas as pl
from jax.experimental.pallas import tpu as pltpu

F32 = jnp.float32
BF16 = jnp.bfloat16
I32 = jnp.int32

D = 1024
HD = 64
EPS = 1e-6
ROPE_THETA = 10000.0
LAMBDA_INIT = 0.8 - 0.6 * math.exp(-0.3 * 0)
N_GROUPS = 4
EPG = 4
N_BUCKETS = 24
BUCKET_ROWS = 32
D_EXPERT = 512
AUG = 128

LANES = 128
VMEM_LIMIT = 48 * 1024 * 1024

TM_PROJ = 512
TQ = 512
BLK = 128
TM_MOE = 512
CH = 1024

PAIR_LO = (0, 0, 0, 1, 1, 2)
PAIR_HI = (1, 2, 3, 2, 3, 3)


def _cparams(sem):
    return pltpu.CompilerParams(dimension_semantics=sem, vmem_limit_bytes=VMEM_LIMIT)


def _ada_kernel(c_ref, w_ref, b_ref, o_ref):
    c = c_ref[...]
    ca = c * (1.0 / (1.0 + jnp.exp(-c)))
    o_ref[...] = jnp.dot(ca.astype(BF16), w_ref[...].astype(BF16),
                         preferred_element_type=F32) + b_ref[...]


def _ada_mod(c, w, b):
    bsz = c.shape[0]
    n = w.shape[1]
    tn = 512
    return pl.pallas_call(
        _ada_kernel,
        grid=(n // tn,),
        in_specs=[pl.BlockSpec((bsz, D), lambda j: (0, 0)),
                  pl.BlockSpec((D, tn), lambda j: (0, j)),
                  pl.BlockSpec((1, tn), lambda j: (0, j))],
        out_specs=pl.BlockSpec((bsz, tn), lambda j: (0, j)),
        out_shape=jax.ShapeDtypeStruct((bsz, n), F32),
        compiler_params=_cparams(("arbitrary",)),
        name="ada_mod",
    )(c, w, b)


def _rope(a, cos, sin, first_half):
    rolled = jnp.where(first_half, pltpu.roll(a, 96, 1), pltpu.roll(a, 32, 1))
    return a * cos + rolled * sin


def _inproj_kernel(x_ref, mod_ref, g_ref, w_ref, cos_ref, sin_ref,
                   dq_ref, dk_ref, dv_ref, sq_ref, sk_ref, sv_ref, h_scr):
    tm = x_ref.shape[0]
    x = x_ref[...]
    ms = jnp.mean(x * x, axis=-1, keepdims=True)
    y = x * lax.rsqrt(ms + EPS) * g_ref[...]
    m = mod_ref[0]
    h = y * (1.0 + m[1:2]) + m[0:1]
    h_scr[...] = h.astype(BF16)

    cos = cos_ref[...]
    sin = sin_ref[...]
    lane = lax.broadcasted_iota(I32, (tm, LANES), 1)
    first_half = (lane & 32) == 0
    low64 = lane < HD
    scale = HD ** -0.5

    def chunk(j):
        return jnp.dot(h_scr[...], w_ref[:, 256 * j:256 * (j + 1)], preferred_element_type=F32)

    def halves(a):
        return a[:, :LANES], a[:, LANES:]

    for j in range(2):
        for t, r in enumerate(halves(chunk(j))):
            c0 = 256 * j + LANES * t
            dq_ref[:, c0:c0 + LANES] = (_rope(r, cos, sin, first_half) * scale).astype(BF16)
    for j in range(2):
        for t, r in enumerate(halves(chunk(2 + j))):
            c0 = 256 * j + LANES * t
            dk_ref[:, c0:c0 + LANES] = _rope(r, cos, sin, first_half).astype(BF16)
    for j in range(2):
        dv_ref[:, 256 * j:256 * (j + 1)] = chunk(4 + j).astype(BF16)
    for j in range(2):
        for t, r in enumerate(halves(chunk(6 + j))):
            rp = _rope(r, cos, sin, first_half) * scale
            hd0 = 4 * j + 2 * t
            sq_ref[:, LANES * hd0:LANES * (hd0 + 1)] = jnp.where(low64, rp, 0.0).astype(BF16)
            sq_ref[:, LANES * (hd0 + 1):LANES * (hd0 + 2)] = jnp.where(
                low64, pltpu.roll(rp, HD, 1), 0.0).astype(BF16)
    kv = chunk(8)
    kk, vv = halves(kv)
    kr = _rope(kk, cos, sin, first_half)
    sk_ref[:, :LANES] = jnp.where(low64, kr, 0.0).astype(BF16)
    sk_ref[:, LANES:] = jnp.where(low64, pltpu.roll(kr, HD, 1), 0.0).astype(BF16)
    sv_ref[:, :LANES] = jnp.where(low64, vv, 0.0).astype(BF16)
    sv_ref[:, LANES:] = jnp.where(low64, pltpu.roll(vv, HD, 1), 0.0).astype(BF16)


def _in_proj(x2, mod3, g_mix, w_in_bf, cos, sin, seq):
    t = x2.shape[0]
    tm = TM_PROJ
    per_b = seq // tm
    n_in = w_in_bf.shape[1]
    row = lambda i: (i, 0)
    return pl.pallas_call(
        _inproj_kernel,
        grid=(t // tm,),
        in_specs=[pl.BlockSpec((tm, D), row),
                  pl.BlockSpec((1, 6, D), lambda i: (i // per_b, 0, 0)),
                  pl.BlockSpec((1, D), lambda i: (0, 0)),
                  pl.BlockSpec((D, n_in), lambda i: (0, 0)),
                  pl.BlockSpec((tm, LANES), lambda i: (i % per_b, 0)),
                  pl.BlockSpec((tm, LANES), lambda i: (i % per_b, 0))],
        out_specs=[pl.BlockSpec((tm, 512), row), pl.BlockSpec((tm, 512), row),
                   pl.BlockSpec((tm, 512), row), pl.BlockSpec((tm, 1024), row),
                   pl.BlockSpec((tm, 256), row), pl.BlockSpec((tm, 256), row)],
        out_shape=[jax.ShapeDtypeStruct((t, 512), BF16), jax.ShapeDtypeStruct((t, 512), BF16),
                   jax.ShapeDtypeStruct((t, 512), BF16), jax.ShapeDtypeStruct((t, 1024), BF16),
                   jax.ShapeDtypeStruct((t, 256), BF16), jax.ShapeDtypeStruct((t, 256), BF16)],
        scratch_shapes=[pltpu.VMEM((tm, D), BF16)],
        compiler_params=_cparams(("arbitrary",)),
        name="in_proj",
    )(x2, mod3, g_mix, w_in_bf, cos, sin)


_NT_DIMS = (((1,), (1,)), ((), ()))


def _diff_kernel(lam_ref, g_ref, q_ref, k_ref, v_ref, o_ref):
    seq = q_ref.shape[0]
    tq = TQ
    lp = lam_ref[...]
    lam = (jnp.exp(jnp.sum(lp[0:1] * lp[1:2], axis=-1, keepdims=True))
           - jnp.exp(jnp.sum(lp[2:3] * lp[3:4], axis=-1, keepdims=True)) + LAMBDA_INIT)
    lane = lax.broadcasted_iota(I32, (tq, LANES), 1)
    low64 = lane < HD
    causal = (lax.broadcasted_iota(I32, (tq, tq), 1) <= lax.broadcasted_iota(I32, (tq, tq), 0))
    gain = g_ref[...] * (1.0 - LAMBDA_INIT)

    for i in range(seq // tq):
        lo, hi = i * tq, (i + 1) * tq
        q = q_ref[lo:hi, :]
        zero = jnp.zeros_like(q)
        probs = []
        for qm in (jnp.where(low64, q, zero), jnp.where(low64, zero, q)):
            s_dg = lax.dot_general(qm, k_ref[lo:hi, :], _NT_DIMS, preferred_element_type=F32)
            s_dg = jnp.where(causal, s_dg, -jnp.inf)
            mx = jnp.max(s_dg, axis=-1, keepdims=True)
            if i > 0:
                s_off = lax.dot_general(qm, k_ref[0:lo, :], _NT_DIMS, preferred_element_type=F32)
                mx = jnp.maximum(mx, jnp.max(s_off, axis=-1, keepdims=True))
                p_off = jnp.exp(s_off - mx)
            p_dg = jnp.exp(s_dg - mx)
            l = jnp.sum(p_dg, axis=-1, keepdims=True)
            if i > 0:
                l = l + jnp.sum(p_off, axis=-1, keepdims=True)
                probs.append((p_off, p_dg, l))
            else:
                probs.append((None, p_dg, l))
        (p0_off, p0_dg, l0), (p1_off, p1_dg, l1) = probs
        c = lam * l0 / l1
        o = jnp.dot((p0_dg - c * p1_dg).astype(BF16), v_ref[lo:hi, :], preferred_element_type=F32)
        if i > 0:
            o = o + jnp.dot((p0_off - c * p1_off).astype(BF16), v_ref[0:lo, :],
                            preferred_element_type=F32)
        o = o / l0
        ms = jnp.mean(o * o, axis=-1, keepdims=True)
        o_ref[lo:hi, :] = (o * lax.rsqrt(ms + EPS) * gain).astype(BF16)


def _diff_attn(dq, dk, dv, lam_p, g_sub, bsz, seq):
    t = dq.shape[0]
    blk = lambda b, h: (b, h)
    return pl.pallas_call(
        _diff_kernel,
        grid=(bsz, 4),
        in_specs=[pl.BlockSpec((4, HD), lambda b, h: (0, 0)),
                  pl.BlockSpec((1, LANES), lambda b, h: (0, 0)),
                  pl.BlockSpec((seq, LANES), blk),
                  pl.BlockSpec((seq, LANES), blk),
                  pl.BlockSpec((seq, LANES), blk)],
        out_specs=pl.BlockSpec((seq, LANES), blk),
        out_shape=jax.ShapeDtypeStruct((t, 512), BF16),
        compiler_params=_cparams(("arbitrary", "arbitrary")),
        name="diff_attn",
    )(lam_p, g_sub, dq, dk, dv)


def _swa_kernel(sink_ref, q_ref, k_ref, v_ref, o_ref):
    seq = q_ref.shape[0]
    g = pl.program_id(1)
    rows = 4 * BLK
    hsel = jnp.right_shift(lax.broadcasted_iota(I32, (rows, 8), 0), 7) + 4 * g
    sink_col = jnp.sum(jnp.where(lax.broadcasted_iota(I32, (rows, 8), 1) == hsel,
                                 sink_ref[...], 0.0), axis=-1, keepdims=True)
    qi = lax.broadcasted_iota(I32, (rows, 2 * BLK), 0) & (BLK - 1)
    kj = lax.broadcasted_iota(I32, (rows, 2 * BLK), 1)
    dist = qi + BLK - kj
    band = (dist >= 0) & (dist < BLK)
    first = (lax.broadcasted_iota(I32, (rows, BLK), 1)
             <= (lax.broadcasted_iota(I32, (rows, BLK), 0) & (BLK - 1)))
    low64 = lax.broadcasted_iota(I32, (BLK, LANES), 1) < HD

    def attend(q_rows, k_rows, mask):
        qs = jnp.concatenate([q_ref[q_rows, LANES * j:LANES * (j + 1)] for j in range(4)], axis=0)
        s = lax.dot_general(qs, k_ref[k_rows, :], _NT_DIMS, preferred_element_type=F32)
        s = jnp.where(mask, s, -jnp.inf)
        mx = jnp.maximum(jnp.max(s, axis=-1, keepdims=True), sink_col)
        p = jnp.exp(s - mx)
        l = jnp.sum(p, axis=-1, keepdims=True) + jnp.exp(sink_col - mx)
        o = jnp.dot(p.astype(BF16), v_ref[k_rows, :], preferred_element_type=F32) / l
        for c in range(2):
            even = o[(2 * c) * BLK:(2 * c + 1) * BLK, :]
            odd = o[(2 * c + 1) * BLK:(2 * c + 2) * BLK, :]
            o_ref[q_rows, LANES * c:LANES * (c + 1)] = jnp.where(
                low64, even, pltpu.roll(odd, HD, 1)).astype(BF16)

    attend(pl.ds(0, BLK), pl.ds(0, BLK), first)

    def body(n, carry):
        q0 = pl.multiple_of(n * BLK, BLK)
        attend(pl.ds(q0, BLK), pl.ds(pl.multiple_of(q0 - BLK, BLK), 2 * BLK), band)
        return carry

    lax.fori_loop(1, seq // BLK, body, 0)


def _swa_attn(sq, sk, sv, sinks, bsz, seq):
    t = sq.shape[0]
    return pl.pallas_call(
        _swa_kernel,
        grid=(bsz, 2),
        in_specs=[pl.BlockSpec((1, 8), lambda b, g: (0, 0)),
                  pl.BlockSpec((seq, 512), lambda b, g: (b, g)),
                  pl.BlockSpec((seq, LANES), lambda b, g: (b, g)),
                  pl.BlockSpec((seq, LANES), lambda b, g: (b, g))],
        out_specs=pl.BlockSpec((seq, 256), lambda b, g: (b, g)),
        out_shape=jax.ShapeDtypeStruct((t, 512), BF16),
        compiler_params=_cparams(("arbitrary", "arbitrary")),
        name="swa_attn",
    )(sinks, sq, sk, sv)


def _outproj_kernel(od_ref, os_ref, x_ref, mod_ref, g_ref, wo_ref, wr_ref, br_ref, tri_ref,
                    x1_ref, h2_ref, bucket_ref, rank_ref, cnt_ref, base_scr):
    tm = x_ref.shape[0]

    @pl.when(pl.program_id(0) == 0)
    def _():
        base_scr[...] = jnp.zeros_like(base_scr)

    m = mod_ref[0]
    mix = (jnp.dot(od_ref[...], wo_ref[0:512, :], preferred_element_type=F32)
           + jnp.dot(os_ref[...], wo_ref[512:1024, :], preferred_element_type=F32))
    x1 = x_ref[...] + m[2:3] * mix
    x1_ref[...] = x1
    ms = jnp.mean(x1 * x1, axis=-1, keepdims=True)
    h2 = x1 * lax.rsqrt(ms + EPS) * g_ref[...] * (1.0 + m[4:5]) + m[3:4]
    h2_ref[:, 0:D] = h2

    logits = jnp.dot(h2.astype(BF16), wr_ref[...], preferred_element_type=F32) + br_ref[...]
    lt = logits.T
    r = [lt[i:i + 1, :] for i in range(N_GROUPS + N_GROUPS * EPG)]

    gl = r[0:N_GROUPS]
    gmax = jnp.maximum(jnp.maximum(gl[0], gl[1]), jnp.maximum(gl[2], gl[3]))
    gidx = jnp.where(gl[0] == gmax, 0, jnp.where(gl[1] == gmax, 1, jnp.where(gl[2] == gmax, 2, 3)))
    gz = (jnp.exp(gl[0] - gmax) + jnp.exp(gl[1] - gmax)
          + jnp.exp(gl[2] - gmax) + jnp.exp(gl[3] - gmax))
    grp_p = 1.0 / gz

    el = []
    for i in range(EPG):
        e = r[N_GROUPS + 3 * EPG + i]
        for gg in (2, 1, 0):
            e = jnp.where(gidx == gg, r[N_GROUPS + gg * EPG + i], e)
        el.append(e)
    emax = jnp.maximum(jnp.maximum(el[0], el[1]), jnp.maximum(el[2], el[3]))
    ex = [jnp.exp(e - emax) for e in el]
    ez = ex[0] + ex[1] + ex[2] + ex[3]
    pr = [e / ez for e in ex]
    p1 = jnp.maximum(jnp.maximum(pr[0], pr[1]), jnp.maximum(pr[2], pr[3]))
    a = jnp.where(pr[0] == p1, 0, jnp.where(pr[1] == p1, 1, jnp.where(pr[2] == p1, 2, 3)))
    rest = [jnp.where(a == i, -1.0, pr[i]) for i in range(EPG)]
    p2 = jnp.maximum(jnp.maximum(rest[0], rest[1]), jnp.maximum(rest[2], rest[3]))
    b = jnp.where(rest[0] == p2, 0, jnp.where(rest[1] == p2, 1, jnp.where(rest[2] == p2, 2, 3)))
    psum = p1 + p2
    wa = grp_p * (p1 / psum)
    wb = grp_p * (p2 / psum)
    lo = jnp.minimum(a, b)
    hi = jnp.maximum(a, b)
    w_lo = jnp.where(a < b, wa, wb)
    w_hi = jnp.where(a < b, wb, wa)
    pair = jnp.where(lo == 0, hi - 1, jnp.where(lo == 1, hi + 1, 5))
    bucket = gidx * 6 + pair

    rid = lax.broadcasted_iota(I32, (AUG, tm), 0)
    h2_ref[:, D:D + AUG] = jnp.where(rid == 0, w_lo, jnp.where(rid == 1, w_hi, 0.0)).T

    oh = (lax.broadcasted_iota(I32, (BUCKET_ROWS, tm), 0) == bucket)
    oh_f = jnp.where(oh, 1.0, 0.0)
    before = jnp.dot(oh_f.astype(BF16), tri_ref[...], preferred_element_type=F32)
    base = base_scr[:, 0:1]
    rank = jnp.sum(oh_f * (base + before), axis=0, keepdims=True)
    new_base = base_scr[...] + jnp.sum(oh_f, axis=1, keepdims=True)
    base_scr[...] = new_base
    cnt_ref[...] = new_base
    bucket_ref[0] = bucket
    rank_ref[0] = rank.astype(I32)


def _out_proj(o_diff, o_swa, x2, mod3, g_ffn, w_out_bf, w_r, b_r, tri, seq):
    t = x2.shape[0]
    tm = TM_PROJ
    per_b = seq // tm
    row = lambda i: (i, 0)
    const = lambda i: (0, 0)
    return pl.pallas_call(
        _outproj_kernel,
        grid=(t // tm,),
        in_specs=[pl.BlockSpec((tm, 512), row), pl.BlockSpec((tm, 512), row),
                  pl.BlockSpec((tm, D), row),
                  pl.BlockSpec((1, 6, D), lambda i: (i // per_b, 0, 0)),
                  pl.BlockSpec((1, D), const),
                  pl.BlockSpec((D, D), const),
                  pl.BlockSpec((D, LANES), const),
                  pl.BlockSpec((1, LANES), const),
                  pl.BlockSpec((tm, tm), const)],
        out_specs=[pl.BlockSpec((tm, D), row),
                   pl.BlockSpec((tm, D + AUG), row),
                   pl.BlockSpec((1, 1, tm), lambda i: (i, 0, 0)),
                   pl.BlockSpec((1, 1, tm), lambda i: (i, 0, 0)),
                   pl.BlockSpec((BUCKET_ROWS, LANES), const)],
        out_shape=[jax.ShapeDtypeStruct((t, D), F32),
                   jax.ShapeDtypeStruct((t, D + AUG), F32),
                   jax.ShapeDtypeStruct((t // tm, 1, tm), I32),
                   jax.ShapeDtypeStruct((t // tm, 1, tm), I32),
                   jax.ShapeDtypeStruct((BUCKET_ROWS, LANES), F32)],
        scratch_shapes=[pltpu.VMEM((BUCKET_ROWS, LANES), F32)],
        compiler_params=_cparams(("arbitrary",)),
        name="out_proj",
    )(o_diff, o_swa, x2, mod3, g_ffn, w_out_bf, w_r, b_r, tri)


def _row_copy(src_ref, dst_ref, src_row, dst_row, sem):
    return pltpu.make_async_copy(src_ref.at[pl.ds(src_row, 1)], dst_ref.at[pl.ds(dst_row, 1)], sem)


def _dispatch_kernel(pos_ref, src_ref, init_hbm, dst_hbm, sem):
    del init_hbm

    def start(r, carry):
        _row_copy(src_ref, dst_hbm, r, pos_ref[0, 0, r], sem).start()
        return carry

    lax.fori_loop(0, CH, start, 0, unroll=8)

    def wait(r, carry):
        _row_copy(src_ref, dst_hbm, 0, 0, sem).wait()
        return carry

    lax.fori_loop(0, CH, wait, 0, unroll=8)


def _dispatch(pos3, h2aug, n_rows):
    t, width = h2aug.shape
    zeros = jnp.zeros((n_rows, width), F32)
    return pl.pallas_call(
        _dispatch_kernel,
        grid=(t // CH,),
        in_specs=[pl.BlockSpec((1, 1, CH), lambda i: (i, 0, 0), memory_space=pltpu.SMEM),
                  pl.BlockSpec((CH, width), lambda i: (i, 0)),
                  pl.BlockSpec(memory_space=pl.ANY)],
        out_specs=pl.BlockSpec(memory_space=pl.ANY),
        out_shape=jax.ShapeDtypeStruct((n_rows, width), F32),
        scratch_shapes=[pltpu.SemaphoreType.DMA(())],
        input_output_aliases={2: 0},
        compiler_params=_cparams(("arbitrary",)),
        name="dispatch",
    )(pos3, h2aug, zeros)


def _moe_kernel(lo_ref, hi_ref, valid_ref, xs_ref, wg_lo, wu_lo, wd_lo, wg_hi, wu_hi, wd_hi, y_ref):
    del lo_ref, hi_ref
    t = pl.program_id(0)

    @pl.when(valid_ref[t] != 0)
    def _():
        xb = xs_ref[:, 0:D].astype(BF16)
        y = None
        for k, (wg, wu, wd) in enumerate(((wg_lo, wu_lo, wd_lo), (wg_hi, wu_hi, wd_hi))):
            gt = jnp.dot(xb, wg[0], preferred_element_type=F32)
            up = jnp.dot(xb, wu[0], preferred_element_type=F32)
            act = (gt * (1.0 / (1.0 + jnp.exp(-gt))) * up).astype(BF16)
            dn = jnp.dot(act, wd[0], preferred_element_type=F32)
            term = xs_ref[:, D + k:D + k + 1] * dn
            y = term if y is None else y + term
        y_ref[...] = y

    @pl.when(valid_ref[t] == 0)
    def _():
        y_ref[...] = jnp.zeros_like(y_ref)


def _moe(tile_lo, tile_hi, tile_valid, xs, wg, wu, wd):
    n_rows, width = xs.shape
    n_tiles = n_rows // TM_MOE
    up_lo = lambda t, lo, hi, v: (lo[t], 0, 0)
    up_hi = lambda t, lo, hi, v: (hi[t], 0, 0)
    grid_spec = pltpu.PrefetchScalarGridSpec(
        num_scalar_prefetch=3,
        grid=(n_tiles,),
        in_specs=[pl.BlockSpec((TM_MOE, width), lambda t, lo, hi, v: (t, 0)),
                  pl.BlockSpec((1, D, D_EXPERT), up_lo), pl.BlockSpec((1, D, D_EXPERT), up_lo),
                  pl.BlockSpec((1, D_EXPERT, D), up_lo),
                  pl.BlockSpec((1, D, D_EXPERT), up_hi), pl.BlockSpec((1, D, D_EXPERT), up_hi),
                  pl.BlockSpec((1, D_EXPERT, D), up_hi)],
        out_specs=pl.BlockSpec((TM_MOE, D), lambda t, lo, hi, v: (t, 0)),
    )
    return pl.pallas_call(
        _moe_kernel,
        grid_spec=grid_spec,
        out_shape=jax.ShapeDtypeStruct((n_rows, D), F32),
        compiler_params=_cparams(("arbitrary",)),
        name="moe",
    )(tile_lo, tile_hi, tile_valid, xs, wg, wu, wd, wg, wu, wd)


def _final_kernel(pos_ref, ys_hbm, x1_ref, mod_ref, g_ref, o_ref, y_scr, sem):
    def start(r, carry):
        pltpu.make_async_copy(ys_hbm.at[pl.ds(pos_ref[0, 0, r], 1)], y_scr.at[pl.ds(r, 1)], sem).start()
        return carry

    lax.fori_loop(0, CH, start, 0, unroll=8)

    def wait(r, carry):
        pltpu.make_async_copy(ys_hbm.at[pl.ds(0, 1)], y_scr.at[pl.ds(0, 1)], sem).wait()
        return carry

    lax.fori_loop(0, CH, wait, 0, unroll=8)

    m = mod_ref[0]
    x2 = x1_ref[...] + m[5:6] * y_scr[...]
    ms = jnp.mean(x2 * x2, axis=-1, keepdims=True)
    o_ref[...] = x2 * lax.rsqrt(ms + EPS) * g_ref[...]


def _final(pos3, ys, x1, mod3, g_final, seq):
    t = x1.shape[0]
    per_b = seq // CH
    return pl.pallas_call(
        _final_kernel,
        grid=(t // CH,),
        in_specs=[pl.BlockSpec((1, 1, CH), lambda i: (i, 0, 0), memory_space=pltpu.SMEM),
                  pl.BlockSpec(memory_space=pl.ANY),
                  pl.BlockSpec((CH, D), lambda i: (i, 0)),
                  pl.BlockSpec((1, 6, D), lambda i: (i // per_b, 0, 0)),
                  pl.BlockSpec((1, D), lambda i: (0, 0))],
        out_specs=pl.BlockSpec((CH, D), lambda i: (i, 0)),
        out_shape=jax.ShapeDtypeStruct((t, D), F32),
        scratch_shapes=[pltpu.VMEM((CH, D), F32), pltpu.SemaphoreType.DMA(())],
        compiler_params=_cparams(("arbitrary",)),
        name="final",
    )(pos3, ys, x1, mod3, g_final)


def _rope_tables(seq):
    inv = ROPE_THETA ** (-jnp.arange(0, HD, 2, dtype=F32) / HD)
    ang = jnp.arange(seq, dtype=F32)[:, None] * inv[None, :]
    ang = jnp.concatenate([ang, ang], axis=-1)
    cos, sin = jnp.cos(ang), jnp.sin(ang)
    sin = jnp.concatenate([-sin[:, :HD // 2], sin[:, HD // 2:]], axis=-1)
    return jnp.tile(cos, (1, 2)), jnp.tile(sin, (1, 2))


def kernel(x, c, w_ada, b_ada, g_mix, w_in, diff_lambda, g_diff_sub, swa_sinks, w_out, g_ffn,
           w_route_group, b_route_group, w_route_expert, b_route_expert, w_gate, w_up, w_down,
           g_final):
    bsz, seq, _ = x.shape
    t = bsz * seq
    x2 = x.reshape(t, D)
    cos, sin = _rope_tables(seq)

    mod3 = _ada_mod(c, w_ada[0], b_ada[0].reshape(1, -1)).reshape(bsz, 6, D)

    dq, dk, dv, sq, sk, sv = _in_proj(x2, mod3, g_mix[0].reshape(1, D), w_in[0].astype(BF16),
                                      cos, sin, seq)
    o_diff = _diff_attn(dq, dk, dv, diff_lambda[0], g_diff_sub[0].reshape(1, LANES), bsz, seq)
    o_swa = _swa_attn(sq, sk, sv, swa_sinks[0].reshape(1, 8), bsz, seq)

    n_r = N_GROUPS + N_GROUPS * EPG
    w_r = jnp.concatenate([w_route_group[0], w_route_expert[0],
                           jnp.zeros((D, LANES - n_r), F32)], axis=1).astype(BF16)
    b_r = jnp.concatenate([b_route_group[0], b_route_expert[0],
                           jnp.zeros((LANES - n_r,), F32)]).reshape(1, LANES)
    tri = (jnp.arange(TM_PROJ)[:, None] < jnp.arange(TM_PROJ)[None, :]).astype(BF16)
    x1, h2aug, bucket, rank, counts = _out_proj(o_diff, o_swa, x2, mod3, g_ffn[0].reshape(1, D),
                                                w_out[0].astype(BF16), w_r, b_r, tri, seq)

    n_tiles = t // TM_MOE + N_BUCKETS
    cnt = counts[:N_BUCKETS, 0].astype(I32)
    tiles_per = (cnt + TM_MOE - 1) // TM_MOE
    tile_end = jnp.cumsum(tiles_per)
    offsets = (tile_end - tiles_per) * TM_MOE
    pos = offsets[bucket.reshape(t)] + rank.reshape(t)
    pos3 = pos.reshape(t // CH, 1, CH)
    tid = jnp.arange(n_tiles, dtype=I32)
    tile_bucket = jnp.minimum(jnp.sum(tid[:, None] >= tile_end[None, :], axis=1), N_BUCKETS - 1)
    tile_valid = (tid < tile_end[-1]).astype(I32)
    grp, pair = tile_bucket // 6, tile_bucket % 6
    tile_lo = (grp * EPG + jnp.asarray(PAIR_LO, I32)[pair]).astype(I32)
    tile_hi = (grp * EPG + jnp.asarray(PAIR_HI, I32)[pair]).astype(I32)

    xs = _dispatch(pos3, h2aug, n_tiles * TM_MOE)
    ys = _moe(tile_lo, tile_hi, tile_valid, xs, w_gate[0].astype(BF16), w_up[0].astype(BF16),
              w_down[0].astype(BF16))
    out = _final(pos3, ys, x1, mod3, g_final.reshape(1, D), seq)
    return out.reshape(bsz, seq, D)
```

```python
import functools
import math

import jax
import jax.numpy as jnp
import numpy as np
from jax import lax
from jax.experimental import pallas as pl
from jax.experimental.pallas import tpu as pltpu

F32 = jnp.float32
BF16 = jnp.bfloat16
I32 = jnp.int32

D = 1024
HD = 64
EPS = 1e-6
ROPE_THETA = 10000.0
LOG2E = math.log2(math.e)
LAMBDA_INIT = 0.8 - 0.6 * math.exp(-0.3 * 0)
N_GROUPS = 4
EPG = 4
N_BUCKETS = 24
BUCKET_ROWS = 32
D_EXPERT = 512
AUG = 128

LANES = 128
VMEM_LIMIT = 48 * 1024 * 1024

TM_PROJ = 512
TQ = 512
BLK = 128
TM_MOE = 512
CH = 1024

PAIR_LO = (0, 0, 0, 1, 1, 2)
PAIR_HI = (1, 2, 3, 2, 3, 3)


def _cparams(sem):
    return pltpu.CompilerParams(dimension_semantics=sem, vmem_limit_bytes=VMEM_LIMIT)


def _ada_kernel(c_ref, w_ref, b_ref, o_ref):
    c = c_ref[...]
    ca = c * (1.0 / (1.0 + jnp.exp(-c)))
    o_ref[...] = jnp.dot(ca.astype(BF16), w_ref[...].astype(BF16),
                         preferred_element_type=F32) + b_ref[...]


def _ada_mod(c, w, b):
    bsz = c.shape[0]
    n = w.shape[1]
    tn = 512
    return pl.pallas_call(
        _ada_kernel,
        grid=(n // tn,),
        in_specs=[pl.BlockSpec((bsz, D), lambda j: (0, 0)),
                  pl.BlockSpec((D, tn), lambda j: (0, j)),
                  pl.BlockSpec((1, tn), lambda j: (0, j))],
        out_specs=pl.BlockSpec((bsz, tn), lambda j: (0, j)),
        out_shape=jax.ShapeDtypeStruct((bsz, n), F32),
        compiler_params=_cparams(("arbitrary",)),
        name="ada_mod",
    )(c, w, b)


def _rope(a, cos, sin):
    return a * cos + pltpu.roll(a, HD, 1) * sin


def _inproj_kernel(x_ref, mod_ref, g_ref, w_ref, cos_ref, sin_ref,
                   dq_ref, dk_ref, dv_ref, sq_ref, sk_ref, sv_ref, h_scr):
    tm = x_ref.shape[0]
    x = x_ref[...]
    ms = jnp.mean(x * x, axis=-1, keepdims=True)
    y = x * lax.rsqrt(ms + EPS) * g_ref[...]
    m = mod_ref[0]
    h = y * (1.0 + m[1:2]) + m[0:1]
    h_scr[...] = h.astype(BF16)

    cos = cos_ref[...]
    sin = sin_ref[...]
    lane = lax.broadcasted_iota(I32, (tm, LANES), 1)
    head_a = (lane & 32) == 0
    low64 = lane < HD
    scale = (HD ** -0.5) * LOG2E

    def chunk(j):
        return jnp.dot(h_scr[...], w_ref[:, 256 * j:256 * (j + 1)], preferred_element_type=F32)

    def halves(a):
        return a[:, :LANES], a[:, LANES:]

    for j in range(2):
        for t, r in enumerate(halves(chunk(j))):
            c0 = 256 * j + LANES * t
            dq_ref[:, c0:c0 + LANES] = (_rope(r, cos, sin) * scale).astype(BF16)
    for j in range(2):
        for t, r in enumerate(halves(chunk(2 + j))):
            c0 = 256 * j + LANES * t
            dk_ref[:, c0:c0 + LANES] = _rope(r, cos, sin).astype(BF16)
    for j in range(2):
        dv_ref[:, 256 * j:256 * (j + 1)] = chunk(4 + j).astype(BF16)
    for j in range(2):
        for t, r in enumerate(halves(chunk(6 + j))):
            rp = _rope(r, cos, sin) * scale
            c = 2 * j + t
            sq_ref[:, LANES * c:LANES * (c + 1)] = jnp.where(head_a, rp, 0.0).astype(BF16)
            sq_ref[:, LANES * (c + 4):LANES * (c + 5)] = jnp.where(head_a, 0.0, rp).astype(BF16)
    kv = chunk(8)
    kk, vv = halves(kv)
    kr = _rope(kk, cos, sin)
    sk_ref[:, :LANES] = jnp.where(head_a, kr, 0.0).astype(BF16)
    sk_ref[:, LANES:] = jnp.where(head_a, 0.0, kr).astype(BF16)
    sv_ref[:, :LANES] = jnp.where(low64, vv, 0.0).astype(BF16)
    sv_ref[:, LANES:] = jnp.where(low64, pltpu.roll(vv, HD, 1), 0.0).astype(BF16)


def _in_proj(x2, mod3, g_mix, w_in_bf, cos, sin, seq):
    t = x2.shape[0]
    tm = TM_PROJ
    per_b = seq // tm
    n_in = w_in_bf.shape[1]
    row = lambda i: (i, 0)
    return pl.pallas_call(
        _inproj_kernel,
        grid=(t // tm,),
        in_specs=[pl.BlockSpec((tm, D), row),
                  pl.BlockSpec((1, 6, D), lambda i: (i // per_b, 0, 0)),
                  pl.BlockSpec((1, D), lambda i: (0, 0)),
                  pl.BlockSpec((D, n_in), lambda i: (0, 0)),
                  pl.BlockSpec((tm, LANES), lambda i: (i % per_b, 0)),
                  pl.BlockSpec((tm, LANES), lambda i: (i % per_b, 0))],
        out_specs=[pl.BlockSpec((tm, 512), row), pl.BlockSpec((tm, 512), row),
                   pl.BlockSpec((tm, 512), row), pl.BlockSpec((tm, 1024), row),
                   pl.BlockSpec((tm, 256), row), pl.BlockSpec((tm, 256), row)],
        out_shape=[jax.ShapeDtypeStruct((t, 512), BF16), jax.ShapeDtypeStruct((t, 512), BF16),
                   jax.ShapeDtypeStruct((t, 512), BF16), jax.ShapeDtypeStruct((t, 1024), BF16),
                   jax.ShapeDtypeStruct((t, 256), BF16), jax.ShapeDtypeStruct((t, 256), BF16)],
        scratch_shapes=[pltpu.VMEM((tm, D), BF16)],
        compiler_params=_cparams(("arbitrary",)),
        name="in_proj",
    )(x2, mod3, g_mix, w_in_bf, cos, sin)


_NT_DIMS = (((1,), (1,)), ((), ()))


def _diff_kernel(lam_ref, g_ref, q_ref, k_ref, v_ref, o_ref):
    seq = q_ref.shape[0]
    tq = TQ
    lp = lam_ref[...]
    lam = (jnp.exp(jnp.sum(lp[0:1] * lp[1:2], axis=-1, keepdims=True))
           - jnp.exp(jnp.sum(lp[2:3] * lp[3:4], axis=-1, keepdims=True)) + LAMBDA_INIT)
    lane = lax.broadcasted_iota(I32, (tq, LANES), 1)
    map_a = (lane & 32) == 0
    causal = (lax.broadcasted_iota(I32, (tq, tq), 1) <= lax.broadcasted_iota(I32, (tq, tq), 0))
    gain = g_ref[...] * (1.0 - LAMBDA_INIT)

    for i in reversed(range(seq // tq)):
        lo, hi = i * tq, (i + 1) * tq
        q = q_ref[lo:hi, :]
        zero = jnp.zeros_like(q)
        probs = []
        for qm in (jnp.where(map_a, q, zero), jnp.where(map_a, zero, q)):
            s_dg = lax.dot_general(qm, k_ref[lo:hi, :], _NT_DIMS, preferred_element_type=F32)
            s_dg = jnp.where(causal, s_dg, -jnp.inf)
            mx = jnp.max(s_dg, axis=-1, keepdims=True)
            if i > 0:
                s_off = lax.dot_general(qm, k_ref[0:lo, :], _NT_DIMS, preferred_element_type=F32)
                mx = jnp.maximum(mx, jnp.max(s_off, axis=-1, keepdims=True))
                p_off = jnp.exp2(s_off - mx)
            p_dg = jnp.exp2(s_dg - mx)
            l = jnp.sum(p_dg, axis=-1, keepdims=True)
            if i > 0:
                l = l + jnp.sum(p_off, axis=-1, keepdims=True)
                probs.append((p_off, p_dg, l))
            else:
                probs.append((None, p_dg, l))
        (p0_off, p0_dg, l0), (p1_off, p1_dg, l1) = probs
        c = lam * l0 / l1
        o = jnp.dot((p0_dg - c * p1_dg).astype(BF16), v_ref[lo:hi, :], preferred_element_type=F32)
        if i > 0:
            o = o + jnp.dot((p0_off - c * p1_off).astype(BF16), v_ref[0:lo, :],
                            preferred_element_type=F32)
        o = o / l0
        ms = jnp.mean(o * o, axis=-1, keepdims=True)
        o_ref[lo:hi, :] = (o * lax.rsqrt(ms + EPS) * gain).astype(BF16)


def _diff_attn(dq, dk, dv, lam_p, g_sub, bsz, seq):
    t = dq.shape[0]
    blk = lambda b, h: (b, h)
    return pl.pallas_call(
        _diff_kernel,
        grid=(bsz, 4),
        in_specs=[pl.BlockSpec((4, HD), lambda b, h: (0, 0)),
                  pl.BlockSpec((1, LANES), lambda b, h: (0, 0)),
                  pl.BlockSpec((seq, LANES), blk),
                  pl.BlockSpec((seq, LANES), blk),
                  pl.BlockSpec((seq, LANES), blk)],
        out_specs=pl.BlockSpec((seq, LANES), blk),
        out_shape=jax.ShapeDtypeStruct((t, 512), BF16),
        compiler_params=_cparams(("arbitrary", "arbitrary")),
        name="diff_attn",
    )(lam_p, g_sub, dq, dk, dv)


def _swa_kernel(sink_ref, q_ref, k_ref, v_ref, o_ref):
    seq = q_ref.shape[0]
    g = pl.program_id(1)
    rows = 4 * BLK
    hsel = jnp.right_shift(lax.broadcasted_iota(I32, (rows, 8), 0), 7) + 4 * g
    sink_col = jnp.sum(jnp.where(lax.broadcasted_iota(I32, (rows, 8), 1) == hsel,
                                 sink_ref[...] * LOG2E, 0.0), axis=-1, keepdims=True)
    qi = lax.broadcasted_iota(I32, (rows, 2 * BLK), 0) & (BLK - 1)
    kj = lax.broadcasted_iota(I32, (rows, 2 * BLK), 1)
    dist = qi + BLK - kj
    band = (dist >= 0) & (dist < BLK)
    first = (lax.broadcasted_iota(I32, (rows, BLK), 1)
             <= (lax.broadcasted_iota(I32, (rows, BLK), 0) & (BLK - 1)))
    low64 = lax.broadcasted_iota(I32, (BLK, LANES), 1) < HD

    def attend(q_rows, k_rows, mask):
        qs = jnp.concatenate([q_ref[q_rows, LANES * j:LANES * (j + 1)] for j in range(4)], axis=0)
        s = lax.dot_general(qs, k_ref[k_rows, :], _NT_DIMS, preferred_element_type=F32)
        s = jnp.where(mask, s, -jnp.inf)
        mx = jnp.maximum(jnp.max(s, axis=-1, keepdims=True), sink_col)
        p = jnp.exp2(s - mx)
        l = jnp.sum(p, axis=-1, keepdims=True) + jnp.exp2(sink_col - mx)
        o = jnp.dot(p.astype(BF16), v_ref[k_rows, :], preferred_element_type=F32) / l
        for c in range(2):
            even = o[(2 * c) * BLK:(2 * c + 1) * BLK, :]
            odd = o[(2 * c + 1) * BLK:(2 * c + 2) * BLK, :]
            o_ref[q_rows, LANES * c:LANES * (c + 1)] = jnp.where(
                low64, even, pltpu.roll(odd, HD, 1)).astype(BF16)

    attend(pl.ds(0, BLK), pl.ds(0, BLK), first)

    for n in range(1, seq // BLK):
        attend(pl.ds(n * BLK, BLK), pl.ds((n - 1) * BLK, 2 * BLK), band)


def _swa_attn(sq, sk, sv, sinks, bsz, seq):
    t = sq.shape[0]
    return pl.pallas_call(
        _swa_kernel,
        grid=(bsz, 2),
        in_specs=[pl.BlockSpec((1, 8), lambda b, g: (0, 0)),
                  pl.BlockSpec((seq, 512), lambda b, g: (b, g)),
                  pl.BlockSpec((seq, LANES), lambda b, g: (b, g)),
                  pl.BlockSpec((seq, LANES), lambda b, g: (b, g))],
        out_specs=pl.BlockSpec((seq, 256), lambda b, g: (b, g)),
        out_shape=jax.ShapeDtypeStruct((t, 512), BF16),
        compiler_params=_cparams(("arbitrary", "arbitrary")),
        name="swa_attn",
    )(sinks, sq, sk, sv)


def _outproj_kernel(od_ref, os_ref, x_ref, mod_ref, g_ref, wo_ref, wr_ref, br_ref, tri_ref,
                    x1_ref, h2_ref, bucket_ref, rank_ref, cnt_ref, base_scr):
    tm = x_ref.shape[0]

    @pl.when(pl.program_id(0) == 0)
    def _():
        base_scr[...] = jnp.zeros_like(base_scr)

    m = mod_ref[0]
    mix = (jnp.dot(od_ref[...], wo_ref[0:512, :], preferred_element_type=F32)
           + jnp.dot(os_ref[...], wo_ref[512:1024, :], preferred_element_type=F32))
    x1 = x_ref[...] + m[2:3] * mix
    x1_ref[...] = x1
    ms = jnp.mean(x1 * x1, axis=-1, keepdims=True)
    h2 = x1 * lax.rsqrt(ms + EPS) * g_ref[...] * (1.0 + m[4:5]) + m[3:4]
    h2_ref[:, 0:D] = h2

    logits = jnp.dot(h2.astype(BF16), wr_ref[...], preferred_element_type=F32) + br_ref[...]
    lt = logits.T
    r = [lt[i:i + 1, :] for i in range(N_GROUPS + N_GROUPS * EPG)]

    gl = r[0:N_GROUPS]
    gmax = jnp.maximum(jnp.maximum(gl[0], gl[1]), jnp.maximum(gl[2], gl[3]))
    gidx = jnp.where(gl[0] == gmax, 0, jnp.where(gl[1] == gmax, 1, jnp.where(gl[2] == gmax, 2, 3)))
    gz = (jnp.exp(gl[0] - gmax) + jnp.exp(gl[1] - gmax)
          + jnp.exp(gl[2] - gmax) + jnp.exp(gl[3] - gmax))
    grp_p = 1.0 / gz

    el = []
    for i in range(EPG):
        e = r[N_GROUPS + 3 * EPG + i]
        for gg in (2, 1, 0):
            e = jnp.where(gidx == gg, r[N_GROUPS + gg * EPG + i], e)
        el.append(e)
    emax = jnp.maximum(jnp.maximum(el[0], el[1]), jnp.maximum(el[2], el[3]))
    ex = [jnp.exp(e - emax) for e in el]
    ez = ex[0] + ex[1] + ex[2] + ex[3]
    pr = [e / ez for e in ex]
    p1 = jnp.maximum(jnp.maximum(pr[0], pr[1]), jnp.maximum(pr[2], pr[3]))
    a = jnp.where(pr[0] == p1, 0, jnp.where(pr[1] == p1, 1, jnp.where(pr[2] == p1, 2, 3)))
    rest = [jnp.where(a == i, -1.0, pr[i]) for i in range(EPG)]
    p2 = jnp.maximum(jnp.maximum(rest[0], rest[1]), jnp.maximum(rest[2], rest[3]))
    b = jnp.where(rest[0] == p2, 0, jnp.where(rest[1] == p2, 1, jnp.where(rest[2] == p2, 2, 3)))
    psum = p1 + p2
    wa = grp_p * (p1 / psum)
    wb = grp_p * (p2 / psum)
    lo = jnp.minimum(a, b)
    hi = jnp.maximum(a, b)
    w_lo = jnp.where(a < b, wa, wb)
    w_hi = jnp.where(a < b, wb, wa)
    pair = jnp.where(lo == 0, hi - 1, jnp.where(lo == 1, hi + 1, 5))
    bucket = gidx * 6 + pair

    rid = lax.broadcasted_iota(I32, (AUG, tm), 0)
    h2_ref[:, D:D + AUG] = jnp.where(rid == 0, w_lo, jnp.where(rid == 1, w_hi, 0.0)).T

    oh = (lax.broadcasted_iota(I32, (BUCKET_ROWS, tm), 0) == bucket)
    oh_f = jnp.where(oh, 1.0, 0.0)
    before = jnp.dot(oh_f.astype(BF16), tri_ref[...], preferred_element_type=F32)
    base = base_scr[:, 0:1]
    rank = jnp.sum(oh_f * (base + before), axis=0, keepdims=True)
    new_base = base_scr[...] + jnp.sum(oh_f, axis=1, keepdims=True)
    base_scr[...] = new_base
    cnt_ref[...] = new_base
    bucket_ref[0] = bucket
    rank_ref[0] = rank.astype(I32)


def _out_proj(o_diff, o_swa, x2, mod3, g_ffn, w_out_bf, w_r, b_r, tri, seq):
    t = x2.shape[0]
    tm = TM_PROJ
    per_b = seq // tm
    row = lambda i: (i, 0)
    const = lambda i: (0, 0)
    return pl.pallas_call(
        _outproj_kernel,
        grid=(t // tm,),
        in_specs=[pl.BlockSpec((tm, 512), row), pl.BlockSpec((tm, 512), row),
                  pl.BlockSpec((tm, D), row),
                  pl.BlockSpec((1, 6, D), lambda i: (i // per_b, 0, 0)),
                  pl.BlockSpec((1, D), const),
                  pl.BlockSpec((D, D), const),
                  pl.BlockSpec((D, LANES), const),
                  pl.BlockSpec((1, LANES), const),
                  pl.BlockSpec((tm, tm), const)],
        out_specs=[pl.BlockSpec((tm, D), row),
                   pl.BlockSpec((tm, D + AUG), row),
                   pl.BlockSpec((1, 1, tm), lambda i: (i, 0, 0)),
                   pl.BlockSpec((1, 1, tm), lambda i: (i, 0, 0)),
                   pl.BlockSpec((BUCKET_ROWS, LANES), const)],
        out_shape=[jax.ShapeDtypeStruct((t, D), F32),
                   jax.ShapeDtypeStruct((t, D + AUG), F32),
                   jax.ShapeDtypeStruct((t // tm, 1, tm), I32),
                   jax.ShapeDtypeStruct((t // tm, 1, tm), I32),
                   jax.ShapeDtypeStruct((BUCKET_ROWS, LANES), F32)],
        scratch_shapes=[pltpu.VMEM((BUCKET_ROWS, LANES), F32)],
        compiler_params=_cparams(("arbitrary",)),
        name="out_proj",
    )(o_diff, o_swa, x2, mod3, g_ffn, w_out_bf, w_r, b_r, tri)


def _row_copy(src_ref, dst_ref, src_row, dst_row, sem):
    return pltpu.make_async_copy(src_ref.at[pl.ds(src_row, 1)], dst_ref.at[pl.ds(dst_row, 1)], sem)


def _dispatch_kernel(pos_ref, zfill_ref, src_ref, dst_hbm, zero_scr, sem, zsem):
    n_tiles = zfill_ref.shape[0]

    @pl.when(pl.program_id(0) == 0)
    def _():
        zero_scr[...] = jnp.zeros_like(zero_scr)

        def zcopy(tile):
            return pltpu.make_async_copy(
                zero_scr, dst_hbm.at[pl.ds(pl.multiple_of(tile * TM_MOE, TM_MOE), TM_MOE)], zsem)

        def zstart(tile, carry):
            @pl.when(zfill_ref[tile] != 0)
            def _():
                zcopy(tile).start()
            return carry

        lax.fori_loop(0, n_tiles, zstart, 0)

        def zwait(tile, carry):
            @pl.when(zfill_ref[tile] != 0)
            def _():
                zcopy(tile).wait()
            return carry

        lax.fori_loop(0, n_tiles, zwait, 0)

    def start(r, carry):
        _row_copy(src_ref, dst_hbm, r, pos_ref[0, 0, r], sem).start()
        return carry

    lax.fori_loop(0, CH, start, 0, unroll=8)

    def wait(r, carry):
        _row_copy(src_ref, dst_hbm, 0, 0, sem).wait()
        return carry

    lax.fori_loop(0, CH, wait, 0, unroll=8)


def _dispatch(pos3, zfill, h2aug):
    t, width = h2aug.shape
    n_rows = zfill.shape[0] * TM_MOE
    return pl.pallas_call(
        _dispatch_kernel,
        grid=(t // CH,),
        in_specs=[pl.BlockSpec((1, 1, CH), lambda i: (i, 0, 0), memory_space=pltpu.SMEM),
                  pl.BlockSpec(memory_space=pltpu.SMEM),
                  pl.BlockSpec((CH, width), lambda i: (i, 0))],
        out_specs=pl.BlockSpec(memory_space=pl.ANY),
        out_shape=jax.ShapeDtypeStruct((n_rows, width), F32),
        scratch_shapes=[pltpu.VMEM((TM_MOE, width), F32), pltpu.SemaphoreType.DMA(()),
                        pltpu.SemaphoreType.DMA(())],
        compiler_params=_cparams(("arbitrary",)),
        name="dispatch",
    )(pos3, zfill, h2aug)


def _moe_kernel(lo_ref, hi_ref, valid_ref, xs_ref, wg_lo, wu_lo, wd_lo, wg_hi, wu_hi, wd_hi, y_ref):
    del lo_ref, hi_ref
    t = pl.program_id(0)

    @pl.when(valid_ref[t] != 0)
    def _():
        xb = xs_ref[:, 0:D].astype(BF16)
        y = None
        for k, (wg, wu, wd) in enumerate(((wg_lo, wu_lo, wd_lo), (wg_hi, wu_hi, wd_hi))):
            gt = jnp.dot(xb, wg[0], preferred_element_type=F32)
            up = jnp.dot(xb, wu[0], preferred_element_type=F32)
            act = (gt * (1.0 / (1.0 + jnp.exp(-gt))) * up).astype(BF16)
            dn = jnp.dot(act, wd[0], preferred_element_type=F32)
            term = xs_ref[:, D + k:D + k + 1] * dn
            y = term if y is None else y + term
        y_ref[...] = y

    @pl.when(valid_ref[t] == 0)
    def _():
        y_ref[...] = jnp.zeros_like(y_ref)


def _moe(tile_lo, tile_hi, tile_valid, xs, wg, wu, wd):
    n_rows, width = xs.shape
    n_tiles = n_rows // TM_MOE
    up_lo = lambda t, lo, hi, v: (lo[t], 0, 0)
    up_hi = lambda t, lo, hi, v: (hi[t], 0, 0)
    grid_spec = pltpu.PrefetchScalarGridSpec(
        num_scalar_prefetch=3,
        grid=(n_tiles,),
        in_specs=[pl.BlockSpec((TM_MOE, width), lambda t, lo, hi, v: (t, 0)),
                  pl.BlockSpec((1, D, D_EXPERT), up_lo), pl.BlockSpec((1, D, D_EXPERT), up_lo),
                  pl.BlockSpec((1, D_EXPERT, D), up_lo),
                  pl.BlockSpec((1, D, D_EXPERT), up_hi), pl.BlockSpec((1, D, D_EXPERT), up_hi),
                  pl.BlockSpec((1, D_EXPERT, D), up_hi)],
        out_specs=pl.BlockSpec((TM_MOE, D), lambda t, lo, hi, v: (t, 0)),
    )
    return pl.pallas_call(
        _moe_kernel,
        grid_spec=grid_spec,
        out_shape=jax.ShapeDtypeStruct((n_rows, D), F32),
        compiler_params=_cparams(("arbitrary",)),
        name="moe",
    )(tile_lo, tile_hi, tile_valid, xs, wg, wu, wd, wg, wu, wd)


def _final_kernel(pos_ref, ys_hbm, x1_ref, mod_ref, g_ref, o_ref, y_scr, sem):
    def start(r, carry):
        pltpu.make_async_copy(ys_hbm.at[pl.ds(pos_ref[0, 0, r], 1)], y_scr.at[pl.ds(r, 1)], sem).start()
        return carry

    lax.fori_loop(0, CH, start, 0, unroll=8)

    def wait(r, carry):
        pltpu.make_async_copy(ys_hbm.at[pl.ds(0, 1)], y_scr.at[pl.ds(0, 1)], sem).wait()
        return carry

    lax.fori_loop(0, CH, wait, 0, unroll=8)

    m = mod_ref[0]
    x2 = x1_ref[...] + m[5:6] * y_scr[...]
    ms = jnp.mean(x2 * x2, axis=-1, keepdims=True)
    o_ref[...] = x2 * lax.rsqrt(ms + EPS) * g_ref[...]


def _final(pos3, ys, x1, mod3, g_final, seq):
    t = x1.shape[0]
    per_b = seq // CH
    return pl.pallas_call(
        _final_kernel,
        grid=(t // CH,),
        in_specs=[pl.BlockSpec((1, 1, CH), lambda i: (i, 0, 0), memory_space=pltpu.SMEM),
                  pl.BlockSpec(memory_space=pl.ANY),
                  pl.BlockSpec((CH, D), lambda i: (i, 0)),
                  pl.BlockSpec((1, 6, D), lambda i: (i // per_b, 0, 0)),
                  pl.BlockSpec((1, D), lambda i: (0, 0))],
        out_specs=pl.BlockSpec((CH, D), lambda i: (i, 0)),
        out_shape=jax.ShapeDtypeStruct((t, D), F32),
        scratch_shapes=[pltpu.VMEM((CH, D), F32), pltpu.SemaphoreType.DMA(())],
        compiler_params=_cparams(("arbitrary",)),
        name="final",
    )(pos3, ys, x1, mod3, g_final)


def _rope_tables(seq):
    inv = ROPE_THETA ** (-jnp.arange(0, HD, 2, dtype=F32) / HD)
    ang = jnp.arange(seq, dtype=F32)[:, None] * inv[None, :]
    cos, sin = jnp.cos(ang), jnp.sin(ang)
    return jnp.tile(cos, (1, 4)), jnp.concatenate([-sin, -sin, sin, sin], axis=-1)


def _rotary_column_order():
    half = HD // 2
    lane = np.arange(LANES)
    which, part, f = (lane // half) % 2, lane // HD, lane % half
    cols = []
    for base in (0, 512):
        for h in range(4):
            cols.append(base + h * LANES + which * HD + part * half + f)
    cols.append(np.arange(1024, 1536))
    for c in range(4):
        cols.append(1536 + (c + 4 * which) * HD + part * half + f)
    cols.append(2048 + which * HD + part * half + f)
    cols.append(np.arange(2176, 2304))
    return np.concatenate(cols)


def kernel(x, c, w_ada, b_ada, g_mix, w_in, diff_lambda, g_diff_sub, swa_sinks, w_out, g_ffn,
           w_route_group, b_route_group, w_route_expert, b_route_expert, w_gate, w_up, w_down,
           g_final):
    bsz, seq, _ = x.shape
    t = bsz * seq
    x2 = x.reshape(t, D)
    cos, sin = _rope_tables(seq)

    mod3 = _ada_mod(c, w_ada[0], b_ada[0].reshape(1, -1)).reshape(bsz, 6, D)

    w_in_bf = w_in[0][:, _rotary_column_order()].astype(BF16)
    dq, dk, dv, sq, sk, sv = _in_proj(x2, mod3, g_mix[0].reshape(1, D), w_in_bf, cos, sin, seq)
    o_diff = _diff_attn(dq, dk, dv, diff_lambda[0], g_diff_sub[0].reshape(1, LANES), bsz, seq)
    o_swa = _swa_attn(sq, sk, sv, swa_sinks[0].reshape(1, 8), bsz, seq)

    n_r = N_GROUPS + N_GROUPS * EPG
    w_r = jnp.concatenate([w_route_group[0], w_route_expert[0],
                           jnp.zeros((D, LANES - n_r), F32)], axis=1).astype(BF16)
    b_r = jnp.concatenate([b_route_group[0], b_route_expert[0],
                           jnp.zeros((LANES - n_r,), F32)]).reshape(1, LANES)
    tri = (jnp.arange(TM_PROJ)[:, None] < jnp.arange(TM_PROJ)[None, :]).astype(BF16)
    x1, h2aug, bucket, rank, counts = _out_proj(o_diff, o_swa, x2, mod3, g_ffn[0].reshape(1, D),
                                                w_out[0].astype(BF16), w_r, b_r, tri, seq)

    n_tiles = t // TM_MOE + N_BUCKETS
    cnt = counts[:N_BUCKETS, 0].astype(I32)
    tiles_per = (cnt + TM_MOE - 1) // TM_MOE
    tile_end = jnp.cumsum(tiles_per)
    offsets = (tile_end - tiles_per) * TM_MOE
    pos = offsets[bucket.reshape(t)] + rank.reshape(t)
    pos3 = pos.reshape(t // CH, 1, CH)
    tid = jnp.arange(n_tiles, dtype=I32)
    tile_bucket = jnp.minimum(jnp.sum(tid[:, None] >= tile_end[None, :], axis=1), N_BUCKETS - 1)
    tile_valid = (tid < tile_end[-1]).astype(I32)
    last_of_bucket = jnp.sum(tid[:, None] == (tile_end - 1)[None, :], axis=1) > 0
    zfill = (last_of_bucket | (tile_valid == 0)).astype(I32)
    grp, pair = tile_bucket // 6, tile_bucket % 6
    tile_lo = (grp * EPG + jnp.asarray(PAIR_LO, I32)[pair]).astype(I32)
    tile_hi = (grp * EPG + jnp.asarray(PAIR_HI, I32)[pair]).astype(I32)

    xs = _dispatch(pos3, zfill, h2aug)
    ys = _moe(tile_lo, tile_hi, tile_valid, xs, w_gate[0].astype(BF16), w_up[0].astype(BF16),
              w_down[0].astype(BF16))
    out = _final(pos3, ys, x1, mod3, g_final.reshape(1, D), seq)
    return out.reshape(bsz, seq, D)
```

```python
import functools
import math

import jax
import jax.numpy as jnp
import numpy as np
from jax import lax
from jax.experimental import pallas as pl
from jax.experimental.pallas import tpu as pltpu

F32 = jnp.float32
BF16 = jnp.bfloat16
I32 = jnp.int32

D = 1024
HD = 64
EPS = 1e-6
ROPE_THETA = 10000.0
LOG2E = math.log2(math.e)
LAMBDA_INIT = 0.8 - 0.6 * math.exp(-0.3 * 0)
N_GROUPS = 4
EPG = 4
N_BUCKETS = 24
BUCKET_ROWS = 32
D_EXPERT = 512
AUG = 128

LANES = 128
VMEM_LIMIT = 48 * 1024 * 1024

TM_PROJ = 512
TQ = 512
BLK = 128
TM_MOE = 512
SUBLANES = 8
RUN_ROWS = TM_PROJ + N_BUCKETS * (SUBLANES - 1)
LS = -(-RUN_ROWS // 64) * 64
XW = D // 2 + AUG

PAIR_LO = (0, 0, 0, 1, 1, 2)
PAIR_HI = (1, 2, 3, 2, 3, 3)


def _cparams(sem):
    return pltpu.CompilerParams(dimension_semantics=sem, vmem_limit_bytes=VMEM_LIMIT)


def _ada_kernel(c_ref, w_ref, b_ref, o_ref):
    c = c_ref[...]
    ca = c * (1.0 / (1.0 + jnp.exp(-c)))
    o_ref[...] = jnp.dot(ca.astype(BF16), w_ref[...].astype(BF16),
                         preferred_element_type=F32) + b_ref[...]


def _ada_mod(c, w, b):
    bsz = c.shape[0]
    n = w.shape[1]
    tn = 512
    return pl.pallas_call(
        _ada_kernel,
        grid=(n // tn,),
        in_specs=[pl.BlockSpec((bsz, D), lambda j: (0, 0)),
                  pl.BlockSpec((D, tn), lambda j: (0, j)),
                  pl.BlockSpec((1, tn), lambda j: (0, j))],
        out_specs=pl.BlockSpec((bsz, tn), lambda j: (0, j)),
        out_shape=jax.ShapeDtypeStruct((bsz, n), F32),
        compiler_params=_cparams(("arbitrary",)),
        name="ada_mod",
    )(c, w, b)


def _rope(a, cos, sin):
    return a * cos + pltpu.roll(a, HD, 1) * sin


def _inproj_kernel(x_ref, mod_ref, g_ref, w_ref, cos_ref, sin_ref,
                   dq_ref, dk_ref, dv_ref, sq_ref, sk_ref, sv_ref, h_scr):
    tm = x_ref.shape[0]
    x = x_ref[...]
    ms = jnp.mean(x * x, axis=-1, keepdims=True)
    y = x * lax.rsqrt(ms + EPS) * g_ref[...]
    m = mod_ref[0]
    h = y * (1.0 + m[1:2]) + m[0:1]
    h_scr[...] = h.astype(BF16)

    cos = cos_ref[...]
    sin = sin_ref[...]
    lane = lax.broadcasted_iota(I32, (tm, LANES), 1)
    head_a = (lane & 32) == 0
    low64 = lane < HD
    scale = (HD ** -0.5) * LOG2E

    def chunk(j):
        return jnp.dot(h_scr[...], w_ref[:, 256 * j:256 * (j + 1)], preferred_element_type=F32)

    def halves(a):
        return a[:, :LANES], a[:, LANES:]

    for j in range(2):
        for t, r in enumerate(halves(chunk(j))):
            c0 = 256 * j + LANES * t
            dq_ref[:, c0:c0 + LANES] = (_rope(r, cos, sin) * scale).astype(BF16)
    for j in range(2):
        for t, r in enumerate(halves(chunk(2 + j))):
            c0 = 256 * j + LANES * t
            dk_ref[:, c0:c0 + LANES] = _rope(r, cos, sin).astype(BF16)
    for j in range(2):
        dv_ref[:, 256 * j:256 * (j + 1)] = chunk(4 + j).astype(BF16)
    for j in range(2):
        for t, r in enumerate(halves(chunk(6 + j))):
            rp = _rope(r, cos, sin) * scale
            c = 2 * j + t
            sq_ref[:, LANES * c:LANES * (c + 1)] = jnp.where(head_a, rp, 0.0).astype(BF16)
            sq_ref[:, LANES * (c + 4):LANES * (c + 5)] = jnp.where(head_a, 0.0, rp).astype(BF16)
    kv = chunk(8)
    kk, vv = halves(kv)
    kr = _rope(kk, cos, sin)
    sk_ref[:, :LANES] = jnp.where(head_a, kr, 0.0).astype(BF16)
    sk_ref[:, LANES:] = jnp.where(head_a, 0.0, kr).astype(BF16)
    sv_ref[:, :LANES] = jnp.where(low64, vv, 0.0).astype(BF16)
    sv_ref[:, LANES:] = jnp.where(low64, pltpu.roll(vv, HD, 1), 0.0).astype(BF16)


def _in_proj(x2, mod3, g_mix, w_in_bf, cos, sin, seq):
    t = x2.shape[0]
    tm = TM_PROJ
    per_b = seq // tm
    n_in = w_in_bf.shape[1]
    row = lambda i: (i, 0)
    return pl.pallas_call(
        _inproj_kernel,
        grid=(t // tm,),
        in_specs=[pl.BlockSpec((tm, D), row),
                  pl.BlockSpec((1, 6, D), lambda i: (i // per_b, 0, 0)),
                  pl.BlockSpec((1, D), lambda i: (0, 0)),
                  pl.BlockSpec((D, n_in), lambda i: (0, 0)),
                  pl.BlockSpec((tm, LANES), lambda i: (i % per_b, 0)),
                  pl.BlockSpec((tm, LANES), lambda i: (i % per_b, 0))],
        out_specs=[pl.BlockSpec((tm, 512), row), pl.BlockSpec((tm, 512), row),
                   pl.BlockSpec((tm, 512), row), pl.BlockSpec((tm, 1024), row),
                   pl.BlockSpec((tm, 256), row), pl.BlockSpec((tm, 256), row)],
        out_shape=[jax.ShapeDtypeStruct((t, 512), BF16), jax.ShapeDtypeStruct((t, 512), BF16),
                   jax.ShapeDtypeStruct((t, 512), BF16), jax.ShapeDtypeStruct((t, 1024), BF16),
                   jax.ShapeDtypeStruct((t, 256), BF16), jax.ShapeDtypeStruct((t, 256), BF16)],
        scratch_shapes=[pltpu.VMEM((tm, D), BF16)],
        compiler_params=_cparams(("arbitrary",)),
        name="in_proj",
    )(x2, mod3, g_mix, w_in_bf, cos, sin)


_NT_DIMS = (((1,), (1,)), ((), ()))


def _diff_kernel(lam_ref, g_ref, q_ref, k_ref, v_ref, o_ref):
    seq = q_ref.shape[0]
    tq = TQ
    lp = lam_ref[...]
    lam = (jnp.exp(jnp.sum(lp[0:1] * lp[1:2], axis=-1, keepdims=True))
           - jnp.exp(jnp.sum(lp[2:3] * lp[3:4], axis=-1, keepdims=True)) + LAMBDA_INIT)
    lane = lax.broadcasted_iota(I32, (tq, LANES), 1)
    map_a = (lane & 32) == 0
    causal = (lax.broadcasted_iota(I32, (tq, tq), 1) <= lax.broadcasted_iota(I32, (tq, tq), 0))
    gain = g_ref[...] * (1.0 - LAMBDA_INIT)

    for i in reversed(range(seq // tq)):
        lo, hi = i * tq, (i + 1) * tq
        q = q_ref[lo:hi, :]
        zero = jnp.zeros_like(q)
        probs = []
        for qm in (jnp.where(map_a, q, zero), jnp.where(map_a, zero, q)):
            s_dg = lax.dot_general(qm, k_ref[lo:hi, :], _NT_DIMS, preferred_element_type=F32)
            s_dg = jnp.where(causal, s_dg, -jnp.inf)
            mx = jnp.max(s_dg, axis=-1, keepdims=True)
            if i > 0:
                s_off = lax.dot_general(qm, k_ref[0:lo, :], _NT_DIMS, preferred_element_type=F32)
                mx = jnp.maximum(mx, jnp.max(s_off, axis=-1, keepdims=True))
                p_off = jnp.exp2(s_off - mx)
            p_dg = jnp.exp2(s_dg - mx)
            l = jnp.sum(p_dg, axis=-1, keepdims=True)
            if i > 0:
                l = l + jnp.sum(p_off, axis=-1, keepdims=True)
                probs.append((p_off, p_dg, l))
            else:
                probs.append((None, p_dg, l))
        (p0_off, p0_dg, l0), (p1_off, p1_dg, l1) = probs
        c = lam * l0 / l1
        o = jnp.dot((p0_dg - c * p1_dg).astype(BF16), v_ref[lo:hi, :], preferred_element_type=F32)
        if i > 0:
            o = o + jnp.dot((p0_off - c * p1_off).astype(BF16), v_ref[0:lo, :],
                            preferred_element_type=F32)
        o = o / l0
        ms = jnp.mean(o * o, axis=-1, keepdims=True)
        o_ref[lo:hi, :] = (o * lax.rsqrt(ms + EPS) * gain).astype(BF16)


def _diff_attn(dq, dk, dv, lam_p, g_sub, bsz, seq):
    t = dq.shape[0]
    blk = lambda b, h: (b, h)
    return pl.pallas_call(
        _diff_kernel,
        grid=(bsz, 4),
        in_specs=[pl.BlockSpec((4, HD), lambda b, h: (0, 0)),
                  pl.BlockSpec((1, LANES), lambda b, h: (0, 0)),
                  pl.BlockSpec((seq, LANES), blk),
                  pl.BlockSpec((seq, LANES), blk),
                  pl.BlockSpec((seq, LANES), blk)],
        out_specs=pl.BlockSpec((seq, LANES), blk),
        out_shape=jax.ShapeDtypeStruct((t, 512), BF16),
        compiler_params=_cparams(("arbitrary", "arbitrary")),
        name="diff_attn",
    )(lam_p, g_sub, dq, dk, dv)


def _swa_kernel(sink_ref, q_ref, k_ref, v_ref, o_ref):
    seq = q_ref.shape[0]
    g = pl.program_id(1)
    rows = 4 * BLK
    hsel = jnp.right_shift(lax.broadcasted_iota(I32, (rows, 8), 0), 7) + 4 * g
    sink_col = jnp.sum(jnp.where(lax.broadcasted_iota(I32, (rows, 8), 1) == hsel,
                                 sink_ref[...] * LOG2E, 0.0), axis=-1, keepdims=True)
    qi = lax.broadcasted_iota(I32, (rows, 2 * BLK), 0) & (BLK - 1)
    kj = lax.broadcasted_iota(I32, (rows, 2 * BLK), 1)
    dist = qi + BLK - kj
    band = (dist >= 0) & (dist < BLK)
    first = (lax.broadcasted_iota(I32, (rows, BLK), 1)
             <= (lax.broadcasted_iota(I32, (rows, BLK), 0) & (BLK - 1)))
    low64 = lax.broadcasted_iota(I32, (BLK, LANES), 1) < HD

    def attend(q_rows, k_rows, mask):
        qs = jnp.concatenate([q_ref[q_rows, LANES * j:LANES * (j + 1)] for j in range(4)], axis=0)
        s = lax.dot_general(qs, k_ref[k_rows, :], _NT_DIMS, preferred_element_type=F32)
        s = jnp.where(mask, s, -jnp.inf)
        mx = jnp.maximum(jnp.max(s, axis=-1, keepdims=True), sink_col)
        p = jnp.exp2(s - mx)
        l = jnp.sum(p, axis=-1, keepdims=True) + jnp.exp2(sink_col - mx)
        o = jnp.dot(p.astype(BF16), v_ref[k_rows, :], preferred_element_type=F32) / l
        for c in range(2):
            even = o[(2 * c) * BLK:(2 * c + 1) * BLK, :]
            odd = o[(2 * c + 1) * BLK:(2 * c + 2) * BLK, :]
            o_ref[q_rows, LANES * c:LANES * (c + 1)] = jnp.where(
                low64, even, pltpu.roll(odd, HD, 1)).astype(BF16)

    attend(pl.ds(0, BLK), pl.ds(0, BLK), first)

    for n in range(1, seq // BLK):
        attend(pl.ds(n * BLK, BLK), pl.ds((n - 1) * BLK, 2 * BLK), band)


def _swa_attn(sq, sk, sv, sinks, bsz, seq):
    t = sq.shape[0]
    return pl.pallas_call(
        _swa_kernel,
        grid=(bsz, 2),
        in_specs=[pl.BlockSpec((1, 8), lambda b, g: (0, 0)),
                  pl.BlockSpec((seq, 512), lambda b, g: (b, g)),
                  pl.BlockSpec((seq, LANES), lambda b, g: (b, g)),
                  pl.BlockSpec((seq, LANES), lambda b, g: (b, g))],
        out_specs=pl.BlockSpec((seq, 256), lambda b, g: (b, g)),
        out_shape=jax.ShapeDtypeStruct((t, 512), BF16),
        compiler_params=_cparams(("arbitrary", "arbitrary")),
        name="swa_attn",
    )(sinks, sq, sk, sv)


def _bf16_pieces(w):
    p0 = w.astype(BF16).astype(F32)
    r1 = w - p0
    p1 = r1.astype(BF16).astype(F32)
    return p0, p1, r1 - p1


def _outproj_kernel(od_ref, os_ref, x_ref, mod_ref, g_ref, wo_ref, wr_ref, br_ref, tri_ref,
                    x1_ref, h2_ref, bucket_ref, rank_ref, cnt_ref):
    tm = x_ref.shape[0]
    m = mod_ref[0]
    mix = (jnp.dot(od_ref[...], wo_ref[0:512, :], preferred_element_type=F32)
           + jnp.dot(os_ref[...], wo_ref[512:1024, :], preferred_element_type=F32))
    x1 = x_ref[...] + m[2:3] * mix
    x1_ref[...] = x1
    ms = jnp.mean(x1 * x1, axis=-1, keepdims=True)
    h2 = x1 * lax.rsqrt(ms + EPS) * g_ref[...] * (1.0 + m[4:5]) + m[3:4]
    h2_ref[:, 0:D] = h2.astype(BF16)

    logits = jnp.dot(h2.astype(BF16), wr_ref[...], preferred_element_type=F32) + br_ref[...]
    lt = logits.T
    r = [lt[i:i + 1, :] for i in range(N_GROUPS + N_GROUPS * EPG)]

    gl = r[0:N_GROUPS]
    gmax = jnp.maximum(jnp.maximum(gl[0], gl[1]), jnp.maximum(gl[2], gl[3]))
    gidx = jnp.where(gl[0] == gmax, 0, jnp.where(gl[1] == gmax, 1, jnp.where(gl[2] == gmax, 2, 3)))
    gz = (jnp.exp(gl[0] - gmax) + jnp.exp(gl[1] - gmax)
          + jnp.exp(gl[2] - gmax) + jnp.exp(gl[3] - gmax))
    grp_p = 1.0 / gz

    el = []
    for i in range(EPG):
        e = r[N_GROUPS + 3 * EPG + i]
        for gg in (2, 1, 0):
            e = jnp.where(gidx == gg, r[N_GROUPS + gg * EPG + i], e)
        el.append(e)
    emax = jnp.maximum(jnp.maximum(el[0], el[1]), jnp.maximum(el[2], el[3]))
    ex = [jnp.exp(e - emax) for e in el]
    ez = ex[0] + ex[1] + ex[2] + ex[3]
    pr = [e / ez for e in ex]
    p1 = jnp.maximum(jnp.maximum(pr[0], pr[1]), jnp.maximum(pr[2], pr[3]))
    a = jnp.where(pr[0] == p1, 0, jnp.where(pr[1] == p1, 1, jnp.where(pr[2] == p1, 2, 3)))
    rest = [jnp.where(a == i, -1.0, pr[i]) for i in range(EPG)]
    p2 = jnp.maximum(jnp.maximum(rest[0], rest[1]), jnp.maximum(rest[2], rest[3]))
    b = jnp.where(rest[0] == p2, 0, jnp.where(rest[1] == p2, 1, jnp.where(rest[2] == p2, 2, 3)))
    psum = p1 + p2
    wa = grp_p * (p1 / psum)
    wb = grp_p * (p2 / psum)
    lo = jnp.minimum(a, b)
    hi = jnp.maximum(a, b)
    w_lo = jnp.where(a < b, wa, wb)
    w_hi = jnp.where(a < b, wb, wa)
    pair = jnp.where(lo == 0, hi - 1, jnp.where(lo == 1, hi + 1, 5))
    bucket = gidx * 6 + pair

    rid = lax.broadcasted_iota(I32, (AUG, tm), 0)
    aug = jnp.zeros((AUG, tm), F32)
    for k, piece in enumerate(_bf16_pieces(w_lo) + _bf16_pieces(w_hi)):
        aug = jnp.where(rid == k, piece, aug)
    h2_ref[:, D:D + AUG] = aug.T.astype(BF16)

    oh = (lax.broadcasted_iota(I32, (BUCKET_ROWS, tm), 0) == bucket)
    oh_f = jnp.where(oh, 1.0, 0.0)
    before = jnp.dot(oh_f.astype(BF16), tri_ref[...], preferred_element_type=F32)
    rank = jnp.sum(oh_f * before, axis=0, keepdims=True)
    cnt_ref[0] = jnp.broadcast_to(jnp.sum(oh_f, axis=1, keepdims=True), (BUCKET_ROWS, LANES))
    bucket_ref[0] = bucket
    rank_ref[0] = rank.astype(I32)


def _out_proj(o_diff, o_swa, x2, mod3, g_ffn, w_out_bf, w_r, b_r, tri, seq):
    t = x2.shape[0]
    tm = TM_PROJ
    per_b = seq // tm
    row = lambda i: (i, 0)
    const = lambda i: (0, 0)
    return pl.pallas_call(
        _outproj_kernel,
        grid=(t // tm,),
        in_specs=[pl.BlockSpec((tm, 512), row), pl.BlockSpec((tm, 512), row),
                  pl.BlockSpec((tm, D), row),
                  pl.BlockSpec((1, 6, D), lambda i: (i // per_b, 0, 0)),
                  pl.BlockSpec((1, D), const),
                  pl.BlockSpec((D, D), const),
                  pl.BlockSpec((D, LANES), const),
                  pl.BlockSpec((1, LANES), const),
                  pl.BlockSpec((tm, tm), const)],
        out_specs=[pl.BlockSpec((tm, D), row),
                   pl.BlockSpec((tm, D + AUG), row),
                   pl.BlockSpec((1, 1, tm), lambda i: (i, 0, 0)),
                   pl.BlockSpec((1, 1, tm), lambda i: (i, 0, 0)),
                   pl.BlockSpec((1, BUCKET_ROWS, LANES), lambda i: (i, 0, 0))],
        out_shape=[jax.ShapeDtypeStruct((t, D), F32),
                   jax.ShapeDtypeStruct((t, D + AUG), BF16),
                   jax.ShapeDtypeStruct((t // tm, 1, tm), I32),
                   jax.ShapeDtypeStruct((t // tm, 1, tm), I32),
                   jax.ShapeDtypeStruct((t // tm, BUCKET_ROWS, LANES), F32)],
        compiler_params=_cparams(("arbitrary",)),
        name="out_proj",
    )(o_diff, o_swa, x2, mod3, g_ffn, w_out_bf, w_r, b_r, tri)


META_LOCAL, META_GLOBAL, META_ROWS, META_TILE_ROWS = 0, 1, 2, 3


def _local_pos(meta_ref, tile, bucket, rank):
    pos = rank
    for b in range(N_BUCKETS):
        pos = pos + jnp.where(bucket == b, meta_ref[META_LOCAL, tile * N_BUCKETS + b], 0)
    return pos


def _for_each_run(meta_ref, tile, fn):
    for b in range(N_BUCKETS):
        rows = meta_ref[META_ROWS, tile * N_BUCKETS + b]

        @pl.when(rows > 0)
        def _(b=b, rows=rows):
            fn(pl.multiple_of(meta_ref[META_LOCAL, tile * N_BUCKETS + b], SUBLANES),
               pl.multiple_of(meta_ref[META_GLOBAL, tile * N_BUCKETS + b], SUBLANES),
               pl.multiple_of(rows, SUBLANES))


def _dispatch_kernel(meta_ref, zfill_ref, h2_ref, bucket_ref, rank_ref, xs_hbm,
                     sort_scr, zero_scr, sems, zsem):
    i = pl.program_id(0)
    n_steps = pl.num_programs(0)
    n_tiles = zfill_ref.shape[0]
    tm = h2_ref.shape[0]
    slot = i % 2

    @pl.when(i == 0)
    def _():
        zero_scr[...] = jnp.zeros_like(zero_scr)

        def zcopy(tile):
            return pltpu.make_async_copy(
                zero_scr, xs_hbm.at[pl.ds(pl.multiple_of(tile * TM_MOE, TM_MOE), TM_MOE)], zsem)

        def zstart(tile, carry):
            @pl.when(zfill_ref[tile] != 0)
            def _():
                zcopy(tile).start()
            return carry

        lax.fori_loop(0, n_tiles, zstart, 0)

        def zwait(tile, carry):
            @pl.when(zfill_ref[tile] != 0)
            def _():
                zcopy(tile).wait()
            return carry

        lax.fori_loop(0, n_tiles, zwait, 0)

    def run_copy(buf, local_row, global_row, rows):
        return pltpu.make_async_copy(sort_scr.at[buf, pl.ds(local_row, rows)],
                                     xs_hbm.at[pl.ds(global_row, rows)], sems.at[buf])

    def wait_runs(tile, buf):
        run_copy(buf, 0, 0, pl.multiple_of(meta_ref[META_TILE_ROWS, tile], SUBLANES)).wait()

    @pl.when(i >= 2)
    def _():
        wait_runs(i - 2, slot)

    lpos = _local_pos(meta_ref, i, bucket_ref[0], rank_ref[0])
    onehot = jnp.where(lax.broadcasted_iota(I32, (LS, tm), 0) == lpos, 1.0, 0.0).astype(BF16)
    srt = jnp.dot(onehot, h2_ref[...], preferred_element_type=F32)
    half = D // 2
    lo_bits = lax.bitcast_convert_type(srt[:, 0:half], jnp.uint32) >> 16
    hi_bits = lax.bitcast_convert_type(srt[:, half:D], jnp.uint32) & jnp.uint32(0xFFFF0000)
    aug = srt[:, D:D + AUG]
    w_lo = aug[:, 0:1] + aug[:, 1:2] + aug[:, 2:3]
    w_hi = aug[:, 3:4] + aug[:, 4:5] + aug[:, 5:6]
    lane = lax.broadcasted_iota(I32, (LS, AUG), 1)
    wts = jnp.where(lane == 0, w_lo, jnp.where(lane == 1, w_hi, 0.0))

    sort_scr[slot, :, 0:half] = lo_bits | hi_bits
    sort_scr[slot, :, half:XW] = lax.bitcast_convert_type(wts, jnp.uint32)
    _for_each_run(meta_ref, i, lambda lr, gr, rows: run_copy(slot, lr, gr, rows).start())

    @pl.when(i == n_steps - 1)
    def _():
        @pl.when(i >= 1)
        def _():
            wait_runs(i - 1, 1 - slot)
        wait_runs(i, slot)


def _dispatch(meta, zfill, h2a, bucket, rank):
    t, width = h2a.shape
    tm = TM_PROJ
    n_rows = zfill.shape[0] * TM_MOE
    grid_spec = pltpu.PrefetchScalarGridSpec(
        num_scalar_prefetch=2,
        grid=(t // tm,),
        in_specs=[pl.BlockSpec((tm, width), lambda i, m, z: (i, 0)),
                  pl.BlockSpec((1, 1, tm), lambda i, m, z: (i, 0, 0)),
                  pl.BlockSpec((1, 1, tm), lambda i, m, z: (i, 0, 0))],
        out_specs=pl.BlockSpec(memory_space=pl.ANY),
        scratch_shapes=[pltpu.VMEM((2, LS, XW), jnp.uint32), pltpu.VMEM((TM_MOE, XW), jnp.uint32),
                        pltpu.SemaphoreType.DMA((2,)), pltpu.SemaphoreType.DMA(())],
    )
    return pl.pallas_call(
        _dispatch_kernel,
        grid_spec=grid_spec,
        out_shape=jax.ShapeDtypeStruct((n_rows, XW), jnp.uint32),
        compiler_params=_cparams(("arbitrary",)),
        name="dispatch",
    )(meta, zfill, h2a, bucket, rank)


def _moe_kernel(lo_ref, hi_ref, valid_ref, xs_ref, wg_lo, wu_lo, wd_lo, wg_hi, wu_hi, wd_hi, y_ref):
    del lo_ref, hi_ref
    t = pl.program_id(0)

    @pl.when(valid_ref[t] != 0)
    def _():
        half = D // 2
        words = xs_ref[:, 0:half]
        x_a = lax.bitcast_convert_type(words << 16, F32).astype(BF16)
        x_b = lax.bitcast_convert_type(words & jnp.uint32(0xFFFF0000), F32).astype(BF16)
        xb = jnp.concatenate([x_a, x_b], axis=1)
        y = None
        for k, (wg, wu, wd) in enumerate(((wg_lo, wu_lo, wd_lo), (wg_hi, wu_hi, wd_hi))):
            gt = jnp.dot(xb, wg[0], preferred_element_type=F32)
            up = jnp.dot(xb, wu[0], preferred_element_type=F32)
            act = (gt * (1.0 / (1.0 + jnp.exp(-gt))) * up).astype(BF16)
            dn = jnp.dot(act, wd[0], preferred_element_type=F32)
            term = lax.bitcast_convert_type(xs_ref[:, half + k:half + k + 1], F32) * dn
            y = term if y is None else y + term
        y_ref[...] = y

    @pl.when(valid_ref[t] == 0)
    def _():
        y_ref[...] = jnp.zeros_like(y_ref)


def _moe(tile_lo, tile_hi, tile_valid, xs, wg, wu, wd):
    n_rows, width = xs.shape
    n_tiles = n_rows // TM_MOE
    up_lo = lambda t, lo, hi, v: (lo[t], 0, 0)
    up_hi = lambda t, lo, hi, v: (hi[t], 0, 0)
    grid_spec = pltpu.PrefetchScalarGridSpec(
        num_scalar_prefetch=3,
        grid=(n_tiles,),
        in_specs=[pl.BlockSpec((TM_MOE, width), lambda t, lo, hi, v: (t, 0)),
                  pl.BlockSpec((1, D, D_EXPERT), up_lo), pl.BlockSpec((1, D, D_EXPERT), up_lo),
                  pl.BlockSpec((1, D_EXPERT, D), up_lo),
                  pl.BlockSpec((1, D, D_EXPERT), up_hi), pl.BlockSpec((1, D, D_EXPERT), up_hi),
                  pl.BlockSpec((1, D_EXPERT, D), up_hi)],
        out_specs=pl.BlockSpec((TM_MOE, D), lambda t, lo, hi, v: (t, 0)),
    )
    return pl.pallas_call(
        _moe_kernel,
        grid_spec=grid_spec,
        out_shape=jax.ShapeDtypeStruct((n_rows, D), F32),
        compiler_params=_cparams(("arbitrary",)),
        name="moe",
    )(tile_lo, tile_hi, tile_valid, xs, wg, wu, wd, wg, wu, wd)


def _final_kernel(meta_ref, ys_hbm, x1_ref, mod_ref, g_ref, bucket_ref, rank_ref, o_ref, y_scr, sems):
    i = pl.program_id(0)
    n_steps = pl.num_programs(0)
    tm = x1_ref.shape[0]
    slot = i % 2

    def run_copy(buf, local_row, global_row, rows):
        return pltpu.make_async_copy(ys_hbm.at[pl.ds(global_row, rows)],
                                     y_scr.at[buf, pl.ds(local_row, rows)], sems.at[buf])

    def fetch_runs(tile, buf):
        _for_each_run(meta_ref, tile, lambda lr, gr, rows: run_copy(buf, lr, gr, rows).start())

    @pl.when(i == 0)
    def _():
        y_scr[...] = jnp.zeros_like(y_scr)
        fetch_runs(0, 0)

    @pl.when(i + 1 < n_steps)
    def _():
        fetch_runs(i + 1, 1 - slot)

    run_copy(slot, 0, 0, pl.multiple_of(meta_ref[META_TILE_ROWS, i], SUBLANES)).wait()

    lpos = _local_pos(meta_ref, i, bucket_ref[0], rank_ref[0])
    lpos_col = jnp.broadcast_to(lpos.astype(F32), (LANES, tm)).T[:, 0:1].astype(I32)
    onehot = jnp.where(lax.broadcasted_iota(I32, (tm, LS), 1) == lpos_col, 1.0, 0.0).astype(BF16)
    ysrt = y_scr[slot]
    y_hi = ysrt.astype(BF16)
    y_lo = (ysrt - y_hi.astype(F32)).astype(BF16)
    y = (jnp.dot(onehot, y_hi, preferred_element_type=F32)
         + jnp.dot(onehot, y_lo, preferred_element_type=F32))

    m = mod_ref[0]
    x2 = x1_ref[...] + m[5:6] * y
    ms = jnp.mean(x2 * x2, axis=-1, keepdims=True)
    o_ref[...] = x2 * lax.rsqrt(ms + EPS) * g_ref[...]


def _final(meta, ys, x1, mod3, g_final, bucket, rank, seq):
    t = x1.shape[0]
    tm = TM_PROJ
    per_b = seq // tm
    grid_spec = pltpu.PrefetchScalarGridSpec(
        num_scalar_prefetch=1,
        grid=(t // tm,),
        in_specs=[pl.BlockSpec(memory_space=pl.ANY),
                  pl.BlockSpec((tm, D), lambda i, m: (i, 0)),
                  pl.BlockSpec((1, 6, D), lambda i, m: (i // per_b, 0, 0)),
                  pl.BlockSpec((1, D), lambda i, m: (0, 0)),
                  pl.BlockSpec((1, 1, tm), lambda i, m: (i, 0, 0)),
                  pl.BlockSpec((1, 1, tm), lambda i, m: (i, 0, 0))],
        out_specs=pl.BlockSpec((tm, D), lambda i, m: (i, 0)),
        scratch_shapes=[pltpu.VMEM((2, LS, D), F32), pltpu.SemaphoreType.DMA((2,))],
    )
    return pl.pallas_call(
        _final_kernel,
        grid_spec=grid_spec,
        out_shape=jax.ShapeDtypeStruct((t, D), F32),
        compiler_params=_cparams(("arbitrary",)),
        name="final",
    )(meta, ys, x1, mod3, g_final, bucket, rank)


def _rope_tables(seq):
    inv = ROPE_THETA ** (-jnp.arange(0, HD, 2, dtype=F32) / HD)
    ang = jnp.arange(seq, dtype=F32)[:, None] * inv[None, :]
    cos, sin = jnp.cos(ang), jnp.sin(ang)
    return jnp.tile(cos, (1, 4)), jnp.concatenate([-sin, -sin, sin, sin], axis=-1)


def _rotary_column_order():
    half = HD // 2
    lane = np.arange(LANES)
    which, part, f = (lane // half) % 2, lane // HD, lane % half
    cols = []
    for base in (0, 512):
        for h in range(4):
            cols.append(base + h * LANES + which * HD + part * half + f)
    cols.append(np.arange(1024, 1536))
    for c in range(4):
        cols.append(1536 + (c + 4 * which) * HD + part * half + f)
    cols.append(2048 + which * HD + part * half + f)
    cols.append(np.arange(2176, 2304))
    return np.concatenate(cols)


def kernel(x, c, w_ada, b_ada, g_mix, w_in, diff_lambda, g_diff_sub, swa_sinks, w_out, g_ffn,
           w_route_group, b_route_group, w_route_expert, b_route_expert, w_gate, w_up, w_down,
           g_final):
    bsz, seq, _ = x.shape
    t = bsz * seq
    x2 = x.reshape(t, D)
    cos, sin = _rope_tables(seq)

    mod3 = _ada_mod(c, w_ada[0], b_ada[0].reshape(1, -1)).reshape(bsz, 6, D)

    w_in_bf = w_in[0][:, _rotary_column_order()].astype(BF16)
    dq, dk, dv, sq, sk, sv = _in_proj(x2, mod3, g_mix[0].reshape(1, D), w_in_bf, cos, sin, seq)
    o_diff = _diff_attn(dq, dk, dv, diff_lambda[0], g_diff_sub[0].reshape(1, LANES), bsz, seq)
    o_swa = _swa_attn(sq, sk, sv, swa_sinks[0].reshape(1, 8), bsz, seq)

    n_r = N_GROUPS + N_GROUPS * EPG
    w_r = jnp.concatenate([w_route_group[0], w_route_expert[0],
                           jnp.zeros((D, LANES - n_r), F32)], axis=1).astype(BF16)
    b_r = jnp.concatenate([b_route_group[0], b_route_expert[0],
                           jnp.zeros((LANES - n_r,), F32)]).reshape(1, LANES)
    tri = (jnp.arange(TM_PROJ)[:, None] < jnp.arange(TM_PROJ)[None, :]).astype(BF16)
    x1, h2a, bucket, rank, counts = _out_proj(o_diff, o_swa, x2, mod3, g_ffn[0].reshape(1, D),
                                              w_out[0].astype(BF16), w_r, b_r, tri, seq)

    n_tok_tiles = t // TM_PROJ
    n_tiles = -(-(t + n_tok_tiles * N_BUCKETS * (SUBLANES - 1)) // TM_MOE) + N_BUCKETS
    cnt = counts[:, :N_BUCKETS, 0].astype(I32)
    run_rows = (cnt + SUBLANES - 1) // SUBLANES * SUBLANES
    local_off = jnp.cumsum(run_rows, axis=1) - run_rows
    tiles_per = (jnp.sum(run_rows, axis=0) + TM_MOE - 1) // TM_MOE
    tile_end = jnp.cumsum(tiles_per)
    bucket_start = (tile_end - tiles_per) * TM_MOE
    global_off = bucket_start[None, :] + jnp.cumsum(run_rows, axis=0) - run_rows
    tile_rows = jnp.pad(jnp.sum(run_rows, axis=1), (0, n_tok_tiles * (N_BUCKETS - 1)))
    meta = jnp.stack([local_off.reshape(-1), global_off.reshape(-1), run_rows.reshape(-1), tile_rows])
    tid = jnp.arange(n_tiles, dtype=I32)
    tile_bucket = jnp.minimum(jnp.sum(tid[:, None] >= tile_end[None, :], axis=1), N_BUCKETS - 1)
    tile_valid = (tid < tile_end[-1]).astype(I32)
    last_of_bucket = jnp.sum(tid[:, None] == (tile_end - 1)[None, :], axis=1) > 0
    zfill = (last_of_bucket | (tile_valid == 0)).astype(I32)
    grp, pair = tile_bucket // 6, tile_bucket % 6
    tile_lo = (grp * EPG + jnp.asarray(PAIR_LO, I32)[pair]).astype(I32)
    tile_hi = (grp * EPG + jnp.asarray(PAIR_HI, I32)[pair]).astype(I32)

    xs = _dispatch(meta, zfill, h2a, bucket, rank)
    ys = _moe(tile_lo, tile_hi, tile_valid, xs, w_gate[0].astype(BF16), w_up[0].astype(BF16),
              w_down[0].astype(BF16))
    out = _final(meta, ys, x1, mod3, g_final.reshape(1, D), bucket, rank, seq)
    return out.reshape(bsz, seq, D)
```

```python
import functools
import math

import jax
import jax.numpy as jnp
import numpy as np
from jax import lax
from jax.experimental import pallas as pl
from jax.experimental.pallas import tpu as pltpu

F32 = jnp.float32
BF16 = jnp.bfloat16
I32 = jnp.int32

D = 1024
HD = 64
EPS = 1e-6
ROPE_THETA = 10000.0
LOG2E = math.log2(math.e)
LAMBDA_INIT = 0.8 - 0.6 * math.exp(-0.3 * 0)
N_GROUPS = 4
EPG = 4
N_BUCKETS = 24
BUCKET_ROWS = 32
D_EXPERT = 512
AUG = 128

LANES = 128
VMEM_LIMIT = 48 * 1024 * 1024

TM_PROJ = 512
TQ = 512
BLK = 128
TM_MOE = 512
SUBLANES = 8
RUN_ROWS = TM_PROJ + N_BUCKETS * (SUBLANES - 1)
LS = -(-RUN_ROWS // 64) * 64
XW = D + AUG

PAIR_LO = (0, 0, 0, 1, 1, 2)
PAIR_HI = (1, 2, 3, 2, 3, 3)


def _cparams(sem):
    return pltpu.CompilerParams(dimension_semantics=sem, vmem_limit_bytes=VMEM_LIMIT)


def _ada_kernel(c_ref, w_ref, b_ref, o_ref):
    c = c_ref[...]
    ca = c * (1.0 / (1.0 + jnp.exp(-c)))
    o_ref[...] = jnp.dot(ca.astype(BF16), w_ref[...].astype(BF16),
                         preferred_element_type=F32) + b_ref[...]


def _ada_mod(c, w, b):
    bsz = c.shape[0]
    n = w.shape[1]
    tn = 512
    return pl.pallas_call(
        _ada_kernel,
        grid=(n // tn,),
        in_specs=[pl.BlockSpec((bsz, D), lambda j: (0, 0)),
                  pl.BlockSpec((D, tn), lambda j: (0, j)),
                  pl.BlockSpec((1, tn), lambda j: (0, j))],
        out_specs=pl.BlockSpec((bsz, tn), lambda j: (0, j)),
        out_shape=jax.ShapeDtypeStruct((bsz, n), F32),
        compiler_params=_cparams(("arbitrary",)),
        name="ada_mod",
    )(c, w, b)


def _rope(a, cos, sin):
    return a * cos + pltpu.roll(a, HD, 1) * sin


def _inproj_kernel(x_ref, mod_ref, g_ref, w_ref, cos_ref, sin_ref,
                   dq_ref, dk_ref, dv_ref, sq_ref, sk_ref, sv_ref, h_scr):
    tm = x_ref.shape[0]
    x = x_ref[...]
    ms = jnp.mean(x * x, axis=-1, keepdims=True)
    y = x * lax.rsqrt(ms + EPS) * g_ref[...]
    m = mod_ref[0]
    h = y * (1.0 + m[1:2]) + m[0:1]
    h_scr[...] = h.astype(BF16)

    cos = cos_ref[...]
    sin = sin_ref[...]
    lane = lax.broadcasted_iota(I32, (tm, LANES), 1)
    head_a = (lane & 32) == 0
    low64 = lane < HD
    scale = (HD ** -0.5) * LOG2E

    def chunk(j):
        return jnp.dot(h_scr[...], w_ref[:, 256 * j:256 * (j + 1)], preferred_element_type=F32)

    def halves(a):
        return a[:, :LANES], a[:, LANES:]

    for j in range(2):
        for t, r in enumerate(halves(chunk(j))):
            c0 = 256 * j + LANES * t
            dq_ref[:, c0:c0 + LANES] = (_rope(r, cos, sin) * scale).astype(BF16)
    for j in range(2):
        for t, r in enumerate(halves(chunk(2 + j))):
            c0 = 256 * j + LANES * t
            dk_ref[:, c0:c0 + LANES] = _rope(r, cos, sin).astype(BF16)
    for j in range(2):
        dv_ref[:, 256 * j:256 * (j + 1)] = chunk(4 + j).astype(BF16)
    for j in range(2):
        for t, r in enumerate(halves(chunk(6 + j))):
            rp = _rope(r, cos, sin) * scale
            c = 2 * j + t
            sq_ref[:, LANES * c:LANES * (c + 1)] = jnp.where(head_a, rp, 0.0).astype(BF16)
            sq_ref[:, LANES * (c + 4):LANES * (c + 5)] = jnp.where(head_a, 0.0, rp).astype(BF16)
    kv = chunk(8)
    kk, vv = halves(kv)
    kr = _rope(kk, cos, sin)
    sk_ref[:, :LANES] = jnp.where(head_a, kr, 0.0).astype(BF16)
    sk_ref[:, LANES:] = jnp.where(head_a, 0.0, kr).astype(BF16)
    sv_ref[:, :LANES] = jnp.where(low64, vv, 0.0).astype(BF16)
    sv_ref[:, LANES:] = jnp.where(low64, pltpu.roll(vv, HD, 1), 0.0).astype(BF16)


def _in_proj(x2, mod3, g_mix, w_in_bf, cos, sin, seq):
    t = x2.shape[0]
    tm = TM_PROJ
    per_b = seq // tm
    n_in = w_in_bf.shape[1]
    row = lambda i: (i, 0)
    return pl.pallas_call(
        _inproj_kernel,
        grid=(t // tm,),
        in_specs=[pl.BlockSpec((tm, D), row),
                  pl.BlockSpec((1, 6, D), lambda i: (i // per_b, 0, 0)),
                  pl.BlockSpec((1, D), lambda i: (0, 0)),
                  pl.BlockSpec((D, n_in), lambda i: (0, 0)),
                  pl.BlockSpec((tm, LANES), lambda i: (i % per_b, 0)),
                  pl.BlockSpec((tm, LANES), lambda i: (i % per_b, 0))],
        out_specs=[pl.BlockSpec((tm, 512), row), pl.BlockSpec((tm, 512), row),
                   pl.BlockSpec((tm, 512), row), pl.BlockSpec((tm, 1024), row),
                   pl.BlockSpec((tm, 256), row), pl.BlockSpec((tm, 256), row)],
        out_shape=[jax.ShapeDtypeStruct((t, 512), BF16), jax.ShapeDtypeStruct((t, 512), BF16),
                   jax.ShapeDtypeStruct((t, 512), BF16), jax.ShapeDtypeStruct((t, 1024), BF16),
                   jax.ShapeDtypeStruct((t, 256), BF16), jax.ShapeDtypeStruct((t, 256), BF16)],
        scratch_shapes=[pltpu.VMEM((tm, D), BF16)],
        compiler_params=_cparams(("arbitrary",)),
        name="in_proj",
    )(x2, mod3, g_mix, w_in_bf, cos, sin)


_NT_DIMS = (((1,), (1,)), ((), ()))


def _diff_kernel(lam_ref, g_ref, q_ref, k_ref, v_ref, o_ref):
    seq = q_ref.shape[0]
    tq = TQ
    lp = lam_ref[...]
    lam = (jnp.exp(jnp.sum(lp[0:1] * lp[1:2], axis=-1, keepdims=True))
           - jnp.exp(jnp.sum(lp[2:3] * lp[3:4], axis=-1, keepdims=True)) + LAMBDA_INIT)
    lane = lax.broadcasted_iota(I32, (tq, LANES), 1)
    map_a = (lane & 32) == 0
    causal = (lax.broadcasted_iota(I32, (tq, tq), 1) <= lax.broadcasted_iota(I32, (tq, tq), 0))
    gain = g_ref[...] * (1.0 - LAMBDA_INIT)

    for i in reversed(range(seq // tq)):
        lo, hi = i * tq, (i + 1) * tq
        q = q_ref[lo:hi, :]
        zero = jnp.zeros_like(q)
        probs = []
        for qm in (jnp.where(map_a, q, zero), jnp.where(map_a, zero, q)):
            s_dg = lax.dot_general(qm, k_ref[lo:hi, :], _NT_DIMS, preferred_element_type=F32)
            s_dg = jnp.where(causal, s_dg, -jnp.inf)
            mx = jnp.max(s_dg, axis=-1, keepdims=True)
            if i > 0:
                s_off = lax.dot_general(qm, k_ref[0:lo, :], _NT_DIMS, preferred_element_type=F32)
                mx = jnp.maximum(mx, jnp.max(s_off, axis=-1, keepdims=True))
                p_off = jnp.exp2(s_off - mx)
            p_dg = jnp.exp2(s_dg - mx)
            l = jnp.sum(p_dg, axis=-1, keepdims=True)
            if i > 0:
                l = l + jnp.sum(p_off, axis=-1, keepdims=True)
                probs.append((p_off, p_dg, l))
            else:
                probs.append((None, p_dg, l))
        (p0_off, p0_dg, l0), (p1_off, p1_dg, l1) = probs
        c = lam * l0 / l1
        o = jnp.dot((p0_dg - c * p1_dg).astype(BF16), v_ref[lo:hi, :], preferred_element_type=F32)
        if i > 0:
            o = o + jnp.dot((p0_off - c * p1_off).astype(BF16), v_ref[0:lo, :],
                            preferred_element_type=F32)
        o = o / l0
        ms = jnp.mean(o * o, axis=-1, keepdims=True)
        o_ref[lo:hi, :] = (o * lax.rsqrt(ms + EPS) * gain).astype(BF16)


def _diff_attn(dq, dk, dv, lam_p, g_sub, bsz, seq):
    t = dq.shape[0]
    blk = lambda b, h: (b, h)
    return pl.pallas_call(
        _diff_kernel,
        grid=(bsz, 4),
        in_specs=[pl.BlockSpec((4, HD), lambda b, h: (0, 0)),
                  pl.BlockSpec((1, LANES), lambda b, h: (0, 0)),
                  pl.BlockSpec((seq, LANES), blk),
                  pl.BlockSpec((seq, LANES), blk),
                  pl.BlockSpec((seq, LANES), blk)],
        out_specs=pl.BlockSpec((seq, LANES), blk),
        out_shape=jax.ShapeDtypeStruct((t, 512), BF16),
        compiler_params=_cparams(("arbitrary", "arbitrary")),
        name="diff_attn",
    )(lam_p, g_sub, dq, dk, dv)


def _swa_kernel(sink_ref, q_ref, k_ref, v_ref, o_ref):
    seq = q_ref.shape[0]
    g = pl.program_id(1)
    rows = 4 * BLK
    hsel = jnp.right_shift(lax.broadcasted_iota(I32, (rows, 8), 0), 7) + 4 * g
    sink_col = jnp.sum(jnp.where(lax.broadcasted_iota(I32, (rows, 8), 1) == hsel,
                                 sink_ref[...] * LOG2E, 0.0), axis=-1, keepdims=True)
    qi = lax.broadcasted_iota(I32, (rows, 2 * BLK), 0) & (BLK - 1)
    kj = lax.broadcasted_iota(I32, (rows, 2 * BLK), 1)
    dist = qi + BLK - kj
    band = (dist >= 0) & (dist < BLK)
    first = (lax.broadcasted_iota(I32, (rows, BLK), 1)
             <= (lax.broadcasted_iota(I32, (rows, BLK), 0) & (BLK - 1)))
    low64 = lax.broadcasted_iota(I32, (BLK, LANES), 1) < HD

    def attend(q_rows, k_rows, mask):
        qs = jnp.concatenate([q_ref[q_rows, LANES * j:LANES * (j + 1)] for j in range(4)], axis=0)
        s = lax.dot_general(qs, k_ref[k_rows, :], _NT_DIMS, preferred_element_type=F32)
        s = jnp.where(mask, s, -jnp.inf)
        mx = jnp.maximum(jnp.max(s, axis=-1, keepdims=True), sink_col)
        p = jnp.exp2(s - mx)
        l = jnp.sum(p, axis=-1, keepdims=True) + jnp.exp2(sink_col - mx)
        o = jnp.dot(p.astype(BF16), v_ref[k_rows, :], preferred_element_type=F32) / l
        for c in range(2):
            even = o[(2 * c) * BLK:(2 * c + 1) * BLK, :]
            odd = o[(2 * c + 1) * BLK:(2 * c + 2) * BLK, :]
            o_ref[q_rows, LANES * c:LANES * (c + 1)] = jnp.where(
                low64, even, pltpu.roll(odd, HD, 1)).astype(BF16)

    attend(pl.ds(0, BLK), pl.ds(0, BLK), first)

    for n in range(1, seq // BLK):
        attend(pl.ds(n * BLK, BLK), pl.ds((n - 1) * BLK, 2 * BLK), band)


def _swa_attn(sq, sk, sv, sinks, bsz, seq):
    t = sq.shape[0]
    return pl.pallas_call(
        _swa_kernel,
        grid=(bsz, 2),
        in_specs=[pl.BlockSpec((1, 8), lambda b, g: (0, 0)),
                  pl.BlockSpec((seq, 512), lambda b, g: (b, g)),
                  pl.BlockSpec((seq, LANES), lambda b, g: (b, g)),
                  pl.BlockSpec((seq, LANES), lambda b, g: (b, g))],
        out_specs=pl.BlockSpec((seq, 256), lambda b, g: (b, g)),
        out_shape=jax.ShapeDtypeStruct((t, 512), BF16),
        compiler_params=_cparams(("arbitrary", "arbitrary")),
        name="swa_attn",
    )(sinks, sq, sk, sv)


def _bf16_pieces(w):
    p0 = w.astype(BF16).astype(F32)
    r1 = w - p0
    p1 = r1.astype(BF16).astype(F32)
    return p0, p1, r1 - p1


def _outproj_kernel(od_ref, os_ref, x_ref, mod_ref, g_ref, wo_ref, wr_ref, br_ref, tri_ref,
                    x1_ref, h2_ref, bucket_ref, rank_ref, cnt_ref):
    tm = x_ref.shape[0]
    m = mod_ref[0]
    mix = (jnp.dot(od_ref[...], wo_ref[0:512, :], preferred_element_type=F32)
           + jnp.dot(os_ref[...], wo_ref[512:1024, :], preferred_element_type=F32))
    x1 = x_ref[...] + m[2:3] * mix
    x1_ref[...] = x1
    ms = jnp.mean(x1 * x1, axis=-1, keepdims=True)
    h2 = x1 * lax.rsqrt(ms + EPS) * g_ref[...] * (1.0 + m[4:5]) + m[3:4]
    h2_ref[:, 0:D] = h2.astype(BF16)

    logits = jnp.dot(h2.astype(BF16), wr_ref[...], preferred_element_type=F32) + br_ref[...]
    lt = logits.T
    r = [lt[i:i + 1, :] for i in range(N_GROUPS + N_GROUPS * EPG)]

    gl = r[0:N_GROUPS]
    gmax = jnp.maximum(jnp.maximum(gl[0], gl[1]), jnp.maximum(gl[2], gl[3]))
    gidx = jnp.where(gl[0] == gmax, 0, jnp.where(gl[1] == gmax, 1, jnp.where(gl[2] == gmax, 2, 3)))
    gz = (jnp.exp(gl[0] - gmax) + jnp.exp(gl[1] - gmax)
          + jnp.exp(gl[2] - gmax) + jnp.exp(gl[3] - gmax))
    grp_p = 1.0 / gz

    el = []
    for i in range(EPG):
        e = r[N_GROUPS + 3 * EPG + i]
        for gg in (2, 1, 0):
            e = jnp.where(gidx == gg, r[N_GROUPS + gg * EPG + i], e)
        el.append(e)
    emax = jnp.maximum(jnp.maximum(el[0], el[1]), jnp.maximum(el[2], el[3]))
    ex = [jnp.exp(e - emax) for e in el]
    ez = ex[0] + ex[1] + ex[2] + ex[3]
    pr = [e / ez for e in ex]
    p1 = jnp.maximum(jnp.maximum(pr[0], pr[1]), jnp.maximum(pr[2], pr[3]))
    a = jnp.where(pr[0] == p1, 0, jnp.where(pr[1] == p1, 1, jnp.where(pr[2] == p1, 2, 3)))
    rest = [jnp.where(a == i, -1.0, pr[i]) for i in range(EPG)]
    p2 = jnp.maximum(jnp.maximum(rest[0], rest[1]), jnp.maximum(rest[2], rest[3]))
    b = jnp.where(rest[0] == p2, 0, jnp.where(rest[1] == p2, 1, jnp.where(rest[2] == p2, 2, 3)))
    psum = p1 + p2
    wa = grp_p * (p1 / psum)
    wb = grp_p * (p2 / psum)
    lo = jnp.minimum(a, b)
    hi = jnp.maximum(a, b)
    w_lo = jnp.where(a < b, wa, wb)
    w_hi = jnp.where(a < b, wb, wa)
    pair = jnp.where(lo == 0, hi - 1, jnp.where(lo == 1, hi + 1, 5))
    bucket = gidx * 6 + pair

    rid = lax.broadcasted_iota(I32, (AUG, tm), 0)
    aug = jnp.zeros((AUG, tm), F32)
    for k, piece in enumerate(_bf16_pieces(w_lo) + _bf16_pieces(w_hi)):
        aug = jnp.where(rid == k, piece, aug)
    h2_ref[:, D:D + AUG] = aug.T.astype(BF16)

    oh = (lax.broadcasted_iota(I32, (BUCKET_ROWS, tm), 0) == bucket)
    oh_f = jnp.where(oh, 1.0, 0.0)
    before = jnp.dot(oh_f.astype(BF16), tri_ref[...], preferred_element_type=F32)
    rank = jnp.sum(oh_f * before, axis=0, keepdims=True)
    cnt_ref[0] = jnp.broadcast_to(jnp.sum(oh_f, axis=1, keepdims=True), (BUCKET_ROWS, LANES))
    bucket_ref[0] = bucket
    rank_ref[0] = rank.astype(I32)


def _out_proj(o_diff, o_swa, x2, mod3, g_ffn, w_out_bf, w_r, b_r, tri, seq):
    t = x2.shape[0]
    tm = TM_PROJ
    per_b = seq // tm
    row = lambda i: (i, 0)
    const = lambda i: (0, 0)
    return pl.pallas_call(
        _outproj_kernel,
        grid=(t // tm,),
        in_specs=[pl.BlockSpec((tm, 512), row), pl.BlockSpec((tm, 512), row),
                  pl.BlockSpec((tm, D), row),
                  pl.BlockSpec((1, 6, D), lambda i: (i // per_b, 0, 0)),
                  pl.BlockSpec((1, D), const),
                  pl.BlockSpec((D, D), const),
                  pl.BlockSpec((D, LANES), const),
                  pl.BlockSpec((1, LANES), const),
                  pl.BlockSpec((tm, tm), const)],
        out_specs=[pl.BlockSpec((tm, D), row),
                   pl.BlockSpec((tm, D + AUG), row),
                   pl.BlockSpec((1, 1, tm), lambda i: (i, 0, 0)),
                   pl.BlockSpec((1, 1, tm), lambda i: (i, 0, 0)),
                   pl.BlockSpec((1, BUCKET_ROWS, LANES), lambda i: (i, 0, 0))],
        out_shape=[jax.ShapeDtypeStruct((t, D), F32),
                   jax.ShapeDtypeStruct((t, D + AUG), BF16),
                   jax.ShapeDtypeStruct((t // tm, 1, tm), I32),
                   jax.ShapeDtypeStruct((t // tm, 1, tm), I32),
                   jax.ShapeDtypeStruct((t // tm, BUCKET_ROWS, LANES), F32)],
        compiler_params=_cparams(("arbitrary",)),
        name="out_proj",
    )(o_diff, o_swa, x2, mod3, g_ffn, w_out_bf, w_r, b_r, tri)


META_LOCAL, META_GLOBAL, META_ROWS, META_TILE_ROWS = 0, 1, 2, 3


def _local_pos(meta_ref, tile, bucket, rank):
    pos = rank
    for b in range(N_BUCKETS):
        pos = pos + jnp.where(bucket == b, meta_ref[META_LOCAL, tile * N_BUCKETS + b], 0)
    return pos


def _for_each_run(meta_ref, tile, fn):
    for b in range(N_BUCKETS):
        rows = meta_ref[META_ROWS, tile * N_BUCKETS + b]

        @pl.when(rows > 0)
        def _(b=b, rows=rows):
            fn(pl.multiple_of(meta_ref[META_LOCAL, tile * N_BUCKETS + b], SUBLANES),
               pl.multiple_of(meta_ref[META_GLOBAL, tile * N_BUCKETS + b], SUBLANES),
               pl.multiple_of(rows, SUBLANES))


def _dispatch_kernel(meta_ref, zfill_ref, h2_ref, bucket_ref, rank_ref, xs_hbm,
                     sort_scr, zero_scr, sems, zsem):
    i = pl.program_id(0)
    n_steps = pl.num_programs(0)
    n_tiles = zfill_ref.shape[0]
    tm = h2_ref.shape[0]
    slot = i % 2

    @pl.when(i == 0)
    def _():
        zero_scr[...] = jnp.zeros_like(zero_scr)

        def zcopy(tile):
            return pltpu.make_async_copy(
                zero_scr, xs_hbm.at[pl.ds(pl.multiple_of(tile * TM_MOE, TM_MOE), TM_MOE)], zsem)

        def zstart(tile, carry):
            @pl.when(zfill_ref[tile] != 0)
            def _():
                zcopy(tile).start()
            return carry

        lax.fori_loop(0, n_tiles, zstart, 0)

        def zwait(tile, carry):
            @pl.when(zfill_ref[tile] != 0)
            def _():
                zcopy(tile).wait()
            return carry

        lax.fori_loop(0, n_tiles, zwait, 0)

    def run_copy(buf, local_row, global_row, rows):
        return pltpu.make_async_copy(sort_scr.at[buf, pl.ds(local_row, rows)],
                                     xs_hbm.at[pl.ds(global_row, rows)], sems.at[buf])

    def wait_runs(tile, buf):
        run_copy(buf, 0, 0, pl.multiple_of(meta_ref[META_TILE_ROWS, tile], SUBLANES)).wait()

    @pl.when(i >= 2)
    def _():
        wait_runs(i - 2, slot)

    lpos = _local_pos(meta_ref, i, bucket_ref[0], rank_ref[0])
    onehot = jnp.where(lax.broadcasted_iota(I32, (LS, tm), 0) == lpos, 1.0, 0.0).astype(BF16)
    srt = jnp.dot(onehot, h2_ref[...], preferred_element_type=F32)
    aug = srt[:, D:D + AUG]
    w_lo = aug[:, 0:1] + aug[:, 1:2] + aug[:, 2:3]
    w_hi = aug[:, 3:4] + aug[:, 4:5] + aug[:, 5:6]
    lane = lax.broadcasted_iota(I32, (LS, AUG), 1)
    sort_scr[slot, :, 0:D] = srt[:, 0:D]
    sort_scr[slot, :, D:XW] = jnp.where(lane == 0, w_lo, jnp.where(lane == 1, w_hi, 0.0))
    _for_each_run(meta_ref, i, lambda lr, gr, rows: run_copy(slot, lr, gr, rows).start())

    @pl.when(i == n_steps - 1)
    def _():
        @pl.when(i >= 1)
        def _():
            wait_runs(i - 1, 1 - slot)
        wait_runs(i, slot)


def _dispatch(meta, zfill, h2a, bucket, rank):
    t, width = h2a.shape
    tm = TM_PROJ
    n_rows = zfill.shape[0] * TM_MOE
    grid_spec = pltpu.PrefetchScalarGridSpec(
        num_scalar_prefetch=2,
        grid=(t // tm,),
        in_specs=[pl.BlockSpec((tm, width), lambda i, m, z: (i, 0)),
                  pl.BlockSpec((1, 1, tm), lambda i, m, z: (i, 0, 0)),
                  pl.BlockSpec((1, 1, tm), lambda i, m, z: (i, 0, 0))],
        out_specs=pl.BlockSpec(memory_space=pl.ANY),
        scratch_shapes=[pltpu.VMEM((2, LS, XW), F32), pltpu.VMEM((TM_MOE, XW), F32),
                        pltpu.SemaphoreType.DMA((2,)), pltpu.SemaphoreType.DMA(())],
    )
    return pl.pallas_call(
        _dispatch_kernel,
        grid_spec=grid_spec,
        out_shape=jax.ShapeDtypeStruct((n_rows, XW), F32),
        compiler_params=_cparams(("arbitrary",)),
        name="dispatch",
    )(meta, zfill, h2a, bucket, rank)


def _moe_kernel(lo_ref, hi_ref, valid_ref, xs_ref, wg_lo, wu_lo, wd_lo, wg_hi, wu_hi, wd_hi, y_ref):
    del lo_ref, hi_ref
    t = pl.program_id(0)

    @pl.when(valid_ref[t] != 0)
    def _():
        xb = xs_ref[:, 0:D].astype(BF16)
        y = None
        for k, (wg, wu, wd) in enumerate(((wg_lo, wu_lo, wd_lo), (wg_hi, wu_hi, wd_hi))):
            gt = jnp.dot(xb, wg[0], preferred_element_type=F32)
            up = jnp.dot(xb, wu[0], preferred_element_type=F32)
            act = (gt * (1.0 / (1.0 + jnp.exp(-gt))) * up).astype(BF16)
            dn = jnp.dot(act, wd[0], preferred_element_type=F32)
            term = xs_ref[:, D + k:D + k + 1] * dn
            y = term if y is None else y + term
        y_ref[...] = y

    @pl.when(valid_ref[t] == 0)
    def _():
        y_ref[...] = jnp.zeros_like(y_ref)


def _moe(tile_lo, tile_hi, tile_valid, xs, wg, wu, wd):
    n_rows, width = xs.shape
    n_tiles = n_rows // TM_MOE
    up_lo = lambda t, lo, hi, v: (lo[t], 0, 0)
    up_hi = lambda t, lo, hi, v: (hi[t], 0, 0)
    grid_spec = pltpu.PrefetchScalarGridSpec(
        num_scalar_prefetch=3,
        grid=(n_tiles,),
        in_specs=[pl.BlockSpec((TM_MOE, width), lambda t, lo, hi, v: (t, 0)),
                  pl.BlockSpec((1, D, D_EXPERT), up_lo), pl.BlockSpec((1, D, D_EXPERT), up_lo),
                  pl.BlockSpec((1, D_EXPERT, D), up_lo),
                  pl.BlockSpec((1, D, D_EXPERT), up_hi), pl.BlockSpec((1, D, D_EXPERT), up_hi),
                  pl.BlockSpec((1, D_EXPERT, D), up_hi)],
        out_specs=pl.BlockSpec((TM_MOE, D), lambda t, lo, hi, v: (t, 0)),
    )
    return pl.pallas_call(
        _moe_kernel,
        grid_spec=grid_spec,
        out_shape=jax.ShapeDtypeStruct((n_rows, D), F32),
        compiler_params=_cparams(("arbitrary",)),
        name="moe",
    )(tile_lo, tile_hi, tile_valid, xs, wg, wu, wd, wg, wu, wd)


def _final_kernel(meta_ref, ys_hbm, x1_ref, mod_ref, g_ref, bucket_ref, rank_ref, o_ref, y_scr, sems):
    i = pl.program_id(0)
    n_steps = pl.num_programs(0)
    tm = x1_ref.shape[0]
    slot = i % 2

    def run_copy(buf, local_row, global_row, rows):
        return pltpu.make_async_copy(ys_hbm.at[pl.ds(global_row, rows)],
                                     y_scr.at[buf, pl.ds(local_row, rows)], sems.at[buf])

    def fetch_runs(tile, buf):
        _for_each_run(meta_ref, tile, lambda lr, gr, rows: run_copy(buf, lr, gr, rows).start())

    @pl.when(i == 0)
    def _():
        y_scr[...] = jnp.zeros_like(y_scr)
        fetch_runs(0, 0)

    @pl.when(i + 1 < n_steps)
    def _():
        fetch_runs(i + 1, 1 - slot)

    run_copy(slot, 0, 0, pl.multiple_of(meta_ref[META_TILE_ROWS, i], SUBLANES)).wait()

    lpos = _local_pos(meta_ref, i, bucket_ref[0], rank_ref[0])
    lpos_col = jnp.broadcast_to(lpos.astype(F32), (LANES, tm)).T[:, 0:1].astype(I32)
    onehot = jnp.where(lax.broadcasted_iota(I32, (tm, LS), 1) == lpos_col, 1.0, 0.0).astype(BF16)
    y = jnp.dot(onehot, y_scr[slot].astype(BF16), preferred_element_type=F32)

    m = mod_ref[0]
    x2 = x1_ref[...] + m[5:6] * y
    ms = jnp.mean(x2 * x2, axis=-1, keepdims=True)
    o_ref[...] = x2 * lax.rsqrt(ms + EPS) * g_ref[...]


def _final(meta, ys, x1, mod3, g_final, bucket, rank, seq):
    t = x1.shape[0]
    tm = TM_PROJ
    per_b = seq // tm
    grid_spec = pltpu.PrefetchScalarGridSpec(
        num_scalar_prefetch=1,
        grid=(t // tm,),
        in_specs=[pl.BlockSpec(memory_space=pl.ANY),
                  pl.BlockSpec((tm, D), lambda i, m: (i, 0)),
                  pl.BlockSpec((1, 6, D), lambda i, m: (i // per_b, 0, 0)),
                  pl.BlockSpec((1, D), lambda i, m: (0, 0)),
                  pl.BlockSpec((1, 1, tm), lambda i, m: (i, 0, 0)),
                  pl.BlockSpec((1, 1, tm), lambda i, m: (i, 0, 0))],
        out_specs=pl.BlockSpec((tm, D), lambda i, m: (i, 0)),
        scratch_shapes=[pltpu.VMEM((2, LS, D), F32), pltpu.SemaphoreType.DMA((2,))],
    )
    return pl.pallas_call(
        _final_kernel,
        grid_spec=grid_spec,
        out_shape=jax.ShapeDtypeStruct((t, D), F32),
        compiler_params=_cparams(("arbitrary",)),
        name="final",
    )(meta, ys, x1, mod3, g_final, bucket, rank)


def _rope_tables(seq):
    inv = ROPE_THETA ** (-jnp.arange(0, HD, 2, dtype=F32) / HD)
    ang = jnp.arange(seq, dtype=F32)[:, None] * inv[None, :]
    cos, sin = jnp.cos(ang), jnp.sin(ang)
    return jnp.tile(cos, (1, 4)), jnp.concatenate([-sin, -sin, sin, sin], axis=-1)


def _rotary_column_order():
    half = HD // 2
    lane = np.arange(LANES)
    which, part, f = (lane // half) % 2, lane // HD, lane % half
    cols = []
    for base in (0, 512):
        for h in range(4):
            cols.append(base + h * LANES + which * HD + part * half + f)
    cols.append(np.arange(1024, 1536))
    for c in range(4):
        cols.append(1536 + (c + 4 * which) * HD + part * half + f)
    cols.append(2048 + which * HD + part * half + f)
    cols.append(np.arange(2176, 2304))
    return np.concatenate(cols)


def kernel(x, c, w_ada, b_ada, g_mix, w_in, diff_lambda, g_diff_sub, swa_sinks, w_out, g_ffn,
           w_route_group, b_route_group, w_route_expert, b_route_expert, w_gate, w_up, w_down,
           g_final):
    bsz, seq, _ = x.shape
    t = bsz * seq
    x2 = x.reshape(t, D)
    cos, sin = _rope_tables(seq)

    mod3 = _ada_mod(c, w_ada[0], b_ada[0].reshape(1, -1)).reshape(bsz, 6, D)

    w_in_bf = w_in[0][:, _rotary_column_order()].astype(BF16)
    dq, dk, dv, sq, sk, sv = _in_proj(x2, mod3, g_mix[0].reshape(1, D), w_in_bf, cos, sin, seq)
    o_diff = _diff_attn(dq, dk, dv, diff_lambda[0], g_diff_sub[0].reshape(1, LANES), bsz, seq)
    o_swa = _swa_attn(sq, sk, sv, swa_sinks[0].reshape(1, 8), bsz, seq)

    n_r = N_GROUPS + N_GROUPS * EPG
    w_r = jnp.concatenate([w_route_group[0], w_route_expert[0],
                           jnp.zeros((D, LANES - n_r), F32)], axis=1).astype(BF16)
    b_r = jnp.concatenate([b_route_group[0], b_route_expert[0],
                           jnp.zeros((LANES - n_r,), F32)]).reshape(1, LANES)
    tri = (jnp.arange(TM_PROJ)[:, None] < jnp.arange(TM_PROJ)[None, :]).astype(BF16)
    x1, h2a, bucket, rank, counts = _out_proj(o_diff, o_swa, x2, mod3, g_ffn[0].reshape(1, D),
                                              w_out[0].astype(BF16), w_r, b_r, tri, seq)

    n_tok_tiles = t // TM_PROJ
    n_tiles = -(-(t + n_tok_tiles * N_BUCKETS * (SUBLANES - 1)) // TM_MOE) + N_BUCKETS
    cnt = counts[:, :N_BUCKETS, 0].astype(I32)
    run_rows = (cnt + SUBLANES - 1) // SUBLANES * SUBLANES
    local_off = jnp.cumsum(run_rows, axis=1) - run_rows
    tiles_per = (jnp.sum(run_rows, axis=0) + TM_MOE - 1) // TM_MOE
    tile_end = jnp.cumsum(tiles_per)
    bucket_start = (tile_end - tiles_per) * TM_MOE
    global_off = bucket_start[None, :] + jnp.cumsum(run_rows, axis=0) - run_rows
    tile_rows = jnp.pad(jnp.sum(run_rows, axis=1), (0, n_tok_tiles * (N_BUCKETS - 1)))
    meta = jnp.stack([local_off.reshape(-1), global_off.reshape(-1), run_rows.reshape(-1), tile_rows])
    tid = jnp.arange(n_tiles, dtype=I32)
    tile_bucket = jnp.minimum(jnp.sum(tid[:, None] >= tile_end[None, :], axis=1), N_BUCKETS - 1)
    tile_valid = (tid < tile_end[-1]).astype(I32)
    last_of_bucket = jnp.sum(tid[:, None] == (tile_end - 1)[None, :], axis=1) > 0
    zfill = (last_of_bucket | (tile_valid == 0)).astype(I32)
    grp, pair = tile_bucket // 6, tile_bucket % 6
    tile_lo = (grp * EPG + jnp.asarray(PAIR_LO, I32)[pair]).astype(I32)
    tile_hi = (grp * EPG + jnp.asarray(PAIR_HI, I32)[pair]).astype(I32)

    xs = _dispatch(meta, zfill, h2a, bucket, rank)
    ys = _moe(tile_lo, tile_hi, tile_valid, xs, w_gate[0].astype(BF16), w_up[0].astype(BF16),
              w_down[0].astype(BF16))
    out = _final(meta, ys, x1, mod3, g_final.reshape(1, D), bucket, rank, seq)
    return out.reshape(bsz, seq, D)
```

```python
import functools
import math

import jax
import jax.numpy as jnp
import numpy as np
from jax import lax
from jax.experimental import pallas as pl
from jax.experimental.pallas import tpu as pltpu

F32 = jnp.float32
BF16 = jnp.bfloat16
I32 = jnp.int32

D = 1024
HD = 64
EPS = 1e-6
ROPE_THETA = 10000.0
LOG2E = math.log2(math.e)
LAMBDA_INIT = 0.8 - 0.6 * math.exp(-0.3 * 0)
N_GROUPS = 4
EPG = 4
N_BUCKETS = 24
BUCKET_ROWS = 32
D_EXPERT = 512
AUG = 128

LANES = 128
VMEM_LIMIT = 48 * 1024 * 1024

TM_IN = 1024
TM_IN_SUB = 256
TM_PROJ = 512
TM_OUT_SUB = 256
TQ = 512
BLK = 128
TM_MOE = 512
SUBLANES = 8
RUN_ROWS = TM_PROJ + N_BUCKETS * (SUBLANES - 1)
LS = -(-RUN_ROWS // 64) * 64
LS_SUB = LS // 2
XW = D + AUG

PAIR_LO = (0, 0, 0, 1, 1, 2)
PAIR_HI = (1, 2, 3, 2, 3, 3)


def _cparams(sem):
    return pltpu.CompilerParams(dimension_semantics=sem, vmem_limit_bytes=VMEM_LIMIT)


def _ada_kernel(c_ref, w_ref, b_ref, o_ref):
    c = c_ref[...]
    ca = c * (1.0 / (1.0 + jnp.exp(-c)))
    o_ref[...] = jnp.dot(ca.astype(BF16), w_ref[...].astype(BF16),
                         preferred_element_type=F32) + b_ref[...]


def _ada_mod(c, w, b):
    bsz = c.shape[0]
    n = w.shape[1]
    tn = 512
    return pl.pallas_call(
        _ada_kernel,
        grid=(n // tn,),
        in_specs=[pl.BlockSpec((bsz, D), lambda j: (0, 0)),
                  pl.BlockSpec((D, tn), lambda j: (0, j)),
                  pl.BlockSpec((1, tn), lambda j: (0, j))],
        out_specs=pl.BlockSpec((bsz, tn), lambda j: (0, j)),
        out_shape=jax.ShapeDtypeStruct((bsz, n), F32),
        compiler_params=_cparams(("arbitrary",)),
        name="ada_mod",
    )(c, w, b)


def _rope(a, cos, sin):
    return a * cos + pltpu.roll(a, HD, 1) * sin


def _inproj_kernel(x_ref, mod_ref, g_ref, w_ref, cos_ref, sin_ref,
                   dq_ref, dk_ref, dv_ref, sq_ref, sk_ref, sv_ref, h_scr):
    tm = x_ref.shape[0]
    sub = TM_IN_SUB
    m = mod_ref[0]
    lane = lax.broadcasted_iota(I32, (sub, LANES), 1)
    head_a = (lane & 32) == 0
    low64 = lane < HD
    scale = (HD ** -0.5) * LOG2E

    def halves(a):
        return a[:, :LANES], a[:, LANES:]

    for r0 in range(0, tm, sub):
        rows = slice(r0, r0 + sub)
        x = x_ref[rows, :]
        ms = jnp.mean(x * x, axis=-1, keepdims=True)
        y = x * lax.rsqrt(ms + EPS) * g_ref[...]
        h_scr[rows, :] = (y * (1.0 + m[1:2]) + m[0:1]).astype(BF16)
        cos = cos_ref[rows, :]
        sin = sin_ref[rows, :]

        def chunk(j):
            return jnp.dot(h_scr[rows, :], w_ref[:, 256 * j:256 * (j + 1)], preferred_element_type=F32)

        for j in range(2):
            for t, r in enumerate(halves(chunk(j))):
                c0 = 256 * j + LANES * t
                dq_ref[rows, c0:c0 + LANES] = (_rope(r, cos, sin) * scale).astype(BF16)
        for j in range(2):
            for t, r in enumerate(halves(chunk(2 + j))):
                c0 = 256 * j + LANES * t
                dk_ref[rows, c0:c0 + LANES] = _rope(r, cos, sin).astype(BF16)
        for j in range(2):
            dv_ref[rows, 256 * j:256 * (j + 1)] = chunk(4 + j).astype(BF16)
        for j in range(2):
            for t, r in enumerate(halves(chunk(6 + j))):
                rp = _rope(r, cos, sin) * scale
                c = 2 * j + t
                sq_ref[rows, LANES * c:LANES * (c + 1)] = jnp.where(head_a, rp, 0.0).astype(BF16)
                sq_ref[rows, LANES * (c + 4):LANES * (c + 5)] = jnp.where(head_a, 0.0, rp).astype(BF16)
        kv = chunk(8)
        kk, vv = halves(kv)
        kr = _rope(kk, cos, sin)
        sk_ref[rows, :LANES] = jnp.where(head_a, kr, 0.0).astype(BF16)
        sk_ref[rows, LANES:] = jnp.where(head_a, 0.0, kr).astype(BF16)
        sv_ref[rows, :LANES] = jnp.where(low64, vv, 0.0).astype(BF16)
        sv_ref[rows, LANES:] = jnp.where(low64, pltpu.roll(vv, HD, 1), 0.0).astype(BF16)


def _in_proj(x2, mod3, g_mix, w_in_bf, cos, sin, seq):
    t = x2.shape[0]
    tm = TM_IN
    per_b = seq // tm
    n_in = w_in_bf.shape[1]
    row = lambda i: (i, 0)
    return pl.pallas_call(
        _inproj_kernel,
        grid=(t // tm,),
        in_specs=[pl.BlockSpec((tm, D), row),
                  pl.BlockSpec((1, 6, D), lambda i: (i // per_b, 0, 0)),
                  pl.BlockSpec((1, D), lambda i: (0, 0)),
                  pl.BlockSpec((D, n_in), lambda i: (0, 0)),
                  pl.BlockSpec((tm, LANES), lambda i: (i % per_b, 0)),
                  pl.BlockSpec((tm, LANES), lambda i: (i % per_b, 0))],
        out_specs=[pl.BlockSpec((tm, 512), row), pl.BlockSpec((tm, 512), row),
                   pl.BlockSpec((tm, 512), row), pl.BlockSpec((tm, 1024), row),
                   pl.BlockSpec((tm, 256), row), pl.BlockSpec((tm, 256), row)],
        out_shape=[jax.ShapeDtypeStruct((t, 512), BF16), jax.ShapeDtypeStruct((t, 512), BF16),
                   jax.ShapeDtypeStruct((t, 512), BF16), jax.ShapeDtypeStruct((t, 1024), BF16),
                   jax.ShapeDtypeStruct((t, 256), BF16), jax.ShapeDtypeStruct((t, 256), BF16)],
        scratch_shapes=[pltpu.VMEM((tm, D), BF16)],
        compiler_params=_cparams(("arbitrary",)),
        name="in_proj",
    )(x2, mod3, g_mix, w_in_bf, cos, sin)


_NT_DIMS = (((1,), (1,)), ((), ()))


def _diff_kernel(lam_ref, g_ref, q_ref, k_ref, v_ref, o_ref):
    seq = q_ref.shape[0]
    tq = TQ
    lp = lam_ref[...]
    lam = (jnp.exp(jnp.sum(lp[0:1] * lp[1:2], axis=-1, keepdims=True))
           - jnp.exp(jnp.sum(lp[2:3] * lp[3:4], axis=-1, keepdims=True)) + LAMBDA_INIT)
    lane = lax.broadcasted_iota(I32, (tq, LANES), 1)
    map_a = (lane & 32) == 0
    causal = (lax.broadcasted_iota(I32, (tq, tq), 1) <= lax.broadcasted_iota(I32, (tq, tq), 0))
    gain = g_ref[...] * (1.0 - LAMBDA_INIT)

    for i in reversed(range(seq // tq)):
        lo, hi = i * tq, (i + 1) * tq
        q = q_ref[lo:hi, :]
        zero = jnp.zeros_like(q)
        probs = []
        for qm in (jnp.where(map_a, q, zero), jnp.where(map_a, zero, q)):
            s_dg = lax.dot_general(qm, k_ref[lo:hi, :], _NT_DIMS, preferred_element_type=F32)
            s_dg = jnp.where(causal, s_dg, -jnp.inf)
            mx = jnp.max(s_dg, axis=-1, keepdims=True)
            if i > 0:
                s_off = lax.dot_general(qm, k_ref[0:lo, :], _NT_DIMS, preferred_element_type=F32)
                mx = jnp.maximum(mx, jnp.max(s_off, axis=-1, keepdims=True))
                p_off = jnp.exp2(s_off - mx)
            p_dg = jnp.exp2(s_dg - mx)
            l = jnp.sum(p_dg, axis=-1, keepdims=True)
            if i > 0:
                l = l + jnp.sum(p_off, axis=-1, keepdims=True)
                probs.append((p_off, p_dg, l))
            else:
                probs.append((None, p_dg, l))
        (p0_off, p0_dg, l0), (p1_off, p1_dg, l1) = probs
        c = lam * l0 / l1
        o = jnp.dot((p0_dg - c * p1_dg).astype(BF16), v_ref[lo:hi, :], preferred_element_type=F32)
        if i > 0:
            o = o + jnp.dot((p0_off - c * p1_off).astype(BF16), v_ref[0:lo, :],
                            preferred_element_type=F32)
        o = o / l0
        ms = jnp.mean(o * o, axis=-1, keepdims=True)
        o_ref[lo:hi, :] = (o * lax.rsqrt(ms + EPS) * gain).astype(BF16)


def _diff_attn(dq, dk, dv, lam_p, g_sub, bsz, seq):
    t = dq.shape[0]
    blk = lambda b, h: (b, h)
    return pl.pallas_call(
        _diff_kernel,
        grid=(bsz, 4),
        in_specs=[pl.BlockSpec((4, HD), lambda b, h: (0, 0)),
                  pl.BlockSpec((1, LANES), lambda b, h: (0, 0)),
                  pl.BlockSpec((seq, LANES), blk),
                  pl.BlockSpec((seq, LANES), blk),
                  pl.BlockSpec((seq, LANES), blk)],
        out_specs=pl.BlockSpec((seq, LANES), blk),
        out_shape=jax.ShapeDtypeStruct((t, 512), BF16),
        compiler_params=_cparams(("arbitrary", "arbitrary")),
        name="diff_attn",
    )(lam_p, g_sub, dq, dk, dv)


def _swa_kernel(sink_ref, q_ref, k_ref, v_ref, o_ref):
    seq = q_ref.shape[0]
    g = pl.program_id(1)
    rows = 4 * BLK
    hsel = jnp.right_shift(lax.broadcasted_iota(I32, (rows, 8), 0), 7) + 4 * g
    sink_col = jnp.sum(jnp.where(lax.broadcasted_iota(I32, (rows, 8), 1) == hsel,
                                 sink_ref[...] * LOG2E, 0.0), axis=-1, keepdims=True)
    qi = lax.broadcasted_iota(I32, (rows, 2 * BLK), 0) & (BLK - 1)
    kj = lax.broadcasted_iota(I32, (rows, 2 * BLK), 1)
    dist = qi + BLK - kj
    band = (dist >= 0) & (dist < BLK)
    first = (lax.broadcasted_iota(I32, (rows, BLK), 1)
             <= (lax.broadcasted_iota(I32, (rows, BLK), 0) & (BLK - 1)))
    low64 = lax.broadcasted_iota(I32, (BLK, LANES), 1) < HD

    def attend(q_rows, k_rows, mask):
        qs = jnp.concatenate([q_ref[q_rows, LANES * j:LANES * (j + 1)] for j in range(4)], axis=0)
        s = lax.dot_general(qs, k_ref[k_rows, :], _NT_DIMS, preferred_element_type=F32)
        s = jnp.where(mask, s, -jnp.inf)
        mx = jnp.maximum(jnp.max(s, axis=-1, keepdims=True), sink_col)
        p = jnp.exp2(s - mx)
        l = jnp.sum(p, axis=-1, keepdims=True) + jnp.exp2(sink_col - mx)
        o = jnp.dot(p.astype(BF16), v_ref[k_rows, :], preferred_element_type=F32) / l
        for c in range(2):
            even = o[(2 * c) * BLK:(2 * c + 1) * BLK, :]
            odd = o[(2 * c + 1) * BLK:(2 * c + 2) * BLK, :]
            o_ref[q_rows, LANES * c:LANES * (c + 1)] = jnp.where(
                low64, even, pltpu.roll(odd, HD, 1)).astype(BF16)

    attend(pl.ds(0, BLK), pl.ds(0, BLK), first)

    for n in range(1, seq // BLK):
        attend(pl.ds(n * BLK, BLK), pl.ds((n - 1) * BLK, 2 * BLK), band)


def _swa_attn(sq, sk, sv, sinks, bsz, seq):
    t = sq.shape[0]
    return pl.pallas_call(
        _swa_kernel,
        grid=(bsz, 2),
        in_specs=[pl.BlockSpec((1, 8), lambda b, g: (0, 0)),
                  pl.BlockSpec((seq, 512), lambda b, g: (b, g)),
                  pl.BlockSpec((seq, LANES), lambda b, g: (b, g)),
                  pl.BlockSpec((seq, LANES), lambda b, g: (b, g))],
        out_specs=pl.BlockSpec((seq, 256), lambda b, g: (b, g)),
        out_shape=jax.ShapeDtypeStruct((t, 512), BF16),
        compiler_params=_cparams(("arbitrary", "arbitrary")),
        name="swa_attn",
    )(sinks, sq, sk, sv)


def _bf16_pieces(w):
    p0 = w.astype(BF16).astype(F32)
    r1 = w - p0
    p1 = r1.astype(BF16).astype(F32)
    return p0, p1, r1 - p1


def _outproj_kernel(od_ref, os_ref, x_ref, mod_ref, g_ref, wo_ref, wr_ref, br_ref, tri_ref,
                    x1_ref, h2_ref, bucket_ref, rank_ref, cnt_ref):
    tm = x_ref.shape[0]
    m = mod_ref[0]
    logit_rows = []
    for r0 in range(0, tm, TM_OUT_SUB):
        rows = slice(r0, r0 + TM_OUT_SUB)
        mix = (jnp.dot(od_ref[rows, :], wo_ref[0:512, :], preferred_element_type=F32)
               + jnp.dot(os_ref[rows, :], wo_ref[512:1024, :], preferred_element_type=F32))
        x1 = x_ref[rows, :] + m[2:3] * mix
        x1_ref[rows, :] = x1
        ms = jnp.mean(x1 * x1, axis=-1, keepdims=True)
        h2 = (x1 * lax.rsqrt(ms + EPS) * g_ref[...] * (1.0 + m[4:5]) + m[3:4]).astype(BF16)
        h2_ref[rows, 0:D] = h2
        logit_rows.append(jnp.dot(h2, wr_ref[...], preferred_element_type=F32) + br_ref[...])
    lt = jnp.concatenate(logit_rows, axis=0).T
    r = [lt[i:i + 1, :] for i in range(N_GROUPS + N_GROUPS * EPG)]

    gl = r[0:N_GROUPS]
    gmax = jnp.maximum(jnp.maximum(gl[0], gl[1]), jnp.maximum(gl[2], gl[3]))
    gidx = jnp.where(gl[0] == gmax, 0, jnp.where(gl[1] == gmax, 1, jnp.where(gl[2] == gmax, 2, 3)))
    gz = (jnp.exp(gl[0] - gmax) + jnp.exp(gl[1] - gmax)
          + jnp.exp(gl[2] - gmax) + jnp.exp(gl[3] - gmax))
    grp_p = 1.0 / gz

    el = []
    for i in range(EPG):
        e = r[N_GROUPS + 3 * EPG + i]
        for gg in (2, 1, 0):
            e = jnp.where(gidx == gg, r[N_GROUPS + gg * EPG + i], e)
        el.append(e)
    emax = jnp.maximum(jnp.maximum(el[0], el[1]), jnp.maximum(el[2], el[3]))
    ex = [jnp.exp(e - emax) for e in el]
    ez = ex[0] + ex[1] + ex[2] + ex[3]
    pr = [e / ez for e in ex]
    p1 = jnp.maximum(jnp.maximum(pr[0], pr[1]), jnp.maximum(pr[2], pr[3]))
    a = jnp.where(pr[0] == p1, 0, jnp.where(pr[1] == p1, 1, jnp.where(pr[2] == p1, 2, 3)))
    rest = [jnp.where(a == i, -1.0, pr[i]) for i in range(EPG)]
    p2 = jnp.maximum(jnp.maximum(rest[0], rest[1]), jnp.maximum(rest[2], rest[3]))
    b = jnp.where(rest[0] == p2, 0, jnp.where(rest[1] == p2, 1, jnp.where(rest[2] == p2, 2, 3)))
    psum = p1 + p2
    wa = grp_p * (p1 / psum)
    wb = grp_p * (p2 / psum)
    lo = jnp.minimum(a, b)
    hi = jnp.maximum(a, b)
    w_lo = jnp.where(a < b, wa, wb)
    w_hi = jnp.where(a < b, wb, wa)
    pair = jnp.where(lo == 0, hi - 1, jnp.where(lo == 1, hi + 1, 5))
    bucket = gidx * 6 + pair

    rid = lax.broadcasted_iota(I32, (AUG, tm), 0)
    aug = jnp.zeros((AUG, tm), F32)
    for k, piece in enumerate(_bf16_pieces(w_lo) + _bf16_pieces(w_hi)):
        aug = jnp.where(rid == k, piece, aug)
    h2_ref[:, D:D + AUG] = aug.T.astype(BF16)

    oh = (lax.broadcasted_iota(I32, (BUCKET_ROWS, tm), 0) == bucket)
    oh_f = jnp.where(oh, 1.0, 0.0)
    before = jnp.dot(oh_f.astype(BF16), tri_ref[...], preferred_element_type=F32)
    rank = jnp.sum(oh_f * before, axis=0, keepdims=True)
    cnt_ref[0] = jnp.broadcast_to(jnp.sum(oh_f, axis=1, keepdims=True), (BUCKET_ROWS, LANES))
    bucket_ref[0] = bucket
    rank_ref[0] = rank.astype(I32)


def _out_proj(o_diff, o_swa, x2, mod3, g_ffn, w_out_bf, w_r, b_r, tri, seq):
    t = x2.shape[0]
    tm = TM_PROJ
    per_b = seq // tm
    row = lambda i: (i, 0)
    const = lambda i: (0, 0)
    return pl.pallas_call(
        _outproj_kernel,
        grid=(t // tm,),
        in_specs=[pl.BlockSpec((tm, 512), row), pl.BlockSpec((tm, 512), row),
                  pl.BlockSpec((tm, D), row),
                  pl.BlockSpec((1, 6, D), lambda i: (i // per_b, 0, 0)),
                  pl.BlockSpec((1, D), const),
                  pl.BlockSpec((D, D), const),
                  pl.BlockSpec((D, LANES), const),
                  pl.BlockSpec((1, LANES), const),
                  pl.BlockSpec((tm, tm), const)],
        out_specs=[pl.BlockSpec((tm, D), row),
                   pl.BlockSpec((tm, D + AUG), row),
                   pl.BlockSpec((1, 1, tm), lambda i: (i, 0, 0)),
                   pl.BlockSpec((1, 1, tm), lambda i: (i, 0, 0)),
                   pl.BlockSpec((1, BUCKET_ROWS, LANES), lambda i: (i, 0, 0))],
        out_shape=[jax.ShapeDtypeStruct((t, D), F32),
                   jax.ShapeDtypeStruct((t, D + AUG), BF16),
                   jax.ShapeDtypeStruct((t // tm, 1, tm), I32),
                   jax.ShapeDtypeStruct((t // tm, 1, tm), I32),
                   jax.ShapeDtypeStruct((t // tm, BUCKET_ROWS, LANES), F32)],
        compiler_params=_cparams(("arbitrary",)),
        name="out_proj",
    )(o_diff, o_swa, x2, mod3, g_ffn, w_out_bf, w_r, b_r, tri)


META_LOCAL, META_GLOBAL, META_ROWS, META_TILE_ROWS = 0, 1, 2, 3


def _local_pos(meta_ref, tile, bucket, rank):
    pos = rank
    for b in range(N_BUCKETS):
        pos = pos + jnp.where(bucket == b, meta_ref[META_LOCAL, tile * N_BUCKETS + b], 0)
    return pos


def _for_each_run(meta_ref, tile, fn):
    for b in range(N_BUCKETS):
        rows = meta_ref[META_ROWS, tile * N_BUCKETS + b]

        @pl.when(rows > 0)
        def _(b=b, rows=rows):
            fn(pl.multiple_of(meta_ref[META_LOCAL, tile * N_BUCKETS + b], SUBLANES),
               pl.multiple_of(meta_ref[META_GLOBAL, tile * N_BUCKETS + b], SUBLANES),
               pl.multiple_of(rows, SUBLANES))


def _dispatch_kernel(meta_ref, zfill_ref, h2_ref, bucket_ref, rank_ref, xs_hbm,
                     sort_scr, zero_scr, sems, zsem):
    i = pl.program_id(0)
    n_steps = pl.num_programs(0)
    n_tiles = zfill_ref.shape[0]
    tm = h2_ref.shape[0]
    slot = i % 2

    @pl.when(i == 0)
    def _():
        zero_scr[...] = jnp.zeros_like(zero_scr)

        def zcopy(tile):
            return pltpu.make_async_copy(
                zero_scr, xs_hbm.at[pl.ds(pl.multiple_of(tile * TM_MOE, TM_MOE), TM_MOE)], zsem)

        def zstart(tile, carry):
            @pl.when(zfill_ref[tile] != 0)
            def _():
                zcopy(tile).start()
            return carry

        lax.fori_loop(0, n_tiles, zstart, 0)

        def zwait(tile, carry):
            @pl.when(zfill_ref[tile] != 0)
            def _():
                zcopy(tile).wait()
            return carry

        lax.fori_loop(0, n_tiles, zwait, 0)

    def run_copy(buf, local_row, global_row, rows):
        return pltpu.make_async_copy(sort_scr.at[buf, pl.ds(local_row, rows)],
                                     xs_hbm.at[pl.ds(global_row, rows)], sems.at[buf])

    def wait_runs(tile, buf):
        run_copy(buf, 0, 0, pl.multiple_of(meta_ref[META_TILE_ROWS, tile], SUBLANES)).wait()

    @pl.when(i >= 2)
    def _():
        wait_runs(i - 2, slot)

    lpos = _local_pos(meta_ref, i, bucket_ref[0], rank_ref[0])
    lane = lax.broadcasted_iota(I32, (LS_SUB, AUG), 1)
    for r0 in range(0, LS, LS_SUB):
        row_id = lax.broadcasted_iota(I32, (LS_SUB, tm), 0) + r0
        onehot = jnp.where(row_id == lpos, 1.0, 0.0).astype(BF16)
        srt = jnp.dot(onehot, h2_ref[...], preferred_element_type=F32)
        aug = srt[:, D:D + AUG]
        w_lo = aug[:, 0:1] + aug[:, 1:2] + aug[:, 2:3]
        w_hi = aug[:, 3:4] + aug[:, 4:5] + aug[:, 5:6]
        sort_scr[slot, r0:r0 + LS_SUB, 0:D] = srt[:, 0:D]
        sort_scr[slot, r0:r0 + LS_SUB, D:XW] = jnp.where(lane == 0, w_lo, jnp.where(lane == 1, w_hi, 0.0))
    _for_each_run(meta_ref, i, lambda lr, gr, rows: run_copy(slot, lr, gr, rows).start())

    @pl.when(i == n_steps - 1)
    def _():
        @pl.when(i >= 1)
        def _():
            wait_runs(i - 1, 1 - slot)
        wait_runs(i, slot)


def _dispatch(meta, zfill, h2a, bucket, rank):
    t, width = h2a.shape
    tm = TM_PROJ
    n_rows = zfill.shape[0] * TM_MOE
    grid_spec = pltpu.PrefetchScalarGridSpec(
        num_scalar_prefetch=2,
        grid=(t // tm,),
        in_specs=[pl.BlockSpec((tm, width), lambda i, m, z: (i, 0)),
                  pl.BlockSpec((1, 1, tm), lambda i, m, z: (i, 0, 0)),
                  pl.BlockSpec((1, 1, tm), lambda i, m, z: (i, 0, 0))],
        out_specs=pl.BlockSpec(memory_space=pl.ANY),
        scratch_shapes=[pltpu.VMEM((2, LS, XW), F32), pltpu.VMEM((TM_MOE, XW), F32),
                        pltpu.SemaphoreType.DMA((2,)), pltpu.SemaphoreType.DMA(())],
    )
    return pl.pallas_call(
        _dispatch_kernel,
        grid_spec=grid_spec,
        out_shape=jax.ShapeDtypeStruct((n_rows, XW), F32),
        compiler_params=_cparams(("arbitrary",)),
        name="dispatch",
    )(meta, zfill, h2a, bucket, rank)


def _moe_kernel(lo_ref, hi_ref, valid_ref, xs_ref, wg_lo, wu_lo, wd_lo, wg_hi, wu_hi, wd_hi, y_ref):
    del lo_ref, hi_ref
    t = pl.program_id(0)

    @pl.when(valid_ref[t] != 0)
    def _():
        xb = xs_ref[:, 0:D].astype(BF16)
        y = None
        for k, (wg, wu, wd) in enumerate(((wg_lo, wu_lo, wd_lo), (wg_hi, wu_hi, wd_hi))):
            gt = jnp.dot(xb, wg[0], preferred_element_type=F32)
            up = jnp.dot(xb, wu[0], preferred_element_type=F32)
            act = (gt * (1.0 / (1.0 + jnp.exp(-gt))) * up).astype(BF16)
            dn = jnp.dot(act, wd[0], preferred_element_type=F32)
            term = xs_ref[:, D + k:D + k + 1] * dn
            y = term if y is None else y + term
        y_ref[...] = y

    @pl.when(valid_ref[t] == 0)
    def _():
        y_ref[...] = jnp.zeros_like(y_ref)


def _moe(tile_lo, tile_hi, tile_valid, xs, wg, wu, wd):
    n_rows, width = xs.shape
    n_tiles = n_rows // TM_MOE
    up_lo = lambda t, lo, hi, v: (lo[t], 0, 0)
    up_hi = lambda t, lo, hi, v: (hi[t], 0, 0)
    grid_spec = pltpu.PrefetchScalarGridSpec(
        num_scalar_prefetch=3,
        grid=(n_tiles,),
        in_specs=[pl.BlockSpec((TM_MOE, width), lambda t, lo, hi, v: (t, 0)),
                  pl.BlockSpec((1, D, D_EXPERT), up_lo), pl.BlockSpec((1, D, D_EXPERT), up_lo),
                  pl.BlockSpec((1, D_EXPERT, D), up_lo),
                  pl.BlockSpec((1, D, D_EXPERT), up_hi), pl.BlockSpec((1, D, D_EXPERT), up_hi),
                  pl.BlockSpec((1, D_EXPERT, D), up_hi)],
        out_specs=pl.BlockSpec((TM_MOE, D), lambda t, lo, hi, v: (t, 0)),
    )
    return pl.pallas_call(
        _moe_kernel,
        grid_spec=grid_spec,
        out_shape=jax.ShapeDtypeStruct((n_rows, D), F32),
        compiler_params=_cparams(("arbitrary",)),
        name="moe",
    )(tile_lo, tile_hi, tile_valid, xs, wg, wu, wd, wg, wu, wd)


def _final_kernel(meta_ref, ys_hbm, x1_ref, mod_ref, g_ref, bucket_ref, rank_ref, o_ref, y_scr, sems):
    i = pl.program_id(0)
    n_steps = pl.num_programs(0)
    tm = x1_ref.shape[0]
    slot = i % 2

    def run_copy(buf, local_row, global_row, rows):
        return pltpu.make_async_copy(ys_hbm.at[pl.ds(global_row, rows)],
                                     y_scr.at[buf, pl.ds(local_row, rows)], sems.at[buf])

    def fetch_runs(tile, buf):
        _for_each_run(meta_ref, tile, lambda lr, gr, rows: run_copy(buf, lr, gr, rows).start())

    @pl.when(i == 0)
    def _():
        y_scr[...] = jnp.zeros_like(y_scr)
        fetch_runs(0, 0)

    @pl.when(i + 1 < n_steps)
    def _():
        fetch_runs(i + 1, 1 - slot)

    run_copy(slot, 0, 0, pl.multiple_of(meta_ref[META_TILE_ROWS, i], SUBLANES)).wait()

    lpos = _local_pos(meta_ref, i, bucket_ref[0], rank_ref[0])
    lpos_col = jnp.broadcast_to(lpos.astype(F32), (LANES, tm)).T[:, 0:1].astype(I32)
    ysrt = y_scr[slot].astype(BF16)
    m = mod_ref[0]
    for r0 in range(0, tm, TM_OUT_SUB):
        rows = slice(r0, r0 + TM_OUT_SUB)
        onehot = jnp.where(lax.broadcasted_iota(I32, (TM_OUT_SUB, LS), 1) == lpos_col[rows], 1.0, 0.0)
        y = jnp.dot(onehot.astype(BF16), ysrt, preferred_element_type=F32)
        x2 = x1_ref[rows, :] + m[5:6] * y
        ms = jnp.mean(x2 * x2, axis=-1, keepdims=True)
        o_ref[rows, :] = x2 * lax.rsqrt(ms + EPS) * g_ref[...]


def _final(meta, ys, x1, mod3, g_final, bucket, rank, seq):
    t = x1.shape[0]
    tm = TM_PROJ
    per_b = seq // tm
    grid_spec = pltpu.PrefetchScalarGridSpec(
        num_scalar_prefetch=1,
        grid=(t // tm,),
        in_specs=[pl.BlockSpec(memory_space=pl.ANY),
                  pl.BlockSpec((tm, D), lambda i, m: (i, 0)),
                  pl.BlockSpec((1, 6, D), lambda i, m: (i // per_b, 0, 0)),
                  pl.BlockSpec((1, D), lambda i, m: (0, 0)),
                  pl.BlockSpec((1, 1, tm), lambda i, m: (i, 0, 0)),
                  pl.BlockSpec((1, 1, tm), lambda i, m: (i, 0, 0))],
        out_specs=pl.BlockSpec((tm, D), lambda i, m: (i, 0)),
        scratch_shapes=[pltpu.VMEM((2, LS, D), F32), pltpu.SemaphoreType.DMA((2,))],
    )
    return pl.pallas_call(
        _final_kernel,
        grid_spec=grid_spec,
        out_shape=jax.ShapeDtypeStruct((t, D), F32),
        compiler_params=_cparams(("arbitrary",)),
        name="final",
    )(meta, ys, x1, mod3, g_final, bucket, rank)


def _rope_tables(seq):
    inv = ROPE_THETA ** (-jnp.arange(0, HD, 2, dtype=F32) / HD)
    ang = jnp.arange(seq, dtype=F32)[:, None] * inv[None, :]
    cos, sin = jnp.cos(ang), jnp.sin(ang)
    return jnp.tile(cos, (1, 4)), jnp.concatenate([-sin, -sin, sin, sin], axis=-1)


def _rotary_column_order():
    half = HD // 2
    lane = np.arange(LANES)
    which, part, f = (lane // half) % 2, lane // HD, lane % half
    cols = []
    for base in (0, 512):
        for h in range(4):
            cols.append(base + h * LANES + which * HD + part * half + f)
    cols.append(np.arange(1024, 1536))
    for c in range(4):
        cols.append(1536 + (c + 4 * which) * HD + part * half + f)
    cols.append(2048 + which * HD + part * half + f)
    cols.append(np.arange(2176, 2304))
    return np.concatenate(cols)


def kernel(x, c, w_ada, b_ada, g_mix, w_in, diff_lambda, g_diff_sub, swa_sinks, w_out, g_ffn,
           w_route_group, b_route_group, w_route_expert, b_route_expert, w_gate, w_up, w_down,
           g_final):
    bsz, seq, _ = x.shape
    t = bsz * seq
    x2 = x.reshape(t, D)
    cos, sin = _rope_tables(seq)

    mod3 = _ada_mod(c, w_ada[0], b_ada[0].reshape(1, -1)).reshape(bsz, 6, D)

    w_in_bf = w_in[0][:, _rotary_column_order()].astype(BF16)
    dq, dk, dv, sq, sk, sv = _in_proj(x2, mod3, g_mix[0].reshape(1, D), w_in_bf, cos, sin, seq)
    o_diff = _diff_attn(dq, dk, dv, diff_lambda[0], g_diff_sub[0].reshape(1, LANES), bsz, seq)
    o_swa = _swa_attn(sq, sk, sv, swa_sinks[0].reshape(1, 8), bsz, seq)

    n_r = N_GROUPS + N_GROUPS * EPG
    w_r = jnp.concatenate([w_route_group[0], w_route_expert[0],
                           jnp.zeros((D, LANES - n_r), F32)], axis=1).astype(BF16)
    b_r = jnp.concatenate([b_route_group[0], b_route_expert[0],
                           jnp.zeros((LANES - n_r,), F32)]).reshape(1, LANES)
    tri = (jnp.arange(TM_PROJ)[:, None] < jnp.arange(TM_PROJ)[None, :]).astype(BF16)
    x1, h2a, bucket, rank, counts = _out_proj(o_diff, o_swa, x2, mod3, g_ffn[0].reshape(1, D),
                                              w_out[0].astype(BF16), w_r, b_r, tri, seq)

    n_tok_tiles = t // TM_PROJ
    n_tiles = -(-(t + n_tok_tiles * N_BUCKETS * (SUBLANES - 1)) // TM_MOE) + N_BUCKETS
    cnt = counts[:, :N_BUCKETS, 0].astype(I32)
    run_rows = (cnt + SUBLANES - 1) // SUBLANES * SUBLANES
    local_off = jnp.cumsum(run_rows, axis=1) - run_rows
    tiles_per = (jnp.sum(run_rows, axis=0) + TM_MOE - 1) // TM_MOE
    tile_end = jnp.cumsum(tiles_per)
    bucket_start = (tile_end - tiles_per) * TM_MOE
    global_off = bucket_start[None, :] + jnp.cumsum(run_rows, axis=0) - run_rows
    tile_rows = jnp.pad(jnp.sum(run_rows, axis=1), (0, n_tok_tiles * (N_BUCKETS - 1)))
    meta = jnp.stack([local_off.reshape(-1), global_off.reshape(-1), run_rows.reshape(-1), tile_rows])
    tid = jnp.arange(n_tiles, dtype=I32)
    tile_bucket = jnp.minimum(jnp.sum(tid[:, None] >= tile_end[None, :], axis=1), N_BUCKETS - 1)
    tile_valid = (tid < tile_end[-1]).astype(I32)
    last_of_bucket = jnp.sum(tid[:, None] == (tile_end - 1)[None, :], axis=1) > 0
    zfill = (last_of_bucket | (tile_valid == 0)).astype(I32)
    grp, pair = tile_bucket // 6, tile_bucket % 6
    tile_lo = (grp * EPG + jnp.asarray(PAIR_LO, I32)[pair]).astype(I32)
    tile_hi = (grp * EPG + jnp.asarray(PAIR_HI, I32)[pair]).astype(I32)

    xs = _dispatch(meta, zfill, h2a, bucket, rank)
    ys = _moe(tile_lo, tile_hi, tile_valid, xs, w_gate[0].astype(BF16), w_up[0].astype(BF16),
              w_down[0].astype(BF16))
    out = _final(meta, ys, x1, mod3, g_final.reshape(1, D), bucket, rank, seq)
    return out.reshape(bsz, seq, D)
```

```python
import functools
import math

import jax
import jax.numpy as jnp
import numpy as np
from jax import lax
from jax.experimental import pallas as pl
from jax.experimental.pallas import tpu as pltpu

F32 = jnp.float32
BF16 = jnp.bfloat16
I32 = jnp.int32

D = 1024
HD = 64
EPS = 1e-6
ROPE_THETA = 10000.0
LOG2E = math.log2(math.e)
LAMBDA_INIT = 0.8 - 0.6 * math.exp(-0.3 * 0)
N_GROUPS = 4
EPG = 4
N_BUCKETS = 24
BUCKET_ROWS = 32
D_EXPERT = 512
AUG = 128

LANES = 128
VMEM_LIMIT = 48 * 1024 * 1024

TM_IN = 1024
TM_IN_SUB = 256
TM_PROJ = 512
TM_OUT_SUB = 256
TQ = 512
BLK = 128
TM_MOE = 512
SUBLANES = 8
RUN_ROWS = TM_PROJ + N_BUCKETS * (SUBLANES - 1)
LS = -(-RUN_ROWS // 64) * 64
LS_SUB = LS // 2
XW = D + AUG

PAIR_LO = (0, 0, 0, 1, 1, 2)
PAIR_HI = (1, 2, 3, 2, 3, 3)


def _cparams(sem):
    return pltpu.CompilerParams(dimension_semantics=sem, vmem_limit_bytes=VMEM_LIMIT)


RING = 3


def _ring_fetch(src_hbm, ring, sems, step, n_steps):
    rows = ring.shape[1]

    def copy(s):
        return pltpu.make_async_copy(src_hbm.at[pl.ds(pl.multiple_of(s * rows, rows), rows)],
                                     ring.at[s % RING], sems.at[s % RING])

    @pl.when(step == 0)
    def _():
        for s in range(min(RING - 1, n_steps)):
            copy(s).start()

    @pl.when(step + RING - 1 < n_steps)
    def _():
        copy(step + RING - 1).start()

    copy(step).wait()
    return ring.at[step % RING]


def _ada_kernel(c_ref, w_ref, b_ref, o_ref):
    c = c_ref[...]
    ca = c * (1.0 / (1.0 + jnp.exp(-c)))
    o_ref[...] = jnp.dot(ca.astype(BF16), w_ref[...].astype(BF16),
                         preferred_element_type=F32) + b_ref[...]


def _ada_mod(c, w, b):
    bsz = c.shape[0]
    n = w.shape[1]
    tn = 512
    return pl.pallas_call(
        _ada_kernel,
        grid=(n // tn,),
        in_specs=[pl.BlockSpec((bsz, D), lambda j: (0, 0)),
                  pl.BlockSpec((D, tn), lambda j: (0, j)),
                  pl.BlockSpec((1, tn), lambda j: (0, j))],
        out_specs=pl.BlockSpec((bsz, tn), lambda j: (0, j)),
        out_shape=jax.ShapeDtypeStruct((bsz, n), F32),
        compiler_params=_cparams(("arbitrary",)),
        name="ada_mod",
    )(c, w, b)


def _rope(a, cos, sin):
    return a * cos + pltpu.roll(a, HD, 1) * sin


def _inproj_kernel(x_ref, mod_ref, g_ref, w_ref, cos_ref, sin_ref,
                   dq_ref, dk_ref, dv_ref, sq_ref, sk_ref, sv_ref, h_scr):
    tm = x_ref.shape[0]
    sub = TM_IN_SUB
    m = mod_ref[0]
    lane = lax.broadcasted_iota(I32, (sub, LANES), 1)
    head_a = (lane & 32) == 0
    low64 = lane < HD
    scale = (HD ** -0.5) * LOG2E

    def halves(a):
        return a[:, :LANES], a[:, LANES:]

    for r0 in range(0, tm, sub):
        rows = slice(r0, r0 + sub)
        x = x_ref[rows, :]
        ms = jnp.mean(x * x, axis=-1, keepdims=True)
        y = x * lax.rsqrt(ms + EPS) * g_ref[...]
        h_scr[rows, :] = (y * (1.0 + m[1:2]) + m[0:1]).astype(BF16)
        cos = cos_ref[rows, :]
        sin = sin_ref[rows, :]

        def chunk(j):
            return jnp.dot(h_scr[rows, :], w_ref[:, 256 * j:256 * (j + 1)], preferred_element_type=F32)

        for j in range(2):
            for t, r in enumerate(halves(chunk(j))):
                c0 = 256 * j + LANES * t
                dq_ref[rows, c0:c0 + LANES] = (_rope(r, cos, sin) * scale).astype(BF16)
        for j in range(2):
            for t, r in enumerate(halves(chunk(2 + j))):
                c0 = 256 * j + LANES * t
                dk_ref[rows, c0:c0 + LANES] = _rope(r, cos, sin).astype(BF16)
        for j in range(2):
            dv_ref[rows, 256 * j:256 * (j + 1)] = chunk(4 + j).astype(BF16)
        for j in range(2):
            for t, r in enumerate(halves(chunk(6 + j))):
                rp = _rope(r, cos, sin) * scale
                c = 2 * j + t
                sq_ref[rows, LANES * c:LANES * (c + 1)] = jnp.where(head_a, rp, 0.0).astype(BF16)
                sq_ref[rows, LANES * (c + 4):LANES * (c + 5)] = jnp.where(head_a, 0.0, rp).astype(BF16)
        kv = chunk(8)
        kk, vv = halves(kv)
        kr = _rope(kk, cos, sin)
        sk_ref[rows, :LANES] = jnp.where(head_a, kr, 0.0).astype(BF16)
        sk_ref[rows, LANES:] = jnp.where(head_a, 0.0, kr).astype(BF16)
        sv_ref[rows, :LANES] = jnp.where(low64, vv, 0.0).astype(BF16)
        sv_ref[rows, LANES:] = jnp.where(low64, pltpu.roll(vv, HD, 1), 0.0).astype(BF16)


def _in_proj(x2, mod3, g_mix, w_in_bf, cos, sin, seq):
    t = x2.shape[0]
    tm = TM_IN
    per_b = seq // tm
    n_in = w_in_bf.shape[1]
    row = lambda i: (i, 0)
    return pl.pallas_call(
        _inproj_kernel,
        grid=(t // tm,),
        in_specs=[pl.BlockSpec((tm, D), row),
                  pl.BlockSpec((1, 6, D), lambda i: (i // per_b, 0, 0)),
                  pl.BlockSpec((1, D), lambda i: (0, 0)),
                  pl.BlockSpec((D, n_in), lambda i: (0, 0)),
                  pl.BlockSpec((tm, LANES), lambda i: (i % per_b, 0)),
                  pl.BlockSpec((tm, LANES), lambda i: (i % per_b, 0))],
        out_specs=[pl.BlockSpec((tm, 512), row), pl.BlockSpec((tm, 512), row),
                   pl.BlockSpec((tm, 512), row), pl.BlockSpec((tm, 1024), row),
                   pl.BlockSpec((tm, 256), row), pl.BlockSpec((tm, 256), row)],
        out_shape=[jax.ShapeDtypeStruct((t, 512), BF16), jax.ShapeDtypeStruct((t, 512), BF16),
                   jax.ShapeDtypeStruct((t, 512), BF16), jax.ShapeDtypeStruct((t, 1024), BF16),
                   jax.ShapeDtypeStruct((t, 256), BF16), jax.ShapeDtypeStruct((t, 256), BF16)],
        scratch_shapes=[pltpu.VMEM((tm, D), BF16)],
        compiler_params=_cparams(("arbitrary",)),
        name="in_proj",
    )(x2, mod3, g_mix, w_in_bf, cos, sin)


_NT_DIMS = (((1,), (1,)), ((), ()))


def _diff_kernel(lam_ref, g_ref, q_ref, k_ref, v_ref, o_ref):
    seq = q_ref.shape[0]
    tq = TQ
    lp = lam_ref[...]
    lam = (jnp.exp(jnp.sum(lp[0:1] * lp[1:2], axis=-1, keepdims=True))
           - jnp.exp(jnp.sum(lp[2:3] * lp[3:4], axis=-1, keepdims=True)) + LAMBDA_INIT)
    lane = lax.broadcasted_iota(I32, (tq, LANES), 1)
    map_a = (lane & 32) == 0
    causal = (lax.broadcasted_iota(I32, (tq, tq), 1) <= lax.broadcasted_iota(I32, (tq, tq), 0))
    gain = g_ref[...] * (1.0 - LAMBDA_INIT)

    for i in reversed(range(seq // tq)):
        lo, hi = i * tq, (i + 1) * tq
        q = q_ref[lo:hi, :]
        zero = jnp.zeros_like(q)
        probs = []
        for qm in (jnp.where(map_a, q, zero), jnp.where(map_a, zero, q)):
            s_dg = lax.dot_general(qm, k_ref[lo:hi, :], _NT_DIMS, preferred_element_type=F32)
            s_dg = jnp.where(causal, s_dg, -jnp.inf)
            mx = jnp.max(s_dg, axis=-1, keepdims=True)
            if i > 0:
                s_off = lax.dot_general(qm, k_ref[0:lo, :], _NT_DIMS, preferred_element_type=F32)
                mx = jnp.maximum(mx, jnp.max(s_off, axis=-1, keepdims=True))
                p_off = jnp.exp2(s_off - mx)
            p_dg = jnp.exp2(s_dg - mx)
            l = jnp.sum(p_dg, axis=-1, keepdims=True)
            if i > 0:
                l = l + jnp.sum(p_off, axis=-1, keepdims=True)
                probs.append((p_off, p_dg, l))
            else:
                probs.append((None, p_dg, l))
        (p0_off, p0_dg, l0), (p1_off, p1_dg, l1) = probs
        c = lam * l0 / l1
        o = jnp.dot((p0_dg - c * p1_dg).astype(BF16), v_ref[lo:hi, :], preferred_element_type=F32)
        if i > 0:
            o = o + jnp.dot((p0_off - c * p1_off).astype(BF16), v_ref[0:lo, :],
                            preferred_element_type=F32)
        o = o / l0
        ms = jnp.mean(o * o, axis=-1, keepdims=True)
        o_ref[lo:hi, :] = (o * lax.rsqrt(ms + EPS) * gain).astype(BF16)


def _diff_attn(dq, dk, dv, lam_p, g_sub, bsz, seq):
    t = dq.shape[0]
    blk = lambda b, h: (b, h)
    return pl.pallas_call(
        _diff_kernel,
        grid=(bsz, 4),
        in_specs=[pl.BlockSpec((4, HD), lambda b, h: (0, 0)),
                  pl.BlockSpec((1, LANES), lambda b, h: (0, 0)),
                  pl.BlockSpec((seq, LANES), blk),
                  pl.BlockSpec((seq, LANES), blk),
                  pl.BlockSpec((seq, LANES), blk)],
        out_specs=pl.BlockSpec((seq, LANES), blk),
        out_shape=jax.ShapeDtypeStruct((t, 512), BF16),
        compiler_params=_cparams(("arbitrary", "arbitrary")),
        name="diff_attn",
    )(lam_p, g_sub, dq, dk, dv)


def _swa_kernel(sink_ref, q_ref, k_ref, v_ref, o_ref):
    seq = q_ref.shape[0]
    g = pl.program_id(1)
    rows = 4 * BLK
    hsel = jnp.right_shift(lax.broadcasted_iota(I32, (rows, 8), 0), 7) + 4 * g
    sink_col = jnp.sum(jnp.where(lax.broadcasted_iota(I32, (rows, 8), 1) == hsel,
                                 sink_ref[...] * LOG2E, 0.0), axis=-1, keepdims=True)
    qi = lax.broadcasted_iota(I32, (rows, 2 * BLK), 0) & (BLK - 1)
    kj = lax.broadcasted_iota(I32, (rows, 2 * BLK), 1)
    dist = qi + BLK - kj
    band = (dist >= 0) & (dist < BLK)
    first = (lax.broadcasted_iota(I32, (rows, BLK), 1)
             <= (lax.broadcasted_iota(I32, (rows, BLK), 0) & (BLK - 1)))
    low64 = lax.broadcasted_iota(I32, (BLK, LANES), 1) < HD

    def attend(q_rows, k_rows, mask):
        qs = jnp.concatenate([q_ref[q_rows, LANES * j:LANES * (j + 1)] for j in range(4)], axis=0)
        s = lax.dot_general(qs, k_ref[k_rows, :], _NT_DIMS, preferred_element_type=F32)
        s = jnp.where(mask, s, -jnp.inf)
        mx = jnp.maximum(jnp.max(s, axis=-1, keepdims=True), sink_col)
        p = jnp.exp2(s - mx)
        l = jnp.sum(p, axis=-1, keepdims=True) + jnp.exp2(sink_col - mx)
        o = jnp.dot(p.astype(BF16), v_ref[k_rows, :], preferred_element_type=F32) / l
        for c in range(2):
            even = o[(2 * c) * BLK:(2 * c + 1) * BLK, :]
            odd = o[(2 * c + 1) * BLK:(2 * c + 2) * BLK, :]
            o_ref[q_rows, LANES * c:LANES * (c + 1)] = jnp.where(
                low64, even, pltpu.roll(odd, HD, 1)).astype(BF16)

    attend(pl.ds(0, BLK), pl.ds(0, BLK), first)

    for n in range(1, seq // BLK):
        attend(pl.ds(n * BLK, BLK), pl.ds((n - 1) * BLK, 2 * BLK), band)


def _swa_attn(sq, sk, sv, sinks, bsz, seq):
    t = sq.shape[0]
    return pl.pallas_call(
        _swa_kernel,
        grid=(bsz, 2),
        in_specs=[pl.BlockSpec((1, 8), lambda b, g: (0, 0)),
                  pl.BlockSpec((seq, 512), lambda b, g: (b, g)),
                  pl.BlockSpec((seq, LANES), lambda b, g: (b, g)),
                  pl.BlockSpec((seq, LANES), lambda b, g: (b, g))],
        out_specs=pl.BlockSpec((seq, 256), lambda b, g: (b, g)),
        out_shape=jax.ShapeDtypeStruct((t, 512), BF16),
        compiler_params=_cparams(("arbitrary", "arbitrary")),
        name="swa_attn",
    )(sinks, sq, sk, sv)


def _bf16_pieces(w):
    p0 = w.astype(BF16).astype(F32)
    r1 = w - p0
    p1 = r1.astype(BF16).astype(F32)
    return p0, p1, r1 - p1


def _outproj_kernel(od_ref, os_ref, x_hbm, mod_ref, g_ref, wo_ref, wr_ref, br_ref, tri_ref,
                    x1_ref, h2_ref, bucket_ref, rank_ref, cnt_ref, x_ring, x_sems, *, n_steps):
    tm = x1_ref.shape[0]
    x_ref = _ring_fetch(x_hbm, x_ring, x_sems, pl.program_id(0), n_steps)
    m = mod_ref[0]
    logit_rows = []
    for r0 in range(0, tm, TM_OUT_SUB):
        rows = slice(r0, r0 + TM_OUT_SUB)
        mix = (jnp.dot(od_ref[rows, :], wo_ref[0:512, :], preferred_element_type=F32)
               + jnp.dot(os_ref[rows, :], wo_ref[512:1024, :], preferred_element_type=F32))
        x1 = x_ref[rows, :] + m[2:3] * mix
        x1_ref[rows, :] = x1
        ms = jnp.mean(x1 * x1, axis=-1, keepdims=True)
        h2 = (x1 * lax.rsqrt(ms + EPS) * g_ref[...] * (1.0 + m[4:5]) + m[3:4]).astype(BF16)
        h2_ref[rows, 0:D] = h2
        logit_rows.append(jnp.dot(h2, wr_ref[...], preferred_element_type=F32) + br_ref[...])
    lt = jnp.concatenate(logit_rows, axis=0).T
    r = [lt[i:i + 1, :] for i in range(N_GROUPS + N_GROUPS * EPG)]

    gl = r[0:N_GROUPS]
    gmax = jnp.maximum(jnp.maximum(gl[0], gl[1]), jnp.maximum(gl[2], gl[3]))
    gidx = jnp.where(gl[0] == gmax, 0, jnp.where(gl[1] == gmax, 1, jnp.where(gl[2] == gmax, 2, 3)))
    gz = (jnp.exp(gl[0] - gmax) + jnp.exp(gl[1] - gmax)
          + jnp.exp(gl[2] - gmax) + jnp.exp(gl[3] - gmax))
    grp_p = 1.0 / gz

    el = []
    for i in range(EPG):
        e = r[N_GROUPS + 3 * EPG + i]
        for gg in (2, 1, 0):
            e = jnp.where(gidx == gg, r[N_GROUPS + gg * EPG + i], e)
        el.append(e)
    emax = jnp.maximum(jnp.maximum(el[0], el[1]), jnp.maximum(el[2], el[3]))
    ex = [jnp.exp(e - emax) for e in el]
    ez = ex[0] + ex[1] + ex[2] + ex[3]
    pr = [e / ez for e in ex]
    p1 = jnp.maximum(jnp.maximum(pr[0], pr[1]), jnp.maximum(pr[2], pr[3]))
    a = jnp.where(pr[0] == p1, 0, jnp.where(pr[1] == p1, 1, jnp.where(pr[2] == p1, 2, 3)))
    rest = [jnp.where(a == i, -1.0, pr[i]) for i in range(EPG)]
    p2 = jnp.maximum(jnp.maximum(rest[0], rest[1]), jnp.maximum(rest[2], rest[3]))
    b = jnp.where(rest[0] == p2, 0, jnp.where(rest[1] == p2, 1, jnp.where(rest[2] == p2, 2, 3)))
    psum = p1 + p2
    wa = grp_p * (p1 / psum)
    wb = grp_p * (p2 / psum)
    lo = jnp.minimum(a, b)
    hi = jnp.maximum(a, b)
    w_lo = jnp.where(a < b, wa, wb)
    w_hi = jnp.where(a < b, wb, wa)
    pair = jnp.where(lo == 0, hi - 1, jnp.where(lo == 1, hi + 1, 5))
    bucket = gidx * 6 + pair

    rid = lax.broadcasted_iota(I32, (AUG, tm), 0)
    aug = jnp.zeros((AUG, tm), F32)
    for k, piece in enumerate(_bf16_pieces(w_lo) + _bf16_pieces(w_hi)):
        aug = jnp.where(rid == k, piece, aug)
    h2_ref[:, D:D + AUG] = aug.T.astype(BF16)

    oh = (lax.broadcasted_iota(I32, (BUCKET_ROWS, tm), 0) == bucket)
    oh_f = jnp.where(oh, 1.0, 0.0)
    before = jnp.dot(oh_f.astype(BF16), tri_ref[...], preferred_element_type=F32)
    rank = jnp.sum(oh_f * before, axis=0, keepdims=True)
    cnt_ref[0] = jnp.broadcast_to(jnp.sum(oh_f, axis=1, keepdims=True), (BUCKET_ROWS, LANES))
    bucket_ref[0] = bucket
    rank_ref[0] = rank.astype(I32)


def _out_proj(o_diff, o_swa, x2, mod3, g_ffn, w_out_bf, w_r, b_r, tri, seq):
    t = x2.shape[0]
    tm = TM_PROJ
    per_b = seq // tm
    row = lambda i: (i, 0)
    const = lambda i: (0, 0)
    return pl.pallas_call(
        functools.partial(_outproj_kernel, n_steps=t // tm),
        grid=(t // tm,),
        in_specs=[pl.BlockSpec((tm, 512), row), pl.BlockSpec((tm, 512), row),
                  pl.BlockSpec(memory_space=pl.ANY),
                  pl.BlockSpec((1, 6, D), lambda i: (i // per_b, 0, 0)),
                  pl.BlockSpec((1, D), const),
                  pl.BlockSpec((D, D), const),
                  pl.BlockSpec((D, LANES), const),
                  pl.BlockSpec((1, LANES), const),
                  pl.BlockSpec((tm, tm), const)],
        out_specs=[pl.BlockSpec((tm, D), row),
                   pl.BlockSpec((tm, D + AUG), row),
                   pl.BlockSpec((1, 1, tm), lambda i: (i, 0, 0)),
                   pl.BlockSpec((1, 1, tm), lambda i: (i, 0, 0)),
                   pl.BlockSpec((1, BUCKET_ROWS, LANES), lambda i: (i, 0, 0))],
        out_shape=[jax.ShapeDtypeStruct((t, D), F32),
                   jax.ShapeDtypeStruct((t, D + AUG), BF16),
                   jax.ShapeDtypeStruct((t // tm, 1, tm), I32),
                   jax.ShapeDtypeStruct((t // tm, 1, tm), I32),
                   jax.ShapeDtypeStruct((t // tm, BUCKET_ROWS, LANES), F32)],
        scratch_shapes=[pltpu.VMEM((RING, tm, D), F32), pltpu.SemaphoreType.DMA((RING,))],
        compiler_params=_cparams(("arbitrary",)),
        name="out_proj",
    )(o_diff, o_swa, x2, mod3, g_ffn, w_out_bf, w_r, b_r, tri)


META_LOCAL, META_GLOBAL, META_ROWS, META_TILE_ROWS = 0, 1, 2, 3


def _local_pos(meta_ref, tile, bucket, rank):
    pos = rank
    for b in range(N_BUCKETS):
        pos = pos + jnp.where(bucket == b, meta_ref[META_LOCAL, tile * N_BUCKETS + b], 0)
    return pos


def _for_each_run(meta_ref, tile, fn):
    for b in range(N_BUCKETS):
        rows = meta_ref[META_ROWS, tile * N_BUCKETS + b]

        @pl.when(rows > 0)
        def _(b=b, rows=rows):
            fn(pl.multiple_of(meta_ref[META_LOCAL, tile * N_BUCKETS + b], SUBLANES),
               pl.multiple_of(meta_ref[META_GLOBAL, tile * N_BUCKETS + b], SUBLANES),
               pl.multiple_of(rows, SUBLANES))


def _dispatch_kernel(meta_ref, ztail_ref, nvalid_ref, h2_hbm, bucket_ref, rank_ref, xs_hbm,
                     sort_scr, zero_scr, sems, zsem, h2_ring, h2_sems, *, n_steps):
    i = pl.program_id(0)
    n_tiles = xs_hbm.shape[0] // TM_MOE
    tm = h2_ring.shape[1]
    slot = i % 2
    h2_ref = _ring_fetch(h2_hbm, h2_ring, h2_sems, i, n_steps)

    @pl.when(i == 0)
    def _():
        zero_scr[...] = jnp.zeros_like(zero_scr)

        def ztail_copy(b):
            rows = pl.multiple_of(ztail_ref[1, b], SUBLANES)
            return pltpu.make_async_copy(
                zero_scr.at[pl.ds(0, rows)],
                xs_hbm.at[pl.ds(pl.multiple_of(ztail_ref[0, b], SUBLANES), rows)], zsem)

        def ztile_copy(tile):
            return pltpu.make_async_copy(
                zero_scr, xs_hbm.at[pl.ds(pl.multiple_of(tile * TM_MOE, TM_MOE), TM_MOE)], zsem)

        for b in range(N_BUCKETS):
            @pl.when(ztail_ref[1, b] > 0)
            def _(b=b):
                ztail_copy(b).start()
        lax.fori_loop(nvalid_ref[0], n_tiles, lambda tile, c: (ztile_copy(tile).start(), c)[1], 0)
        for b in range(N_BUCKETS):
            @pl.when(ztail_ref[1, b] > 0)
            def _(b=b):
                ztail_copy(b).wait()
        lax.fori_loop(nvalid_ref[0], n_tiles, lambda tile, c: (ztile_copy(tile).wait(), c)[1], 0)

    def run_copy(buf, local_row, global_row, rows):
        return pltpu.make_async_copy(sort_scr.at[buf, pl.ds(local_row, rows)],
                                     xs_hbm.at[pl.ds(global_row, rows)], sems.at[buf])

    def wait_runs(tile, buf):
        run_copy(buf, 0, 0, pl.multiple_of(meta_ref[META_TILE_ROWS, tile], SUBLANES)).wait()

    @pl.when(i >= 2)
    def _():
        wait_runs(i - 2, slot)

    lpos = _local_pos(meta_ref, i, bucket_ref[0], rank_ref[0])
    lane = lax.broadcasted_iota(I32, (LS_SUB, AUG), 1)
    for r0 in range(0, LS, LS_SUB):
        row_id = lax.broadcasted_iota(I32, (LS_SUB, tm), 0) + r0
        onehot = jnp.where(row_id == lpos, 1.0, 0.0).astype(BF16)
        srt = jnp.dot(onehot, h2_ref[...], preferred_element_type=F32)
        aug = srt[:, D:D + AUG]
        w_lo = aug[:, 0:1] + aug[:, 1:2] + aug[:, 2:3]
        w_hi = aug[:, 3:4] + aug[:, 4:5] + aug[:, 5:6]
        sort_scr[slot, r0:r0 + LS_SUB, 0:D] = srt[:, 0:D]
        sort_scr[slot, r0:r0 + LS_SUB, D:XW] = jnp.where(lane == 0, w_lo, jnp.where(lane == 1, w_hi, 0.0))
    _for_each_run(meta_ref, i, lambda lr, gr, rows: run_copy(slot, lr, gr, rows).start())

    @pl.when(i == n_steps - 1)
    def _():
        @pl.when(i >= 1)
        def _():
            wait_runs(i - 1, 1 - slot)
        wait_runs(i, slot)


def _dispatch(meta, ztail, n_valid, h2a, bucket, rank, n_rows):
    t, width = h2a.shape
    tm = TM_PROJ
    grid_spec = pltpu.PrefetchScalarGridSpec(
        num_scalar_prefetch=3,
        grid=(t // tm,),
        in_specs=[pl.BlockSpec(memory_space=pl.ANY),
                  pl.BlockSpec((1, 1, tm), lambda i, m, z, nv: (i, 0, 0)),
                  pl.BlockSpec((1, 1, tm), lambda i, m, z, nv: (i, 0, 0))],
        out_specs=pl.BlockSpec(memory_space=pl.ANY),
        scratch_shapes=[pltpu.VMEM((2, LS, XW), F32), pltpu.VMEM((TM_MOE, XW), F32),
                        pltpu.SemaphoreType.DMA((2,)), pltpu.SemaphoreType.DMA(()),
                        pltpu.VMEM((RING, tm, width), BF16), pltpu.SemaphoreType.DMA((RING,))],
    )
    return pl.pallas_call(
        functools.partial(_dispatch_kernel, n_steps=t // tm),
        grid_spec=grid_spec,
        out_shape=jax.ShapeDtypeStruct((n_rows, XW), F32),
        compiler_params=_cparams(("arbitrary",)),
        name="dispatch",
    )(meta, ztail, n_valid, h2a, bucket, rank)


def _moe_kernel(lo_ref, hi_ref, nvalid_ref, xs_ref, wg_lo, wu_lo, wd_lo, wg_hi, wu_hi, wd_hi, y_ref):
    del lo_ref, hi_ref
    t = pl.program_id(0)

    @pl.when(t < nvalid_ref[0])
    def _():
        xb = xs_ref[:, 0:D].astype(BF16)
        y = None
        for k, (wg, wu, wd) in enumerate(((wg_lo, wu_lo, wd_lo), (wg_hi, wu_hi, wd_hi))):
            gt = jnp.dot(xb, wg[0], preferred_element_type=F32)
            up = jnp.dot(xb, wu[0], preferred_element_type=F32)
            act = (gt * (1.0 / (1.0 + jnp.exp(-gt))) * up).astype(BF16)
            dn = jnp.dot(act, wd[0], preferred_element_type=F32)
            term = xs_ref[:, D + k:D + k + 1] * dn
            y = term if y is None else y + term
        y_ref[...] = y

    @pl.when(t >= nvalid_ref[0])
    def _():
        y_ref[...] = jnp.zeros_like(y_ref)


def _moe(tile_lo, tile_hi, n_valid, xs, wg, wu, wd):
    n_rows, width = xs.shape
    n_tiles = n_rows // TM_MOE
    up_lo = lambda t, lo, hi, nv: (lo[t], 0, 0)
    up_hi = lambda t, lo, hi, nv: (hi[t], 0, 0)
    grid_spec = pltpu.PrefetchScalarGridSpec(
        num_scalar_prefetch=3,
        grid=(n_tiles,),
        in_specs=[pl.BlockSpec((TM_MOE, width), lambda t, lo, hi, nv: (jnp.minimum(t, nv[0] - 1), 0)),
                  pl.BlockSpec((1, D, D_EXPERT), up_lo), pl.BlockSpec((1, D, D_EXPERT), up_lo),
                  pl.BlockSpec((1, D_EXPERT, D), up_lo),
                  pl.BlockSpec((1, D, D_EXPERT), up_hi), pl.BlockSpec((1, D, D_EXPERT), up_hi),
                  pl.BlockSpec((1, D_EXPERT, D), up_hi)],
        out_specs=pl.BlockSpec((TM_MOE, D), lambda t, lo, hi, nv: (t, 0)),
    )
    return pl.pallas_call(
        _moe_kernel,
        grid_spec=grid_spec,
        out_shape=jax.ShapeDtypeStruct((n_rows, D), F32),
        compiler_params=_cparams(("arbitrary",)),
        name="moe",
    )(tile_lo, tile_hi, n_valid, xs, wg, wu, wd, wg, wu, wd)


def _final_kernel(meta_ref, ys_hbm, x1_hbm, mod_ref, g_ref, bucket_ref, rank_ref, o_ref, y_scr, sems,
                  x1_ring, x1_sems, *, n_steps):
    i = pl.program_id(0)
    tm = o_ref.shape[0]
    slot = i % 2
    x1_ref = _ring_fetch(x1_hbm, x1_ring, x1_sems, i, n_steps)

    def run_copy(buf, local_row, global_row, rows):
        return pltpu.make_async_copy(ys_hbm.at[pl.ds(global_row, rows)],
                                     y_scr.at[buf, pl.ds(local_row, rows)], sems.at[buf])

    def fetch_runs(tile, buf):
        _for_each_run(meta_ref, tile, lambda lr, gr, rows: run_copy(buf, lr, gr, rows).start())

    @pl.when(i == 0)
    def _():
        y_scr[...] = jnp.zeros_like(y_scr)
        fetch_runs(0, 0)

    @pl.when(i + 1 < n_steps)
    def _():
        fetch_runs(i + 1, 1 - slot)

    run_copy(slot, 0, 0, pl.multiple_of(meta_ref[META_TILE_ROWS, i], SUBLANES)).wait()

    lpos = _local_pos(meta_ref, i, bucket_ref[0], rank_ref[0])
    lpos_col = jnp.broadcast_to(lpos.astype(F32), (LANES, tm)).T[:, 0:1].astype(I32)
    ysrt = y_scr[slot].astype(BF16)
    m = mod_ref[0]
    for r0 in range(0, tm, TM_OUT_SUB):
        rows = slice(r0, r0 + TM_OUT_SUB)
        onehot = jnp.where(lax.broadcasted_iota(I32, (TM_OUT_SUB, LS), 1) == lpos_col[rows], 1.0, 0.0)
        y = jnp.dot(onehot.astype(BF16), ysrt, preferred_element_type=F32)
        x2 = x1_ref[rows, :] + m[5:6] * y
        ms = jnp.mean(x2 * x2, axis=-1, keepdims=True)
        o_ref[rows, :] = x2 * lax.rsqrt(ms + EPS) * g_ref[...]


def _final(meta, ys, x1, mod3, g_final, bucket, rank, seq):
    t = x1.shape[0]
    tm = TM_PROJ
    per_b = seq // tm
    grid_spec = pltpu.PrefetchScalarGridSpec(
        num_scalar_prefetch=1,
        grid=(t // tm,),
        in_specs=[pl.BlockSpec(memory_space=pl.ANY),
                  pl.BlockSpec(memory_space=pl.ANY),
                  pl.BlockSpec((1, 6, D), lambda i, m: (i // per_b, 0, 0)),
                  pl.BlockSpec((1, D), lambda i, m: (0, 0)),
                  pl.BlockSpec((1, 1, tm), lambda i, m: (i, 0, 0)),
                  pl.BlockSpec((1, 1, tm), lambda i, m: (i, 0, 0))],
        out_specs=pl.BlockSpec((tm, D), lambda i, m: (i, 0)),
        scratch_shapes=[pltpu.VMEM((2, LS, D), F32), pltpu.SemaphoreType.DMA((2,)),
                        pltpu.VMEM((RING, tm, D), F32), pltpu.SemaphoreType.DMA((RING,))],
    )
    return pl.pallas_call(
        functools.partial(_final_kernel, n_steps=t // tm),
        grid_spec=grid_spec,
        out_shape=jax.ShapeDtypeStruct((t, D), F32),
        compiler_params=_cparams(("arbitrary",)),
        name="final",
    )(meta, ys, x1, mod3, g_final, bucket, rank)


def _rope_tables(seq):
    inv = ROPE_THETA ** (-jnp.arange(0, HD, 2, dtype=F32) / HD)
    ang = jnp.arange(seq, dtype=F32)[:, None] * inv[None, :]
    cos, sin = jnp.cos(ang), jnp.sin(ang)
    return jnp.tile(cos, (1, 4)), jnp.concatenate([-sin, -sin, sin, sin], axis=-1)


def _rotary_column_order():
    half = HD // 2
    lane = np.arange(LANES)
    which, part, f = (lane // half) % 2, lane // HD, lane % half
    cols = []
    for base in (0, 512):
        for h in range(4):
            cols.append(base + h * LANES + which * HD + part * half + f)
    cols.append(np.arange(1024, 1536))
    for c in range(4):
        cols.append(1536 + (c + 4 * which) * HD + part * half + f)
    cols.append(2048 + which * HD + part * half + f)
    cols.append(np.arange(2176, 2304))
    return np.concatenate(cols)


def kernel(x, c, w_ada, b_ada, g_mix, w_in, diff_lambda, g_diff_sub, swa_sinks, w_out, g_ffn,
           w_route_group, b_route_group, w_route_expert, b_route_expert, w_gate, w_up, w_down,
           g_final):
    bsz, seq, _ = x.shape
    t = bsz * seq
    x2 = x.reshape(t, D)
    cos, sin = _rope_tables(seq)

    mod3 = _ada_mod(c, w_ada[0], b_ada[0].reshape(1, -1)).reshape(bsz, 6, D)

    w_in_bf = w_in[0][:, _rotary_column_order()].astype(BF16)
    dq, dk, dv, sq, sk, sv = _in_proj(x2, mod3, g_mix[0].reshape(1, D), w_in_bf, cos, sin, seq)
    o_diff = _diff_attn(dq, dk, dv, diff_lambda[0], g_diff_sub[0].reshape(1, LANES), bsz, seq)
    o_swa = _swa_attn(sq, sk, sv, swa_sinks[0].reshape(1, 8), bsz, seq)

    n_r = N_GROUPS + N_GROUPS * EPG
    w_r = jnp.concatenate([w_route_group[0], w_route_expert[0],
                           jnp.zeros((D, LANES - n_r), F32)], axis=1).astype(BF16)
    b_r = jnp.concatenate([b_route_group[0], b_route_expert[0],
                           jnp.zeros((LANES - n_r,), F32)]).reshape(1, LANES)
    tri = (jnp.arange(TM_PROJ)[:, None] < jnp.arange(TM_PROJ)[None, :]).astype(BF16)
    x1, h2a, bucket, rank, counts = _out_proj(o_diff, o_swa, x2, mod3, g_ffn[0].reshape(1, D),
                                              w_out[0].astype(BF16), w_r, b_r, tri, seq)

    n_tok_tiles = t // TM_PROJ
    n_tiles = -(-(t + n_tok_tiles * N_BUCKETS * (SUBLANES - 1)) // TM_MOE) + N_BUCKETS
    cnt = counts[:, :N_BUCKETS, 0].astype(I32)
    run_rows = (cnt + SUBLANES - 1) // SUBLANES * SUBLANES
    local_off = jnp.cumsum(run_rows, axis=1) - run_rows
    tiles_per = (jnp.sum(run_rows, axis=0) + TM_MOE - 1) // TM_MOE
    tile_end = jnp.cumsum(tiles_per)
    bucket_start = (tile_end - tiles_per) * TM_MOE
    global_off = bucket_start[None, :] + jnp.cumsum(run_rows, axis=0) - run_rows
    tile_rows = jnp.pad(jnp.sum(run_rows, axis=1), (0, n_tok_tiles * (N_BUCKETS - 1)))
    meta = jnp.stack([local_off.reshape(-1), global_off.reshape(-1), run_rows.reshape(-1), tile_rows])
    tid = jnp.arange(n_tiles, dtype=I32)
    tile_bucket = jnp.minimum(jnp.sum(tid[:, None] >= tile_end[None, :], axis=1), N_BUCKETS - 1)
    n_valid = tile_end[-1:].astype(I32)
    bucket_rows = jnp.sum(run_rows, axis=0)
    ztail = jnp.stack([bucket_start + bucket_rows, tiles_per * TM_MOE - bucket_rows]).astype(I32)
    grp, pair = tile_bucket // 6, tile_bucket % 6
    tile_lo = (grp * EPG + jnp.asarray(PAIR_LO, I32)[pair]).astype(I32)
    tile_hi = (grp * EPG + jnp.asarray(PAIR_HI, I32)[pair]).astype(I32)

    xs = _dispatch(meta, ztail, n_valid, h2a, bucket, rank, n_tiles * TM_MOE)
    ys = _moe(tile_lo, tile_hi, n_valid, xs, w_gate[0].astype(BF16), w_up[0].astype(BF16),
              w_down[0].astype(BF16))
    out = _final(meta, ys, x1, mod3, g_final.reshape(1, D), bucket, rank, seq)
    return out.reshape(bsz, seq, D)
```

```python
import functools
import math

import jax
import jax.numpy as jnp
import numpy as np
from jax import lax
from jax.experimental import pallas as pl
from jax.experimental.pallas import tpu as pltpu

F32 = jnp.float32
BF16 = jnp.bfloat16
I32 = jnp.int32

D = 1024
HD = 64
EPS = 1e-6
ROPE_THETA = 10000.0
LOG2E = math.log2(math.e)
LAMBDA_INIT = 0.8 - 0.6 * math.exp(-0.3 * 0)
N_GROUPS = 4
EPG = 4
N_BUCKETS = 24
BUCKET_ROWS = 32
D_EXPERT = 512
AUG = 128

LANES = 128
VMEM_LIMIT = 48 * 1024 * 1024

TM_IN = 1024
TM_IN_SUB = 256
TM_PROJ = 512
TM_OUT_SUB = 256
TQ = 512
BLK = 128
TM_MOE = 512
SUBLANES = 8
RUN_ROWS = TM_PROJ + N_BUCKETS * (SUBLANES - 1)
LS = -(-RUN_ROWS // 64) * 64
LS_SUB = LS // 2
XW = D + AUG

PAIR_LO = (0, 0, 0, 1, 1, 2)
PAIR_HI = (1, 2, 3, 2, 3, 3)


def _cparams(sem):
    return pltpu.CompilerParams(dimension_semantics=sem, vmem_limit_bytes=VMEM_LIMIT)


RING = 3


def _ring_fetch(src_hbm, ring, sems, step, n_steps):
    rows = ring.shape[1]

    def copy(s):
        return pltpu.make_async_copy(src_hbm.at[pl.ds(pl.multiple_of(s * rows, rows), rows)],
                                     ring.at[s % RING], sems.at[s % RING])

    @pl.when(step == 0)
    def _():
        for s in range(min(RING - 1, n_steps)):
            copy(s).start()

    @pl.when(step + RING - 1 < n_steps)
    def _():
        copy(step + RING - 1).start()

    copy(step).wait()
    return ring.at[step % RING]


def _ada_kernel(c_ref, w_ref, b_ref, o_ref):
    c = c_ref[...]
    ca = c * (1.0 / (1.0 + jnp.exp(-c)))
    o_ref[...] = jnp.dot(ca.astype(BF16), w_ref[...].astype(BF16),
                         preferred_element_type=F32) + b_ref[...]


def _ada_mod(c, w, b):
    bsz = c.shape[0]
    n = w.shape[1]
    tn = 512
    return pl.pallas_call(
        _ada_kernel,
        grid=(n // tn,),
        in_specs=[pl.BlockSpec((bsz, D), lambda j: (0, 0)),
                  pl.BlockSpec((D, tn), lambda j: (0, j)),
                  pl.BlockSpec((1, tn), lambda j: (0, j))],
        out_specs=pl.BlockSpec((bsz, tn), lambda j: (0, j)),
        out_shape=jax.ShapeDtypeStruct((bsz, n), F32),
        compiler_params=_cparams(("arbitrary",)),
        name="ada_mod",
    )(c, w, b)


def _rope(a, cos, sin):
    return a * cos + pltpu.roll(a, HD, 1) * sin


def _inproj_kernel(x_ref, mod_ref, g_ref, w_ref, cos_ref, sin_ref,
                   dq_ref, dk_ref, dv_ref, sq_ref, sk_ref, sv_ref, h_scr):
    tm = x_ref.shape[0]
    sub = TM_IN_SUB
    m = mod_ref[0]
    lane = lax.broadcasted_iota(I32, (sub, LANES), 1)
    head_a = (lane & 32) == 0
    low64 = lane < HD
    scale = (HD ** -0.5) * LOG2E

    def halves(a):
        return a[:, :LANES], a[:, LANES:]

    for r0 in range(0, tm, sub):
        rows = slice(r0, r0 + sub)
        x = x_ref[rows, :]
        ms = jnp.mean(x * x, axis=-1, keepdims=True)
        y = x * lax.rsqrt(ms + EPS) * g_ref[...]
        h_scr[rows, :] = (y * (1.0 + m[1:2]) + m[0:1]).astype(BF16)
        cos = cos_ref[rows, :]
        sin = sin_ref[rows, :]

        def chunk(j):
            return jnp.dot(h_scr[rows, :], w_ref[:, 256 * j:256 * (j + 1)], preferred_element_type=F32)

        for j in range(2):
            for t, r in enumerate(halves(chunk(j))):
                c0 = 256 * j + LANES * t
                dq_ref[rows, c0:c0 + LANES] = (_rope(r, cos, sin) * scale).astype(BF16)
        for j in range(2):
            for t, r in enumerate(halves(chunk(2 + j))):
                c0 = 256 * j + LANES * t
                dk_ref[rows, c0:c0 + LANES] = _rope(r, cos, sin).astype(BF16)
        for j in range(2):
            dv_ref[rows, 256 * j:256 * (j + 1)] = chunk(4 + j).astype(BF16)
        for j in range(2):
            for t, r in enumerate(halves(chunk(6 + j))):
                rp = _rope(r, cos, sin) * scale
                c = 2 * j + t
                sq_ref[rows, LANES * c:LANES * (c + 1)] = jnp.where(head_a, rp, 0.0).astype(BF16)
                sq_ref[rows, LANES * (c + 4):LANES * (c + 5)] = jnp.where(head_a, 0.0, rp).astype(BF16)
        kv = chunk(8)
        kk, vv = halves(kv)
        kr = _rope(kk, cos, sin)
        sk_ref[rows, :LANES] = jnp.where(head_a, kr, 0.0).astype(BF16)
        sk_ref[rows, LANES:] = jnp.where(head_a, 0.0, kr).astype(BF16)
        sv_ref[rows, :LANES] = jnp.where(low64, vv, 0.0).astype(BF16)
        sv_ref[rows, LANES:] = jnp.where(low64, pltpu.roll(vv, HD, 1), 0.0).astype(BF16)


def _in_proj(x2, mod3, g_mix, w_in_bf, cos, sin, seq):
    t = x2.shape[0]
    tm = TM_IN
    per_b = seq // tm
    n_in = w_in_bf.shape[1]
    row = lambda i: (i, 0)
    return pl.pallas_call(
        _inproj_kernel,
        grid=(t // tm,),
        in_specs=[pl.BlockSpec((tm, D), row),
                  pl.BlockSpec((1, 6, D), lambda i: (i // per_b, 0, 0)),
                  pl.BlockSpec((1, D), lambda i: (0, 0)),
                  pl.BlockSpec((D, n_in), lambda i: (0, 0)),
                  pl.BlockSpec((tm, LANES), lambda i: (i % per_b, 0)),
                  pl.BlockSpec((tm, LANES), lambda i: (i % per_b, 0))],
        out_specs=[pl.BlockSpec((tm, 512), row), pl.BlockSpec((tm, 512), row),
                   pl.BlockSpec((tm, 512), row), pl.BlockSpec((tm, 1024), row),
                   pl.BlockSpec((tm, 256), row), pl.BlockSpec((tm, 256), row)],
        out_shape=[jax.ShapeDtypeStruct((t, 512), BF16), jax.ShapeDtypeStruct((t, 512), BF16),
                   jax.ShapeDtypeStruct((t, 512), BF16), jax.ShapeDtypeStruct((t, 1024), BF16),
                   jax.ShapeDtypeStruct((t, 256), BF16), jax.ShapeDtypeStruct((t, 256), BF16)],
        scratch_shapes=[pltpu.VMEM((tm, D), BF16)],
        compiler_params=_cparams(("arbitrary",)),
        name="in_proj",
    )(x2, mod3, g_mix, w_in_bf, cos, sin)


_NT_DIMS = (((1,), (1,)), ((), ()))


def _diff_kernel(lam_ref, g_ref, q_ref, k_ref, v_ref, *rest):
    n_cast = (len(rest) - 1) // 2
    o_ref = rest[n_cast]
    for src, dst in zip(rest[:n_cast], rest[n_cast + 1:]):
        dst[...] = src[...].astype(BF16)

    seq = q_ref.shape[0]
    tq = TQ
    lp = lam_ref[...]
    lam = (jnp.exp(jnp.sum(lp[0:1] * lp[1:2], axis=-1, keepdims=True))
           - jnp.exp(jnp.sum(lp[2:3] * lp[3:4], axis=-1, keepdims=True)) + LAMBDA_INIT)
    lane = lax.broadcasted_iota(I32, (tq, LANES), 1)
    map_a = (lane & 32) == 0
    causal = (lax.broadcasted_iota(I32, (tq, tq), 1) <= lax.broadcasted_iota(I32, (tq, tq), 0))
    gain = g_ref[...] * (1.0 - LAMBDA_INIT)

    for i in reversed(range(seq // tq)):
        lo, hi = i * tq, (i + 1) * tq
        q = q_ref[lo:hi, :]
        zero = jnp.zeros_like(q)
        probs = []
        for qm in (jnp.where(map_a, q, zero), jnp.where(map_a, zero, q)):
            s_dg = lax.dot_general(qm, k_ref[lo:hi, :], _NT_DIMS, preferred_element_type=F32)
            s_dg = jnp.where(causal, s_dg, -jnp.inf)
            mx = jnp.max(s_dg, axis=-1, keepdims=True)
            if i > 0:
                s_off = lax.dot_general(qm, k_ref[0:lo, :], _NT_DIMS, preferred_element_type=F32)
                mx = jnp.maximum(mx, jnp.max(s_off, axis=-1, keepdims=True))
                p_off = jnp.exp2(s_off - mx)
            p_dg = jnp.exp2(s_dg - mx)
            l = jnp.sum(p_dg, axis=-1, keepdims=True)
            if i > 0:
                l = l + jnp.sum(p_off, axis=-1, keepdims=True)
                probs.append((p_off, p_dg, l))
            else:
                probs.append((None, p_dg, l))
        (p0_off, p0_dg, l0), (p1_off, p1_dg, l1) = probs
        c = lam * l0 / l1
        o = jnp.dot((p0_dg - c * p1_dg).astype(BF16), v_ref[lo:hi, :], preferred_element_type=F32)
        if i > 0:
            o = o + jnp.dot((p0_off - c * p1_off).astype(BF16), v_ref[0:lo, :],
                            preferred_element_type=F32)
        o = o / l0
        ms = jnp.mean(o * o, axis=-1, keepdims=True)
        o_ref[lo:hi, :] = (o * lax.rsqrt(ms + EPS) * gain).astype(BF16)


def _diff_attn(dq, dk, dv, lam_p, g_sub, bsz, seq, to_bf16):
    t = dq.shape[0]
    heads = 4
    blk = lambda b, h: (b, h)
    slab = lambda b, h: (b * heads + h, 0)
    slab_specs = [pl.BlockSpec((w.shape[0] // (bsz * heads), w.shape[1]), slab) for w in to_bf16]
    return pl.pallas_call(
        _diff_kernel,
        grid=(bsz, heads),
        in_specs=[pl.BlockSpec((4, HD), lambda b, h: (0, 0)),
                  pl.BlockSpec((1, LANES), lambda b, h: (0, 0)),
                  pl.BlockSpec((seq, LANES), blk),
                  pl.BlockSpec((seq, LANES), blk),
                  pl.BlockSpec((seq, LANES), blk)] + slab_specs,
        out_specs=[pl.BlockSpec((seq, LANES), blk)] + slab_specs,
        out_shape=[jax.ShapeDtypeStruct((t, 512), BF16)]
        + [jax.ShapeDtypeStruct(w.shape, BF16) for w in to_bf16],
        compiler_params=_cparams(("arbitrary", "arbitrary")),
        name="diff_attn",
    )(lam_p, g_sub, dq, dk, dv, *to_bf16)


def _swa_kernel(sink_ref, q_ref, k_ref, v_ref, o_ref):
    seq = q_ref.shape[0]
    g = pl.program_id(1)
    rows = 4 * BLK
    hsel = jnp.right_shift(lax.broadcasted_iota(I32, (rows, 8), 0), 7) + 4 * g
    sink_col = jnp.sum(jnp.where(lax.broadcasted_iota(I32, (rows, 8), 1) == hsel,
                                 sink_ref[...] * LOG2E, 0.0), axis=-1, keepdims=True)
    qi = lax.broadcasted_iota(I32, (rows, 2 * BLK), 0) & (BLK - 1)
    kj = lax.broadcasted_iota(I32, (rows, 2 * BLK), 1)
    dist = qi + BLK - kj
    band = (dist >= 0) & (dist < BLK)
    first = (lax.broadcasted_iota(I32, (rows, BLK), 1)
             <= (lax.broadcasted_iota(I32, (rows, BLK), 0) & (BLK - 1)))
    low64 = lax.broadcasted_iota(I32, (BLK, LANES), 1) < HD

    def attend(q_rows, k_rows, mask):
        qs = jnp.concatenate([q_ref[q_rows, LANES * j:LANES * (j + 1)] for j in range(4)], axis=0)
        s = lax.dot_general(qs, k_ref[k_rows, :], _NT_DIMS, preferred_element_type=F32)
        s = jnp.where(mask, s, -jnp.inf)
        mx = jnp.maximum(jnp.max(s, axis=-1, keepdims=True), sink_col)
        p = jnp.exp2(s - mx)
        l = jnp.sum(p, axis=-1, keepdims=True) + jnp.exp2(sink_col - mx)
        o = jnp.dot(p.astype(BF16), v_ref[k_rows, :], preferred_element_type=F32) / l
        for c in range(2):
            even = o[(2 * c) * BLK:(2 * c + 1) * BLK, :]
            odd = o[(2 * c + 1) * BLK:(2 * c + 2) * BLK, :]
            o_ref[q_rows, LANES * c:LANES * (c + 1)] = jnp.where(
                low64, even, pltpu.roll(odd, HD, 1)).astype(BF16)

    attend(pl.ds(0, BLK), pl.ds(0, BLK), first)

    for n in range(1, seq // BLK):
        attend(pl.ds(n * BLK, BLK), pl.ds((n - 1) * BLK, 2 * BLK), band)


def _swa_attn(sq, sk, sv, sinks, bsz, seq):
    t = sq.shape[0]
    return pl.pallas_call(
        _swa_kernel,
        grid=(bsz, 2),
        in_specs=[pl.BlockSpec((1, 8), lambda b, g: (0, 0)),
                  pl.BlockSpec((seq, 512), lambda b, g: (b, g)),
                  pl.BlockSpec((seq, LANES), lambda b, g: (b, g)),
                  pl.BlockSpec((seq, LANES), lambda b, g: (b, g))],
        out_specs=pl.BlockSpec((seq, 256), lambda b, g: (b, g)),
        out_shape=jax.ShapeDtypeStruct((t, 512), BF16),
        compiler_params=_cparams(("arbitrary", "arbitrary")),
        name="swa_attn",
    )(sinks, sq, sk, sv)


def _bf16_pieces(w):
    p0 = w.astype(BF16).astype(F32)
    r1 = w - p0
    p1 = r1.astype(BF16).astype(F32)
    return p0, p1, r1 - p1


def _outproj_kernel(od_ref, os_ref, x_hbm, mod_ref, g_ref, wo_ref, wr_ref, br_ref, tri_ref,
                    x1_ref, h2_ref, bucket_ref, rank_ref, cnt_ref, x_ring, x_sems, *, n_steps):
    tm = x1_ref.shape[0]
    x_ref = _ring_fetch(x_hbm, x_ring, x_sems, pl.program_id(0), n_steps)
    m = mod_ref[0]
    logit_rows = []
    for r0 in range(0, tm, TM_OUT_SUB):
        rows = slice(r0, r0 + TM_OUT_SUB)
        mix = (jnp.dot(od_ref[rows, :], wo_ref[0:512, :], preferred_element_type=F32)
               + jnp.dot(os_ref[rows, :], wo_ref[512:1024, :], preferred_element_type=F32))
        x1 = x_ref[rows, :] + m[2:3] * mix
        x1_ref[rows, :] = x1
        ms = jnp.mean(x1 * x1, axis=-1, keepdims=True)
        h2 = (x1 * lax.rsqrt(ms + EPS) * g_ref[...] * (1.0 + m[4:5]) + m[3:4]).astype(BF16)
        h2_ref[rows, 0:D] = h2
        logit_rows.append(jnp.dot(h2, wr_ref[...], preferred_element_type=F32) + br_ref[...])
    lt = jnp.concatenate(logit_rows, axis=0).T
    r = [lt[i:i + 1, :] for i in range(N_GROUPS + N_GROUPS * EPG)]

    gl = r[0:N_GROUPS]
    gmax = jnp.maximum(jnp.maximum(gl[0], gl[1]), jnp.maximum(gl[2], gl[3]))
    gidx = jnp.where(gl[0] == gmax, 0, jnp.where(gl[1] == gmax, 1, jnp.where(gl[2] == gmax, 2, 3)))
    gz = (jnp.exp(gl[0] - gmax) + jnp.exp(gl[1] - gmax)
          + jnp.exp(gl[2] - gmax) + jnp.exp(gl[3] - gmax))
    grp_p = 1.0 / gz

    el = []
    for i in range(EPG):
        e = r[N_GROUPS + 3 * EPG + i]
        for gg in (2, 1, 0):
            e = jnp.where(gidx == gg, r[N_GROUPS + gg * EPG + i], e)
        el.append(e)
    emax = jnp.maximum(jnp.maximum(el[0], el[1]), jnp.maximum(el[2], el[3]))
    ex = [jnp.exp(e - emax) for e in el]
    ez = ex[0] + ex[1] + ex[2] + ex[3]
    pr = [e / ez for e in ex]
    p1 = jnp.maximum(jnp.maximum(pr[0], pr[1]), jnp.maximum(pr[2], pr[3]))
    a = jnp.where(pr[0] == p1, 0, jnp.where(pr[1] == p1, 1, jnp.where(pr[2] == p1, 2, 3)))
    rest = [jnp.where(a == i, -1.0, pr[i]) for i in range(EPG)]
    p2 = jnp.maximum(jnp.maximum(rest[0], rest[1]), jnp.maximum(rest[2], rest[3]))
    b = jnp.where(rest[0] == p2, 0, jnp.where(rest[1] == p2, 1, jnp.where(rest[2] == p2, 2, 3)))
    psum = p1 + p2
    wa = grp_p * (p1 / psum)
    wb = grp_p * (p2 / psum)
    lo = jnp.minimum(a, b)
    hi = jnp.maximum(a, b)
    w_lo = jnp.where(a < b, wa, wb)
    w_hi = jnp.where(a < b, wb, wa)
    pair = jnp.where(lo == 0, hi - 1, jnp.where(lo == 1, hi + 1, 5))
    bucket = gidx * 6 + pair

    rid = lax.broadcasted_iota(I32, (AUG, tm), 0)
    aug = jnp.zeros((AUG, tm), F32)
    for k, piece in enumerate(_bf16_pieces(w_lo) + _bf16_pieces(w_hi)):
        aug = jnp.where(rid == k, piece, aug)
    h2_ref[:, D:D + AUG] = aug.T.astype(BF16)

    oh = (lax.broadcasted_iota(I32, (BUCKET_ROWS, tm), 0) == bucket)
    oh_f = jnp.where(oh, 1.0, 0.0)
    before = jnp.dot(oh_f.astype(BF16), tri_ref[...], preferred_element_type=F32)
    rank = jnp.sum(oh_f * before, axis=0, keepdims=True)
    cnt_ref[0] = jnp.broadcast_to(jnp.sum(oh_f, axis=1, keepdims=True), (BUCKET_ROWS, LANES))
    bucket_ref[0] = bucket
    rank_ref[0] = rank.astype(I32)


def _out_proj(o_diff, o_swa, x2, mod3, g_ffn, w_out_bf, w_r, b_r, tri, seq):
    t = x2.shape[0]
    tm = TM_PROJ
    per_b = seq // tm
    row = lambda i: (i, 0)
    const = lambda i: (0, 0)
    return pl.pallas_call(
        functools.partial(_outproj_kernel, n_steps=t // tm),
        grid=(t // tm,),
        in_specs=[pl.BlockSpec((tm, 512), row), pl.BlockSpec((tm, 512), row),
                  pl.BlockSpec(memory_space=pl.ANY),
                  pl.BlockSpec((1, 6, D), lambda i: (i // per_b, 0, 0)),
                  pl.BlockSpec((1, D), const),
                  pl.BlockSpec((D, D), const),
                  pl.BlockSpec((D, LANES), const),
                  pl.BlockSpec((1, LANES), const),
                  pl.BlockSpec((tm, tm), const)],
        out_specs=[pl.BlockSpec((tm, D), row),
                   pl.BlockSpec((tm, D + AUG), row),
                   pl.BlockSpec((1, 1, tm), lambda i: (i, 0, 0)),
                   pl.BlockSpec((1, 1, tm), lambda i: (i, 0, 0)),
                   pl.BlockSpec((1, BUCKET_ROWS, LANES), lambda i: (i, 0, 0))],
        out_shape=[jax.ShapeDtypeStruct((t, D), F32),
                   jax.ShapeDtypeStruct((t, D + AUG), BF16),
                   jax.ShapeDtypeStruct((t // tm, 1, tm), I32),
                   jax.ShapeDtypeStruct((t // tm, 1, tm), I32),
                   jax.ShapeDtypeStruct((t // tm, BUCKET_ROWS, LANES), F32)],
        scratch_shapes=[pltpu.VMEM((RING, tm, D), F32), pltpu.SemaphoreType.DMA((RING,))],
        compiler_params=_cparams(("arbitrary",)),
        name="out_proj",
    )(o_diff, o_swa, x2, mod3, g_ffn, w_out_bf, w_r, b_r, tri)


META_LOCAL, META_GLOBAL, META_ROWS, META_TILE_ROWS = 0, 1, 2, 3


def _local_pos(meta_ref, tile, bucket, rank):
    pos = rank
    for b in range(N_BUCKETS):
        pos = pos + jnp.where(bucket == b, meta_ref[META_LOCAL, tile * N_BUCKETS + b], 0)
    return pos


def _for_each_run(meta_ref, tile, fn):
    for b in range(N_BUCKETS):
        rows = meta_ref[META_ROWS, tile * N_BUCKETS + b]

        @pl.when(rows > 0)
        def _(b=b, rows=rows):
            fn(pl.multiple_of(meta_ref[META_LOCAL, tile * N_BUCKETS + b], SUBLANES),
               pl.multiple_of(meta_ref[META_GLOBAL, tile * N_BUCKETS + b], SUBLANES),
               pl.multiple_of(rows, SUBLANES))


def _dispatch_kernel(meta_ref, ztail_ref, nvalid_ref, h2_hbm, bucket_ref, rank_ref, xs_hbm,
                     sort_scr, zero_scr, sems, zsem, h2_ring, h2_sems, *, n_steps):
    i = pl.program_id(0)
    n_tiles = xs_hbm.shape[0] // TM_MOE
    tm = h2_ring.shape[1]
    slot = i % 2
    h2_ref = _ring_fetch(h2_hbm, h2_ring, h2_sems, i, n_steps)

    @pl.when(i == 0)
    def _():
        zero_scr[...] = jnp.zeros_like(zero_scr)

        def ztail_copy(b):
            rows = pl.multiple_of(ztail_ref[1, b], SUBLANES)
            return pltpu.make_async_copy(
                zero_scr.at[pl.ds(0, rows)],
                xs_hbm.at[pl.ds(pl.multiple_of(ztail_ref[0, b], SUBLANES), rows)], zsem)

        def ztile_copy(tile):
            return pltpu.make_async_copy(
                zero_scr, xs_hbm.at[pl.ds(pl.multiple_of(tile * TM_MOE, TM_MOE), TM_MOE)], zsem)

        for b in range(N_BUCKETS):
            @pl.when(ztail_ref[1, b] > 0)
            def _(b=b):
                ztail_copy(b).start()
        lax.fori_loop(nvalid_ref[0], n_tiles, lambda tile, c: (ztile_copy(tile).start(), c)[1], 0)
        for b in range(N_BUCKETS):
            @pl.when(ztail_ref[1, b] > 0)
            def _(b=b):
                ztail_copy(b).wait()
        lax.fori_loop(nvalid_ref[0], n_tiles, lambda tile, c: (ztile_copy(tile).wait(), c)[1], 0)

    def run_copy(buf, local_row, global_row, rows):
        return pltpu.make_async_copy(sort_scr.at[buf, pl.ds(local_row, rows)],
                                     xs_hbm.at[pl.ds(global_row, rows)], sems.at[buf])

    def wait_runs(tile, buf):
        run_copy(buf, 0, 0, pl.multiple_of(meta_ref[META_TILE_ROWS, tile], SUBLANES)).wait()

    @pl.when(i >= 2)
    def _():
        wait_runs(i - 2, slot)

    lpos = _local_pos(meta_ref, i, bucket_ref[0], rank_ref[0])
    lane = lax.broadcasted_iota(I32, (LS_SUB, AUG), 1)
    for r0 in range(0, LS, LS_SUB):
        row_id = lax.broadcasted_iota(I32, (LS_SUB, tm), 0) + r0
        onehot = jnp.where(row_id == lpos, 1.0, 0.0).astype(BF16)
        srt = jnp.dot(onehot, h2_ref[...], preferred_element_type=F32)
        aug = srt[:, D:D + AUG]
        w_lo = aug[:, 0:1] + aug[:, 1:2] + aug[:, 2:3]
        w_hi = aug[:, 3:4] + aug[:, 4:5] + aug[:, 5:6]
        sort_scr[slot, r0:r0 + LS_SUB, 0:D] = srt[:, 0:D]
        sort_scr[slot, r0:r0 + LS_SUB, D:XW] = jnp.where(lane == 0, w_lo, jnp.where(lane == 1, w_hi, 0.0))
    _for_each_run(meta_ref, i, lambda lr, gr, rows: run_copy(slot, lr, gr, rows).start())

    @pl.when(i == n_steps - 1)
    def _():
        @pl.when(i >= 1)
        def _():
            wait_runs(i - 1, 1 - slot)
        wait_runs(i, slot)


def _dispatch(meta, ztail, n_valid, h2a, bucket, rank, n_rows):
    t, width = h2a.shape
    tm = TM_PROJ
    grid_spec = pltpu.PrefetchScalarGridSpec(
        num_scalar_prefetch=3,
        grid=(t // tm,),
        in_specs=[pl.BlockSpec(memory_space=pl.ANY),
                  pl.BlockSpec((1, 1, tm), lambda i, m, z, nv: (i, 0, 0)),
                  pl.BlockSpec((1, 1, tm), lambda i, m, z, nv: (i, 0, 0))],
        out_specs=pl.BlockSpec(memory_space=pl.ANY),
        scratch_shapes=[pltpu.VMEM((2, LS, XW), F32), pltpu.VMEM((TM_MOE, XW), F32),
                        pltpu.SemaphoreType.DMA((2,)), pltpu.SemaphoreType.DMA(()),
                        pltpu.VMEM((RING, tm, width), BF16), pltpu.SemaphoreType.DMA((RING,))],
    )
    return pl.pallas_call(
        functools.partial(_dispatch_kernel, n_steps=t // tm),
        grid_spec=grid_spec,
        out_shape=jax.ShapeDtypeStruct((n_rows, XW), F32),
        compiler_params=_cparams(("arbitrary",)),
        name="dispatch",
    )(meta, ztail, n_valid, h2a, bucket, rank)


def _moe_kernel(lo_ref, hi_ref, nvalid_ref, xs_ref, wg_lo, wu_lo, wd_lo, wg_hi, wu_hi, wd_hi, y_ref):
    del lo_ref, hi_ref
    t = pl.program_id(0)

    @pl.when(t < nvalid_ref[0])
    def _():
        xb = xs_ref[:, 0:D].astype(BF16)
        y = None
        for k, (wg, wu, wd) in enumerate(((wg_lo, wu_lo, wd_lo), (wg_hi, wu_hi, wd_hi))):
            gt = jnp.dot(xb, wg[0], preferred_element_type=F32)
            up = jnp.dot(xb, wu[0], preferred_element_type=F32)
            act = (gt * (1.0 / (1.0 + jnp.exp(-gt))) * up).astype(BF16)
            dn = jnp.dot(act, wd[0], preferred_element_type=F32)
            term = xs_ref[:, D + k:D + k + 1] * dn
            y = term if y is None else y + term
        y_ref[...] = y

    @pl.when(t >= nvalid_ref[0])
    def _():
        y_ref[...] = jnp.zeros_like(y_ref)


def _moe(tile_lo, tile_hi, n_valid, xs, wg, wu, wd):
    n_rows, width = xs.shape
    n_tiles = n_rows // TM_MOE
    up_lo = lambda t, lo, hi, nv: (lo[t], 0, 0)
    up_hi = lambda t, lo, hi, nv: (hi[t], 0, 0)
    grid_spec = pltpu.PrefetchScalarGridSpec(
        num_scalar_prefetch=3,
        grid=(n_tiles,),
        in_specs=[pl.BlockSpec((TM_MOE, width), lambda t, lo, hi, nv: (jnp.minimum(t, nv[0] - 1), 0)),
                  pl.BlockSpec((1, D, D_EXPERT), up_lo), pl.BlockSpec((1, D, D_EXPERT), up_lo),
                  pl.BlockSpec((1, D_EXPERT, D), up_lo),
                  pl.BlockSpec((1, D, D_EXPERT), up_hi), pl.BlockSpec((1, D, D_EXPERT), up_hi),
                  pl.BlockSpec((1, D_EXPERT, D), up_hi)],
        out_specs=pl.BlockSpec((TM_MOE, D), lambda t, lo, hi, nv: (t, 0)),
    )
    return pl.pallas_call(
        _moe_kernel,
        grid_spec=grid_spec,
        out_shape=jax.ShapeDtypeStruct((n_rows, D), F32),
        compiler_params=_cparams(("arbitrary",)),
        name="moe",
    )(tile_lo, tile_hi, n_valid, xs, wg, wu, wd, wg, wu, wd)


def _final_kernel(meta_ref, ys_hbm, x1_hbm, mod_ref, g_ref, bucket_ref, rank_ref, o_ref, y_scr, sems,
                  x1_ring, x1_sems, *, n_steps):
    i = pl.program_id(0)
    tm = o_ref.shape[0]
    slot = i % 2
    x1_ref = _ring_fetch(x1_hbm, x1_ring, x1_sems, i, n_steps)

    def run_copy(buf, local_row, global_row, rows):
        return pltpu.make_async_copy(ys_hbm.at[pl.ds(global_row, rows)],
                                     y_scr.at[buf, pl.ds(local_row, rows)], sems.at[buf])

    def fetch_runs(tile, buf):
        _for_each_run(meta_ref, tile, lambda lr, gr, rows: run_copy(buf, lr, gr, rows).start())

    @pl.when(i == 0)
    def _():
        y_scr[...] = jnp.zeros_like(y_scr)
        fetch_runs(0, 0)

    @pl.when(i + 1 < n_steps)
    def _():
        fetch_runs(i + 1, 1 - slot)

    run_copy(slot, 0, 0, pl.multiple_of(meta_ref[META_TILE_ROWS, i], SUBLANES)).wait()

    lpos = _local_pos(meta_ref, i, bucket_ref[0], rank_ref[0])
    lpos_col = jnp.broadcast_to(lpos.astype(F32), (LANES, tm)).T[:, 0:1].astype(I32)
    ysrt = y_scr[slot].astype(BF16)
    m = mod_ref[0]
    for r0 in range(0, tm, TM_OUT_SUB):
        rows = slice(r0, r0 + TM_OUT_SUB)
        onehot = jnp.where(lax.broadcasted_iota(I32, (TM_OUT_SUB, LS), 1) == lpos_col[rows], 1.0, 0.0)
        y = jnp.dot(onehot.astype(BF16), ysrt, preferred_element_type=F32)
        x2 = x1_ref[rows, :] + m[5:6] * y
        ms = jnp.mean(x2 * x2, axis=-1, keepdims=True)
        o_ref[rows, :] = x2 * lax.rsqrt(ms + EPS) * g_ref[...]


def _final(meta, ys, x1, mod3, g_final, bucket, rank, seq):
    t = x1.shape[0]
    tm = TM_PROJ
    per_b = seq // tm
    grid_spec = pltpu.PrefetchScalarGridSpec(
        num_scalar_prefetch=1,
        grid=(t // tm,),
        in_specs=[pl.BlockSpec(memory_space=pl.ANY),
                  pl.BlockSpec(memory_space=pl.ANY),
                  pl.BlockSpec((1, 6, D), lambda i, m: (i // per_b, 0, 0)),
                  pl.BlockSpec((1, D), lambda i, m: (0, 0)),
                  pl.BlockSpec((1, 1, tm), lambda i, m: (i, 0, 0)),
                  pl.BlockSpec((1, 1, tm), lambda i, m: (i, 0, 0))],
        out_specs=pl.BlockSpec((tm, D), lambda i, m: (i, 0)),
        scratch_shapes=[pltpu.VMEM((2, LS, D), F32), pltpu.SemaphoreType.DMA((2,)),
                        pltpu.VMEM((RING, tm, D), F32), pltpu.SemaphoreType.DMA((RING,))],
    )
    return pl.pallas_call(
        functools.partial(_final_kernel, n_steps=t // tm),
        grid_spec=grid_spec,
        out_shape=jax.ShapeDtypeStruct((t, D), F32),
        compiler_params=_cparams(("arbitrary",)),
        name="final",
    )(meta, ys, x1, mod3, g_final, bucket, rank)


def _rope_tables(seq):
    inv = ROPE_THETA ** (-jnp.arange(0, HD, 2, dtype=F32) / HD)
    ang = jnp.arange(seq, dtype=F32)[:, None] * inv[None, :]
    cos, sin = jnp.cos(ang), jnp.sin(ang)
    return jnp.tile(cos, (1, 4)), jnp.concatenate([-sin, -sin, sin, sin], axis=-1)


def _rotary_column_order():
    half = HD // 2
    lane = np.arange(LANES)
    which, part, f = (lane // half) % 2, lane // HD, lane % half
    cols = []
    for base in (0, 512):
        for h in range(4):
            cols.append(base + h * LANES + which * HD + part * half + f)
    cols.append(np.arange(1024, 1536))
    for c in range(4):
        cols.append(1536 + (c + 4 * which) * HD + part * half + f)
    cols.append(2048 + which * HD + part * half + f)
    cols.append(np.arange(2176, 2304))
    return np.concatenate(cols)


def kernel(x, c, w_ada, b_ada, g_mix, w_in, diff_lambda, g_diff_sub, swa_sinks, w_out, g_ffn,
           w_route_group, b_route_group, w_route_expert, b_route_expert, w_gate, w_up, w_down,
           g_final):
    bsz, seq, _ = x.shape
    t = bsz * seq
    x2 = x.reshape(t, D)
    cos, sin = _rope_tables(seq)

    mod3 = _ada_mod(c, w_ada[0], b_ada[0].reshape(1, -1)).reshape(bsz, 6, D)

    w_in_bf = w_in[0][:, _rotary_column_order()].astype(BF16)
    dq, dk, dv, sq, sk, sv = _in_proj(x2, mod3, g_mix[0].reshape(1, D), w_in_bf, cos, sin, seq)
    n_exp = w_gate.shape[1]
    o_diff, wg_bf, wu_bf, wd_bf = _diff_attn(
        dq, dk, dv, diff_lambda[0], g_diff_sub[0].reshape(1, LANES), bsz, seq,
        (w_gate[0].reshape(n_exp * D, D_EXPERT), w_up[0].reshape(n_exp * D, D_EXPERT),
         w_down[0].reshape(n_exp * D_EXPERT, D)))
    o_swa = _swa_attn(sq, sk, sv, swa_sinks[0].reshape(1, 8), bsz, seq)

    n_r = N_GROUPS + N_GROUPS * EPG
    w_r = jnp.concatenate([w_route_group[0], w_route_expert[0],
                           jnp.zeros((D, LANES - n_r), F32)], axis=1).astype(BF16)
    b_r = jnp.concatenate([b_route_group[0], b_route_expert[0],
                           jnp.zeros((LANES - n_r,), F32)]).reshape(1, LANES)
    tri = (jnp.arange(TM_PROJ)[:, None] < jnp.arange(TM_PROJ)[None, :]).astype(BF16)
    x1, h2a, bucket, rank, counts = _out_proj(o_diff, o_swa, x2, mod3, g_ffn[0].reshape(1, D),
                                              w_out[0].astype(BF16), w_r, b_r, tri, seq)

    n_tok_tiles = t // TM_PROJ
    n_tiles = -(-(t + n_tok_tiles * N_BUCKETS * (SUBLANES - 1)) // TM_MOE) + N_BUCKETS
    cnt = counts[:, :N_BUCKETS, 0].astype(I32)
    run_rows = (cnt + SUBLANES - 1) // SUBLANES * SUBLANES
    local_off = jnp.cumsum(run_rows, axis=1) - run_rows
    tiles_per = (jnp.sum(run_rows, axis=0) + TM_MOE - 1) // TM_MOE
    tile_end = jnp.cumsum(tiles_per)
    bucket_start = (tile_end - tiles_per) * TM_MOE
    global_off = bucket_start[None, :] + jnp.cumsum(run_rows, axis=0) - run_rows
    tile_rows = jnp.pad(jnp.sum(run_rows, axis=1), (0, n_tok_tiles * (N_BUCKETS - 1)))
    meta = jnp.stack([local_off.reshape(-1), global_off.reshape(-1), run_rows.reshape(-1), tile_rows])
    tid = jnp.arange(n_tiles, dtype=I32)
    tile_bucket = jnp.minimum(jnp.sum(tid[:, None] >= tile_end[None, :], axis=1), N_BUCKETS - 1)
    n_valid = tile_end[-1:].astype(I32)
    bucket_rows = jnp.sum(run_rows, axis=0)
    ztail = jnp.stack([bucket_start + bucket_rows, tiles_per * TM_MOE - bucket_rows]).astype(I32)
    grp, pair = tile_bucket // 6, tile_bucket % 6
    tile_lo = (grp * EPG + jnp.asarray(PAIR_LO, I32)[pair]).astype(I32)
    tile_hi = (grp * EPG + jnp.asarray(PAIR_HI, I32)[pair]).astype(I32)

    xs = _dispatch(meta, ztail, n_valid, h2a, bucket, rank, n_tiles * TM_MOE)
    ys = _moe(tile_lo, tile_hi, n_valid, xs, wg_bf.reshape(n_exp, D, D_EXPERT),
              wu_bf.reshape(n_exp, D, D_EXPERT), wd_bf.reshape(n_exp, D_EXPERT, D))
    out = _final(meta, ys, x1, mod3, g_final.reshape(1, D), bucket, rank, seq)
    return out.reshape(bsz, seq, D)
```

```python
import functools
import math

import jax
import jax.numpy as jnp
import numpy as np
from jax import lax
from jax.experimental import pallas as pl
from jax.experimental.pallas import tpu as pltpu

F32 = jnp.float32
BF16 = jnp.bfloat16
I32 = jnp.int32

D = 1024
HD = 64
EPS = 1e-6
ROPE_THETA = 10000.0
LOG2E = math.log2(math.e)
LAMBDA_INIT = 0.8 - 0.6 * math.exp(-0.3 * 0)
N_GROUPS = 4
EPG = 4
N_BUCKETS = 24
BUCKET_ROWS = 32
D_EXPERT = 512
AUG = 128

LANES = 128
VMEM_LIMIT = 48 * 1024 * 1024

TM_IN = 1024
TM_IN_SUB = 256
TM_PROJ = 512
TM_OUT_SUB = 256
TQ = 512
BLK = 128
TM_MOE = 512
SUBLANES = 8
RUN_ROWS = TM_PROJ + N_BUCKETS * (SUBLANES - 1)
LS = -(-RUN_ROWS // 64) * 64
LS_SUB = LS // 2
XW = D + AUG

PAIR_LO = (0, 0, 0, 1, 1, 2)
PAIR_HI = (1, 2, 3, 2, 3, 3)


def _cparams(sem):
    return pltpu.CompilerParams(dimension_semantics=sem, vmem_limit_bytes=VMEM_LIMIT)


RING = 3


def _ring_fetch(src_hbm, ring, sems, step, n_steps):
    rows = ring.shape[1]

    def copy(s):
        return pltpu.make_async_copy(src_hbm.at[pl.ds(pl.multiple_of(s * rows, rows), rows)],
                                     ring.at[s % RING], sems.at[s % RING])

    @pl.when(step == 0)
    def _():
        for s in range(min(RING - 1, n_steps)):
            copy(s).start()

    @pl.when(step + RING - 1 < n_steps)
    def _():
        copy(step + RING - 1).start()

    copy(step).wait()
    return ring.at[step % RING]


def _ada_kernel(c_ref, w_ref, b_ref, o_ref):
    c = c_ref[...]
    ca = c * (1.0 / (1.0 + jnp.exp(-c)))
    o_ref[...] = jnp.dot(ca.astype(BF16), w_ref[...].astype(BF16),
                         preferred_element_type=F32) + b_ref[...]


def _ada_mod(c, w, b):
    bsz = c.shape[0]
    n = w.shape[1]
    tn = 512
    return pl.pallas_call(
        _ada_kernel,
        grid=(n // tn,),
        in_specs=[pl.BlockSpec((bsz, D), lambda j: (0, 0)),
                  pl.BlockSpec((D, tn), lambda j: (0, j)),
                  pl.BlockSpec((1, tn), lambda j: (0, j))],
        out_specs=pl.BlockSpec((bsz, tn), lambda j: (0, j)),
        out_shape=jax.ShapeDtypeStruct((bsz, n), F32),
        compiler_params=_cparams(("arbitrary",)),
        name="ada_mod",
    )(c, w, b)


def _rope(a, cos, sin):
    return a * cos + pltpu.roll(a, HD, 1) * sin


def _inproj_kernel(x_ref, mod_ref, g_ref, w_ref, cos_ref, sin_ref,
                   dq_ref, dk_ref, dv_ref, sq_ref, sk_ref, sv_ref, h_scr):
    tm = x_ref.shape[0]
    sub = TM_IN_SUB
    m = mod_ref[0]
    lane = lax.broadcasted_iota(I32, (sub, LANES), 1)
    head_a = (lane & 32) == 0
    low64 = lane < HD
    scale = (HD ** -0.5) * LOG2E

    def halves(a):
        return a[:, :LANES], a[:, LANES:]

    for r0 in range(0, tm, sub):
        rows = slice(r0, r0 + sub)
        x = x_ref[rows, :]
        ms = jnp.mean(x * x, axis=-1, keepdims=True)
        y = x * lax.rsqrt(ms + EPS) * g_ref[...]
        h_scr[rows, :] = (y * (1.0 + m[1:2]) + m[0:1]).astype(BF16)
        cos = cos_ref[rows, :]
        sin = sin_ref[rows, :]

        def chunk(j):
            return jnp.dot(h_scr[rows, :], w_ref[:, 256 * j:256 * (j + 1)], preferred_element_type=F32)

        for j in range(2):
            for t, r in enumerate(halves(chunk(j))):
                c0 = 256 * j + LANES * t
                dq_ref[rows, c0:c0 + LANES] = (_rope(r, cos, sin) * scale).astype(BF16)
        for j in range(2):
            for t, r in enumerate(halves(chunk(2 + j))):
                c0 = 256 * j + LANES * t
                dk_ref[rows, c0:c0 + LANES] = _rope(r, cos, sin).astype(BF16)
        for j in range(2):
            dv_ref[rows, 256 * j:256 * (j + 1)] = chunk(4 + j).astype(BF16)
        for j in range(2):
            for t, r in enumerate(halves(chunk(6 + j))):
                rp = _rope(r, cos, sin) * scale
                c = 2 * j + t
                sq_ref[rows, LANES * c:LANES * (c + 1)] = jnp.where(head_a, rp, 0.0).astype(BF16)
                sq_ref[rows, LANES * (c + 4):LANES * (c + 5)] = jnp.where(head_a, 0.0, rp).astype(BF16)
        kv = chunk(8)
        kk, vv = halves(kv)
        kr = _rope(kk, cos, sin)
        sk_ref[rows, :LANES] = jnp.where(head_a, kr, 0.0).astype(BF16)
        sk_ref[rows, LANES:] = jnp.where(head_a, 0.0, kr).astype(BF16)
        sv_ref[rows, :LANES] = jnp.where(low64, vv, 0.0).astype(BF16)
        sv_ref[rows, LANES:] = jnp.where(low64, pltpu.roll(vv, HD, 1), 0.0).astype(BF16)


def _in_proj(x2, mod3, g_mix, w_in_bf, cos, sin, seq):
    t = x2.shape[0]
    tm = TM_IN
    per_b = seq // tm
    n_in = w_in_bf.shape[1]
    row = lambda i: (i, 0)
    return pl.pallas_call(
        _inproj_kernel,
        grid=(t // tm,),
        in_specs=[pl.BlockSpec((tm, D), row),
                  pl.BlockSpec((1, 6, D), lambda i: (i // per_b, 0, 0)),
                  pl.BlockSpec((1, D), lambda i: (0, 0)),
                  pl.BlockSpec((D, n_in), lambda i: (0, 0)),
                  pl.BlockSpec((tm, LANES), lambda i: (i % per_b, 0)),
                  pl.BlockSpec((tm, LANES), lambda i: (i % per_b, 0))],
        out_specs=[pl.BlockSpec((tm, 512), row), pl.BlockSpec((tm, 512), row),
                   pl.BlockSpec((tm, 512), row), pl.BlockSpec((tm, 1024), row),
                   pl.BlockSpec((tm, 256), row), pl.BlockSpec((tm, 256), row)],
        out_shape=[jax.ShapeDtypeStruct((t, 512), BF16), jax.ShapeDtypeStruct((t, 512), BF16),
                   jax.ShapeDtypeStruct((t, 512), BF16), jax.ShapeDtypeStruct((t, 1024), BF16),
                   jax.ShapeDtypeStruct((t, 256), BF16), jax.ShapeDtypeStruct((t, 256), BF16)],
        scratch_shapes=[pltpu.VMEM((tm, D), BF16)],
        compiler_params=_cparams(("arbitrary",)),
        name="in_proj",
    )(x2, mod3, g_mix, w_in_bf, cos, sin)


_NT_DIMS = (((1,), (1,)), ((), ()))


def _diff_kernel(lam_ref, g_ref, q_ref, k_ref, v_ref, *rest):
    n_cast = (len(rest) - 1) // 2
    o_ref = rest[n_cast]
    for src, dst in zip(rest[:n_cast], rest[n_cast + 1:]):
        dst[...] = src[...].astype(BF16)

    seq = q_ref.shape[0]
    tq = TQ
    lp = lam_ref[...]
    lam = (jnp.exp(jnp.sum(lp[0:1] * lp[1:2], axis=-1, keepdims=True))
           - jnp.exp(jnp.sum(lp[2:3] * lp[3:4], axis=-1, keepdims=True)) + LAMBDA_INIT)
    lane = lax.broadcasted_iota(I32, (tq, LANES), 1)
    map_a = (lane & 32) == 0
    causal = (lax.broadcasted_iota(I32, (tq, tq), 1) <= lax.broadcasted_iota(I32, (tq, tq), 0))
    gain = g_ref[...] * (1.0 - LAMBDA_INIT)

    for i in reversed(range(seq // tq)):
        lo, hi = i * tq, (i + 1) * tq
        q = q_ref[lo:hi, :]
        zero = jnp.zeros_like(q)
        probs = []
        for qm in (jnp.where(map_a, q, zero), jnp.where(map_a, zero, q)):
            s_dg = lax.dot_general(qm, k_ref[lo:hi, :], _NT_DIMS, preferred_element_type=F32)
            s_dg = jnp.where(causal, s_dg, -jnp.inf)
            mx = jnp.max(s_dg, axis=-1, keepdims=True)
            if i > 0:
                s_off = lax.dot_general(qm, k_ref[0:lo, :], _NT_DIMS, preferred_element_type=F32)
                mx = jnp.maximum(mx, jnp.max(s_off, axis=-1, keepdims=True))
                p_off = jnp.exp2(s_off - mx)
            p_dg = jnp.exp2(s_dg - mx)
            l = jnp.sum(p_dg, axis=-1, keepdims=True)
            if i > 0:
                l = l + jnp.sum(p_off, axis=-1, keepdims=True)
                probs.append((p_off, p_dg, l))
            else:
                probs.append((None, p_dg, l))
        (p0_off, p0_dg, l0), (p1_off, p1_dg, l1) = probs
        c = lam * l0 / l1
        o = jnp.dot((p0_dg - c * p1_dg).astype(BF16), v_ref[lo:hi, :], preferred_element_type=F32)
        if i > 0:
            o = o + jnp.dot((p0_off - c * p1_off).astype(BF16), v_ref[0:lo, :],
                            preferred_element_type=F32)
        o = o / l0
        ms = jnp.mean(o * o, axis=-1, keepdims=True)
        o_ref[lo:hi, :] = (o * lax.rsqrt(ms + EPS) * gain).astype(BF16)


def _diff_attn(dq, dk, dv, lam_p, g_sub, bsz, seq, to_bf16):
    t = dq.shape[0]
    heads = 4
    blk = lambda b, h: (b, h)
    slab = lambda b, h: (b * heads + h, 0)
    slab_specs = [pl.BlockSpec((w.shape[0] // (bsz * heads), w.shape[1]), slab) for w in to_bf16]
    return pl.pallas_call(
        _diff_kernel,
        grid=(bsz, heads),
        in_specs=[pl.BlockSpec((4, HD), lambda b, h: (0, 0)),
                  pl.BlockSpec((1, LANES), lambda b, h: (0, 0)),
                  pl.BlockSpec((seq, LANES), blk),
                  pl.BlockSpec((seq, LANES), blk),
                  pl.BlockSpec((seq, LANES), blk)] + slab_specs,
        out_specs=[pl.BlockSpec((seq, LANES), blk)] + slab_specs,
        out_shape=[jax.ShapeDtypeStruct((t, 512), BF16)]
        + [jax.ShapeDtypeStruct(w.shape, BF16) for w in to_bf16],
        compiler_params=_cparams(("arbitrary", "arbitrary")),
        name="diff_attn",
    )(lam_p, g_sub, dq, dk, dv, *to_bf16)


def _swa_kernel(sink_ref, q_ref, k_ref, v_ref, o_ref):
    seq = q_ref.shape[0]
    g = pl.program_id(1)
    rows = 4 * BLK
    hsel = jnp.right_shift(lax.broadcasted_iota(I32, (rows, 8), 0), 7) + 4 * g
    sink_col = jnp.sum(jnp.where(lax.broadcasted_iota(I32, (rows, 8), 1) == hsel,
                                 sink_ref[...] * LOG2E, 0.0), axis=-1, keepdims=True)
    qi = lax.broadcasted_iota(I32, (rows, 2 * BLK), 0) & (BLK - 1)
    kj = lax.broadcasted_iota(I32, (rows, 2 * BLK), 1)
    dist = qi + BLK - kj
    band = (dist >= 0) & (dist < BLK)
    first = (lax.broadcasted_iota(I32, (rows, BLK), 1)
             <= (lax.broadcasted_iota(I32, (rows, BLK), 0) & (BLK - 1)))
    low64 = lax.broadcasted_iota(I32, (BLK, LANES), 1) < HD

    def attend(q_rows, k_rows, mask):
        qs = jnp.concatenate([q_ref[q_rows, LANES * j:LANES * (j + 1)] for j in range(4)], axis=0)
        s = lax.dot_general(qs, k_ref[k_rows, :], _NT_DIMS, preferred_element_type=F32)
        s = jnp.where(mask, s, -jnp.inf)
        mx = jnp.maximum(jnp.max(s, axis=-1, keepdims=True), sink_col)
        p = jnp.exp2(s - mx)
        l = jnp.sum(p, axis=-1, keepdims=True) + jnp.exp2(sink_col - mx)
        o = jnp.dot(p.astype(BF16), v_ref[k_rows, :], preferred_element_type=F32) / l
        for c in range(2):
            even = o[(2 * c) * BLK:(2 * c + 1) * BLK, :]
            odd = o[(2 * c + 1) * BLK:(2 * c + 2) * BLK, :]
            o_ref[q_rows, LANES * c:LANES * (c + 1)] = jnp.where(
                low64, even, pltpu.roll(odd, HD, 1)).astype(BF16)

    attend(pl.ds(0, BLK), pl.ds(0, BLK), first)

    for n in range(1, seq // BLK):
        attend(pl.ds(n * BLK, BLK), pl.ds((n - 1) * BLK, 2 * BLK), band)


def _swa_attn(sq, sk, sv, sinks, bsz, seq):
    t = sq.shape[0]
    return pl.pallas_call(
        _swa_kernel,
        grid=(bsz, 2),
        in_specs=[pl.BlockSpec((1, 8), lambda b, g: (0, 0)),
                  pl.BlockSpec((seq, 512), lambda b, g: (b, g)),
                  pl.BlockSpec((seq, LANES), lambda b, g: (b, g)),
                  pl.BlockSpec((seq, LANES), lambda b, g: (b, g))],
        out_specs=pl.BlockSpec((seq, 256), lambda b, g: (b, g)),
        out_shape=jax.ShapeDtypeStruct((t, 512), BF16),
        compiler_params=_cparams(("arbitrary", "arbitrary")),
        name="swa_attn",
    )(sinks, sq, sk, sv)


def _bf16_pieces(w):
    p0 = w.astype(BF16).astype(F32)
    r1 = w - p0
    p1 = r1.astype(BF16).astype(F32)
    return p0, p1, r1 - p1


def _outproj_kernel(od_ref, os_ref, x_hbm, mod_ref, g_ref, wo_ref, wr_ref, br_ref, tri_ref,
                    x1_ref, h2_ref, bucket_ref, rank_ref, cnt_ref, x_ring, x_sems, *, n_steps):
    tm = x1_ref.shape[0]
    x_ref = _ring_fetch(x_hbm, x_ring, x_sems, pl.program_id(0), n_steps)
    m = mod_ref[0]
    logit_rows = []
    for r0 in range(0, tm, TM_OUT_SUB):
        rows = slice(r0, r0 + TM_OUT_SUB)
        mix = (jnp.dot(od_ref[rows, :], wo_ref[0:512, :], preferred_element_type=F32)
               + jnp.dot(os_ref[rows, :], wo_ref[512:1024, :], preferred_element_type=F32))
        x1 = x_ref[rows, :] + m[2:3] * mix
        x1_ref[rows, :] = x1
        ms = jnp.mean(x1 * x1, axis=-1, keepdims=True)
        h2 = (x1 * lax.rsqrt(ms + EPS) * g_ref[...] * (1.0 + m[4:5]) + m[3:4]).astype(BF16)
        h2_ref[rows, 0:D] = h2
        logit_rows.append(jnp.dot(h2, wr_ref[...], preferred_element_type=F32) + br_ref[...])
    lt = jnp.concatenate(logit_rows, axis=0).T
    r = [lt[i:i + 1, :] for i in range(N_GROUPS + N_GROUPS * EPG)]

    gl = r[0:N_GROUPS]
    gmax = jnp.maximum(jnp.maximum(gl[0], gl[1]), jnp.maximum(gl[2], gl[3]))
    gidx = jnp.where(gl[0] == gmax, 0, jnp.where(gl[1] == gmax, 1, jnp.where(gl[2] == gmax, 2, 3)))
    gz = (jnp.exp(gl[0] - gmax) + jnp.exp(gl[1] - gmax)
          + jnp.exp(gl[2] - gmax) + jnp.exp(gl[3] - gmax))
    grp_p = 1.0 / gz

    el = []
    for i in range(EPG):
        e = r[N_GROUPS + 3 * EPG + i]
        for gg in (2, 1, 0):
            e = jnp.where(gidx == gg, r[N_GROUPS + gg * EPG + i], e)
        el.append(e)
    emax = jnp.maximum(jnp.maximum(el[0], el[1]), jnp.maximum(el[2], el[3]))
    ex = [jnp.exp(e - emax) for e in el]
    ez = ex[0] + ex[1] + ex[2] + ex[3]
    pr = [e / ez for e in ex]
    p1 = jnp.maximum(jnp.maximum(pr[0], pr[1]), jnp.maximum(pr[2], pr[3]))
    a = jnp.where(pr[0] == p1, 0, jnp.where(pr[1] == p1, 1, jnp.where(pr[2] == p1, 2, 3)))
    rest = [jnp.where(a == i, -1.0, pr[i]) for i in range(EPG)]
    p2 = jnp.maximum(jnp.maximum(rest[0], rest[1]), jnp.maximum(rest[2], rest[3]))
    b = jnp.where(rest[0] == p2, 0, jnp.where(rest[1] == p2, 1, jnp.where(rest[2] == p2, 2, 3)))
    psum = p1 + p2
    wa = grp_p * (p1 / psum)
    wb = grp_p * (p2 / psum)
    lo = jnp.minimum(a, b)
    hi = jnp.maximum(a, b)
    w_lo = jnp.where(a < b, wa, wb)
    w_hi = jnp.where(a < b, wb, wa)
    pair = jnp.where(lo == 0, hi - 1, jnp.where(lo == 1, hi + 1, 5))
    bucket = gidx * 6 + pair

    rid = lax.broadcasted_iota(I32, (AUG, tm), 0)
    aug = jnp.zeros((AUG, tm), F32)
    for k, piece in enumerate(_bf16_pieces(w_lo) + _bf16_pieces(w_hi)):
        aug = jnp.where(rid == k, piece, aug)
    h2_ref[:, D:D + AUG] = aug.T.astype(BF16)

    oh = (lax.broadcasted_iota(I32, (BUCKET_ROWS, tm), 0) == bucket)
    oh_f = jnp.where(oh, 1.0, 0.0)
    before = jnp.dot(oh_f.astype(BF16), tri_ref[...], preferred_element_type=F32)
    rank = jnp.sum(oh_f * before, axis=0, keepdims=True)
    cnt_ref[0] = jnp.broadcast_to(jnp.sum(oh_f, axis=1, keepdims=True), (BUCKET_ROWS, LANES))
    bucket_ref[0] = bucket
    rank_ref[0] = rank.astype(I32)


def _out_proj(o_diff, o_swa, x2, mod3, g_ffn, w_out_bf, w_r, b_r, tri, seq):
    t = x2.shape[0]
    tm = TM_PROJ
    per_b = seq // tm
    row = lambda i: (i, 0)
    const = lambda i: (0, 0)
    return pl.pallas_call(
        functools.partial(_outproj_kernel, n_steps=t // tm),
        grid=(t // tm,),
        in_specs=[pl.BlockSpec((tm, 512), row), pl.BlockSpec((tm, 512), row),
                  pl.BlockSpec(memory_space=pl.ANY),
                  pl.BlockSpec((1, 6, D), lambda i: (i // per_b, 0, 0)),
                  pl.BlockSpec((1, D), const),
                  pl.BlockSpec((D, D), const),
                  pl.BlockSpec((D, LANES), const),
                  pl.BlockSpec((1, LANES), const),
                  pl.BlockSpec((tm, tm), const)],
        out_specs=[pl.BlockSpec((tm, D), row),
                   pl.BlockSpec((tm, D + AUG), row),
                   pl.BlockSpec((1, 1, tm), lambda i: (i, 0, 0)),
                   pl.BlockSpec((1, 1, tm), lambda i: (i, 0, 0)),
                   pl.BlockSpec((1, BUCKET_ROWS, LANES), lambda i: (i, 0, 0))],
        out_shape=[jax.ShapeDtypeStruct((t, D), F32),
                   jax.ShapeDtypeStruct((t, D + AUG), BF16),
                   jax.ShapeDtypeStruct((t // tm, 1, tm), I32),
                   jax.ShapeDtypeStruct((t // tm, 1, tm), I32),
                   jax.ShapeDtypeStruct((t // tm, BUCKET_ROWS, LANES), F32)],
        scratch_shapes=[pltpu.VMEM((RING, tm, D), F32), pltpu.SemaphoreType.DMA((RING,))],
        compiler_params=_cparams(("arbitrary",)),
        name="out_proj",
    )(o_diff, o_swa, x2, mod3, g_ffn, w_out_bf, w_r, b_r, tri)


META_LOCAL, META_GLOBAL, META_ROWS, META_TILE_ROWS = 0, 1, 2, 3


def _local_pos(meta_ref, tile, bucket, rank):
    pos = rank
    for b in range(N_BUCKETS):
        pos = pos + jnp.where(bucket == b, meta_ref[META_LOCAL, tile * N_BUCKETS + b], 0)
    return pos


def _for_each_run(meta_ref, tile, fn):
    for b in range(N_BUCKETS):
        rows = meta_ref[META_ROWS, tile * N_BUCKETS + b]

        @pl.when(rows > 0)
        def _(b=b, rows=rows):
            fn(pl.multiple_of(meta_ref[META_LOCAL, tile * N_BUCKETS + b], SUBLANES),
               pl.multiple_of(meta_ref[META_GLOBAL, tile * N_BUCKETS + b], SUBLANES),
               pl.multiple_of(rows, SUBLANES))


def _dispatch_kernel(meta_ref, ztail_ref, nvalid_ref, h2_hbm, bucket_ref, rank_ref, xs_hbm,
                     sort_scr, zero_scr, sems, zsem, h2_ring, h2_sems, *, n_steps):
    i = pl.program_id(0)
    n_tiles = xs_hbm.shape[0] // TM_MOE
    tm = h2_ring.shape[1]
    slot = i % 2
    h2_ref = _ring_fetch(h2_hbm, h2_ring, h2_sems, i, n_steps)

    def ztail_copy(b):
        rows = pl.multiple_of(ztail_ref[1, b], SUBLANES)
        return pltpu.make_async_copy(
            zero_scr.at[pl.ds(0, rows)],
            xs_hbm.at[pl.ds(pl.multiple_of(ztail_ref[0, b], SUBLANES), rows)], zsem)

    def ztile_copy(tile):
        return pltpu.make_async_copy(
            zero_scr, xs_hbm.at[pl.ds(pl.multiple_of(tile * TM_MOE, TM_MOE), TM_MOE)], zsem)

    @pl.when(i == 0)
    def _():
        zero_scr[...] = jnp.zeros_like(zero_scr)
        for b in range(N_BUCKETS):
            @pl.when(ztail_ref[1, b] > 0)
            def _(b=b):
                ztail_copy(b).start()
        lax.fori_loop(nvalid_ref[0], n_tiles, lambda tile, c: (ztile_copy(tile).start(), c)[1], 0)

    def run_copy(buf, local_row, global_row, rows):
        return pltpu.make_async_copy(sort_scr.at[buf, pl.ds(local_row, rows)],
                                     xs_hbm.at[pl.ds(global_row, rows)], sems.at[buf])

    def wait_runs(tile, buf):
        run_copy(buf, 0, 0, pl.multiple_of(meta_ref[META_TILE_ROWS, tile], SUBLANES)).wait()

    @pl.when(i >= 2)
    def _():
        wait_runs(i - 2, slot)

    lpos = _local_pos(meta_ref, i, bucket_ref[0], rank_ref[0])
    lane = lax.broadcasted_iota(I32, (LS_SUB, AUG), 1)
    for r0 in range(0, LS, LS_SUB):
        row_id = lax.broadcasted_iota(I32, (LS_SUB, tm), 0) + r0
        onehot = jnp.where(row_id == lpos, 1.0, 0.0).astype(BF16)
        srt = jnp.dot(onehot, h2_ref[...], preferred_element_type=F32)
        aug = srt[:, D:D + AUG]
        w_lo = aug[:, 0:1] + aug[:, 1:2] + aug[:, 2:3]
        w_hi = aug[:, 3:4] + aug[:, 4:5] + aug[:, 5:6]
        sort_scr[slot, r0:r0 + LS_SUB, 0:D] = srt[:, 0:D]
        sort_scr[slot, r0:r0 + LS_SUB, D:XW] = jnp.where(lane == 0, w_lo, jnp.where(lane == 1, w_hi, 0.0))
    _for_each_run(meta_ref, i, lambda lr, gr, rows: run_copy(slot, lr, gr, rows).start())

    @pl.when(i == n_steps - 1)
    def _():
        @pl.when(i >= 1)
        def _():
            wait_runs(i - 1, 1 - slot)
        wait_runs(i, slot)
        for b in range(N_BUCKETS):
            @pl.when(ztail_ref[1, b] > 0)
            def _(b=b):
                ztail_copy(b).wait()
        lax.fori_loop(nvalid_ref[0], n_tiles, lambda tile, c: (ztile_copy(tile).wait(), c)[1], 0)


def _dispatch(meta, ztail, n_valid, h2a, bucket, rank, n_rows):
    t, width = h2a.shape
    tm = TM_PROJ
    grid_spec = pltpu.PrefetchScalarGridSpec(
        num_scalar_prefetch=3,
        grid=(t // tm,),
        in_specs=[pl.BlockSpec(memory_space=pl.ANY),
                  pl.BlockSpec((1, 1, tm), lambda i, m, z, nv: (i, 0, 0)),
                  pl.BlockSpec((1, 1, tm), lambda i, m, z, nv: (i, 0, 0))],
        out_specs=pl.BlockSpec(memory_space=pl.ANY),
        scratch_shapes=[pltpu.VMEM((2, LS, XW), F32), pltpu.VMEM((TM_MOE, XW), F32),
                        pltpu.SemaphoreType.DMA((2,)), pltpu.SemaphoreType.DMA(()),
                        pltpu.VMEM((RING, tm, width), BF16), pltpu.SemaphoreType.DMA((RING,))],
    )
    return pl.pallas_call(
        functools.partial(_dispatch_kernel, n_steps=t // tm),
        grid_spec=grid_spec,
        out_shape=jax.ShapeDtypeStruct((n_rows, XW), F32),
        compiler_params=_cparams(("arbitrary",)),
        name="dispatch",
    )(meta, ztail, n_valid, h2a, bucket, rank)


def _moe_kernel(lo_ref, hi_ref, nvalid_ref, xs_ref, wg_lo, wu_lo, wd_lo, wg_hi, wu_hi, wd_hi, y_ref):
    del lo_ref, hi_ref
    t = pl.program_id(0)

    @pl.when(t < nvalid_ref[0])
    def _():
        xb = xs_ref[:, 0:D].astype(BF16)
        y = None
        for k, (wg, wu, wd) in enumerate(((wg_lo, wu_lo, wd_lo), (wg_hi, wu_hi, wd_hi))):
            gt = jnp.dot(xb, wg[0], preferred_element_type=F32)
            up = jnp.dot(xb, wu[0], preferred_element_type=F32)
            act = (gt * (1.0 / (1.0 + jnp.exp(-gt))) * up).astype(BF16)
            dn = jnp.dot(act, wd[0], preferred_element_type=F32)
            term = xs_ref[:, D + k:D + k + 1] * dn
            y = term if y is None else y + term
        y_ref[...] = y

    @pl.when(t >= nvalid_ref[0])
    def _():
        y_ref[...] = jnp.zeros_like(y_ref)


def _moe(tile_lo, tile_hi, n_valid, xs, wg, wu, wd):
    n_rows, width = xs.shape
    n_tiles = n_rows // TM_MOE
    up_lo = lambda t, lo, hi, nv: (lo[t], 0, 0)
    up_hi = lambda t, lo, hi, nv: (hi[t], 0, 0)
    grid_spec = pltpu.PrefetchScalarGridSpec(
        num_scalar_prefetch=3,
        grid=(n_tiles,),
        in_specs=[pl.BlockSpec((TM_MOE, width), lambda t, lo, hi, nv: (jnp.minimum(t, nv[0] - 1), 0)),
                  pl.BlockSpec((1, D, D_EXPERT), up_lo), pl.BlockSpec((1, D, D_EXPERT), up_lo),
                  pl.BlockSpec((1, D_EXPERT, D), up_lo),
                  pl.BlockSpec((1, D, D_EXPERT), up_hi), pl.BlockSpec((1, D, D_EXPERT), up_hi),
                  pl.BlockSpec((1, D_EXPERT, D), up_hi)],
        out_specs=pl.BlockSpec((TM_MOE, D), lambda t, lo, hi, nv: (t, 0)),
    )
    return pl.pallas_call(
        _moe_kernel,
        grid_spec=grid_spec,
        out_shape=jax.ShapeDtypeStruct((n_rows, D), F32),
        compiler_params=_cparams(("arbitrary",)),
        name="moe",
    )(tile_lo, tile_hi, n_valid, xs, wg, wu, wd, wg, wu, wd)


def _final_kernel(meta_ref, ys_hbm, x1_hbm, mod_ref, g_ref, bucket_ref, rank_ref, o_ref, y_scr, sems,
                  x1_ring, x1_sems, *, n_steps):
    i = pl.program_id(0)
    tm = o_ref.shape[0]
    slot = i % 2
    x1_ref = _ring_fetch(x1_hbm, x1_ring, x1_sems, i, n_steps)

    def run_copy(buf, local_row, global_row, rows):
        return pltpu.make_async_copy(ys_hbm.at[pl.ds(global_row, rows)],
                                     y_scr.at[buf, pl.ds(local_row, rows)], sems.at[buf])

    def fetch_runs(tile, buf):
        _for_each_run(meta_ref, tile, lambda lr, gr, rows: run_copy(buf, lr, gr, rows).start())

    @pl.when(i == 0)
    def _():
        y_scr[...] = jnp.zeros_like(y_scr)
        fetch_runs(0, 0)

    @pl.when(i + 1 < n_steps)
    def _():
        fetch_runs(i + 1, 1 - slot)

    run_copy(slot, 0, 0, pl.multiple_of(meta_ref[META_TILE_ROWS, i], SUBLANES)).wait()

    lpos = _local_pos(meta_ref, i, bucket_ref[0], rank_ref[0])
    lpos_col = jnp.broadcast_to(lpos.astype(F32), (LANES, tm)).T[:, 0:1].astype(I32)
    ysrt = y_scr[slot].astype(BF16)
    m = mod_ref[0]
    for r0 in range(0, tm, TM_OUT_SUB):
        rows = slice(r0, r0 + TM_OUT_SUB)
        onehot = jnp.where(lax.broadcasted_iota(I32, (TM_OUT_SUB, LS), 1) == lpos_col[rows], 1.0, 0.0)
        y = jnp.dot(onehot.astype(BF16), ysrt, preferred_element_type=F32)
        x2 = x1_ref[rows, :] + m[5:6] * y
        ms = jnp.mean(x2 * x2, axis=-1, keepdims=True)
        o_ref[rows, :] = x2 * lax.rsqrt(ms + EPS) * g_ref[...]


def _final(meta, ys, x1, mod3, g_final, bucket, rank, seq):
    t = x1.shape[0]
    tm = TM_PROJ
    per_b = seq // tm
    grid_spec = pltpu.PrefetchScalarGridSpec(
        num_scalar_prefetch=1,
        grid=(t // tm,),
        in_specs=[pl.BlockSpec(memory_space=pl.ANY),
                  pl.BlockSpec(memory_space=pl.ANY),
                  pl.BlockSpec((1, 6, D), lambda i, m: (i // per_b, 0, 0)),
                  pl.BlockSpec((1, D), lambda i, m: (0, 0)),
                  pl.BlockSpec((1, 1, tm), lambda i, m: (i, 0, 0)),
                  pl.BlockSpec((1, 1, tm), lambda i, m: (i, 0, 0))],
        out_specs=pl.BlockSpec((tm, D), lambda i, m: (i, 0)),
        scratch_shapes=[pltpu.VMEM((2, LS, D), F32), pltpu.SemaphoreType.DMA((2,)),
                        pltpu.VMEM((RING, tm, D), F32), pltpu.SemaphoreType.DMA((RING,))],
    )
    return pl.pallas_call(
        functools.partial(_final_kernel, n_steps=t // tm),
        grid_spec=grid_spec,
        out_shape=jax.ShapeDtypeStruct((t, D), F32),
        compiler_params=_cparams(("arbitrary",)),
        name="final",
    )(meta, ys, x1, mod3, g_final, bucket, rank)


def _rope_tables(seq):
    inv = ROPE_THETA ** (-jnp.arange(0, HD, 2, dtype=F32) / HD)
    ang = jnp.arange(seq, dtype=F32)[:, None] * inv[None, :]
    cos, sin = jnp.cos(ang), jnp.sin(ang)
    return jnp.tile(cos, (1, 4)), jnp.concatenate([-sin, -sin, sin, sin], axis=-1)


def _rotary_column_order():
    half = HD // 2
    lane = np.arange(LANES)
    which, part, f = (lane // half) % 2, lane // HD, lane % half
    cols = []
    for base in (0, 512):
        for h in range(4):
            cols.append(base + h * LANES + which * HD + part * half + f)
    cols.append(np.arange(1024, 1536))
    for c in range(4):
        cols.append(1536 + (c + 4 * which) * HD + part * half + f)
    cols.append(2048 + which * HD + part * half + f)
    cols.append(np.arange(2176, 2304))
    return np.concatenate(cols)


def kernel(x, c, w_ada, b_ada, g_mix, w_in, diff_lambda, g_diff_sub, swa_sinks, w_out, g_ffn,
           w_route_group, b_route_group, w_route_expert, b_route_expert, w_gate, w_up, w_down,
           g_final):
    bsz, seq, _ = x.shape
    t = bsz * seq
    x2 = x.reshape(t, D)
    cos, sin = _rope_tables(seq)

    mod3 = _ada_mod(c, w_ada[0], b_ada[0].reshape(1, -1)).reshape(bsz, 6, D)

    w_in_bf = w_in[0][:, _rotary_column_order()].astype(BF16)
    dq, dk, dv, sq, sk, sv = _in_proj(x2, mod3, g_mix[0].reshape(1, D), w_in_bf, cos, sin, seq)
    n_exp = w_gate.shape[1]
    o_diff, wg_bf, wu_bf, wd_bf = _diff_attn(
        dq, dk, dv, diff_lambda[0], g_diff_sub[0].reshape(1, LANES), bsz, seq,
        (w_gate[0].reshape(n_exp * D, D_EXPERT), w_up[0].reshape(n_exp * D, D_EXPERT),
         w_down[0].reshape(n_exp * D_EXPERT, D)))
    o_swa = _swa_attn(sq, sk, sv, swa_sinks[0].reshape(1, 8), bsz, seq)

    n_r = N_GROUPS + N_GROUPS * EPG
    w_r = jnp.concatenate([w_route_group[0], w_route_expert[0],
                           jnp.zeros((D, LANES - n_r), F32)], axis=1).astype(BF16)
    b_r = jnp.concatenate([b_route_group[0], b_route_expert[0],
                           jnp.zeros((LANES - n_r,), F32)]).reshape(1, LANES)
    tri = (jnp.arange(TM_PROJ)[:, None] < jnp.arange(TM_PROJ)[None, :]).astype(BF16)
    x1, h2a, bucket, rank, counts = _out_proj(o_diff, o_swa, x2, mod3, g_ffn[0].reshape(1, D),
                                              w_out[0].astype(BF16), w_r, b_r, tri, seq)

    n_tok_tiles = t // TM_PROJ
    n_tiles = -(-(t + n_tok_tiles * N_BUCKETS * (SUBLANES - 1)) // TM_MOE) + N_BUCKETS
    cnt = counts[:, :N_BUCKETS, 0].astype(I32)
    run_rows = (cnt + SUBLANES - 1) // SUBLANES * SUBLANES
    local_off = jnp.cumsum(run_rows, axis=1) - run_rows
    tiles_per = (jnp.sum(run_rows, axis=0) + TM_MOE - 1) // TM_MOE
    tile_end = jnp.cumsum(tiles_per)
    bucket_start = (tile_end - tiles_per) * TM_MOE
    global_off = bucket_start[None, :] + jnp.cumsum(run_rows, axis=0) - run_rows
    tile_rows = jnp.pad(jnp.sum(run_rows, axis=1), (0, n_tok_tiles * (N_BUCKETS - 1)))
    meta = jnp.stack([local_off.reshape(-1), global_off.reshape(-1), run_rows.reshape(-1), tile_rows])
    tid = jnp.arange(n_tiles, dtype=I32)
    tile_bucket = jnp.minimum(jnp.sum(tid[:, None] >= tile_end[None, :], axis=1), N_BUCKETS - 1)
    n_valid = tile_end[-1:].astype(I32)
    bucket_rows = jnp.sum(run_rows, axis=0)
    ztail = jnp.stack([bucket_start + bucket_rows, tiles_per * TM_MOE - bucket_rows]).astype(I32)
    grp, pair = tile_bucket // 6, tile_bucket % 6
    tile_lo = (grp * EPG + jnp.asarray(PAIR_LO, I32)[pair]).astype(I32)
    tile_hi = (grp * EPG + jnp.asarray(PAIR_HI, I32)[pair]).astype(I32)

    xs = _dispatch(meta, ztail, n_valid, h2a, bucket, rank, n_tiles * TM_MOE)
    ys = _moe(tile_lo, tile_hi, n_valid, xs, wg_bf.reshape(n_exp, D, D_EXPERT),
              wu_bf.reshape(n_exp, D, D_EXPERT), wd_bf.reshape(n_exp, D_EXPERT, D))
    out = _final(meta, ys, x1, mod3, g_final.reshape(1, D), bucket, rank, seq)
    return out.reshape(bsz, seq, D)
```

```python
import functools
import math

import jax
import jax.numpy as jnp
import numpy as np
from jax import lax
from jax.experimental import pallas as pl
from jax.experimental.pallas import tpu as pltpu

F32 = jnp.float32
BF16 = jnp.bfloat16
I32 = jnp.int32

D = 1024
HD = 64
EPS = 1e-6
ROPE_THETA = 10000.0
LOG2E = math.log2(math.e)
LAMBDA_INIT = 0.8 - 0.6 * math.exp(-0.3 * 0)
N_GROUPS = 4
EPG = 4
N_BUCKETS = 24
BUCKET_ROWS = 32
D_EXPERT = 512
AUG = 128

LANES = 128
VMEM_LIMIT = 48 * 1024 * 1024

TM_IN = 1024
TM_IN_SUB = 256
TM_PROJ = 512
TM_OUT_SUB = 256
TQ = 512
BLK = 128
TM_MOE = 512
SUBLANES = 8
RUN_ROWS = TM_PROJ + N_BUCKETS * (SUBLANES - 1)
LS = -(-RUN_ROWS // 64) * 64
LS_SUB = LS // 2
XW = D + AUG

PAIR_LO = (0, 0, 0, 1, 1, 2)
PAIR_HI = (1, 2, 3, 2, 3, 3)


def _cparams(sem):
    return pltpu.CompilerParams(dimension_semantics=sem, vmem_limit_bytes=VMEM_LIMIT)


RING = 3


def _ring_fetch(src_hbm, ring, sems, step, n_steps):
    rows = ring.shape[1]

    def copy(s):
        return pltpu.make_async_copy(src_hbm.at[pl.ds(pl.multiple_of(s * rows, rows), rows)],
                                     ring.at[s % RING], sems.at[s % RING])

    @pl.when(step == 0)
    def _():
        for s in range(min(RING - 1, n_steps)):
            copy(s).start()

    @pl.when(step + RING - 1 < n_steps)
    def _():
        copy(step + RING - 1).start()

    copy(step).wait()
    return ring.at[step % RING]


def _ada_kernel(c_ref, w_ref, b_ref, o_ref):
    c = c_ref[...]
    ca = c * (1.0 / (1.0 + jnp.exp(-c)))
    o_ref[...] = jnp.dot(ca.astype(BF16), w_ref[...].astype(BF16),
                         preferred_element_type=F32) + b_ref[...]


def _ada_mod(c, w, b):
    bsz = c.shape[0]
    n = w.shape[1]
    tn = 512
    return pl.pallas_call(
        _ada_kernel,
        grid=(n // tn,),
        in_specs=[pl.BlockSpec((bsz, D), lambda j: (0, 0)),
                  pl.BlockSpec((D, tn), lambda j: (0, j)),
                  pl.BlockSpec((1, tn), lambda j: (0, j))],
        out_specs=pl.BlockSpec((bsz, tn), lambda j: (0, j)),
        out_shape=jax.ShapeDtypeStruct((bsz, n), F32),
        compiler_params=_cparams(("arbitrary",)),
        name="ada_mod",
    )(c, w, b)


def _rope(a, cos, sin):
    return a * cos + pltpu.roll(a, HD, 1) * sin


def _inproj_kernel(x_ref, mod_ref, g_ref, w_ref, cos_ref, sin_ref,
                   dq_ref, dk_ref, dv_ref, sq_ref, sk_ref, sv_ref, h_scr):
    tm = x_ref.shape[0]
    sub = TM_IN_SUB
    m = mod_ref[0]
    lane = lax.broadcasted_iota(I32, (sub, LANES), 1)
    head_a = (lane & 32) == 0
    low64 = lane < HD
    scale = (HD ** -0.5) * LOG2E

    def halves(a):
        return a[:, :LANES], a[:, LANES:]

    for r0 in range(0, tm, sub):
        rows = slice(r0, r0 + sub)
        x = x_ref[rows, :]
        ms = jnp.mean(x * x, axis=-1, keepdims=True)
        y = x * lax.rsqrt(ms + EPS) * g_ref[...]
        h_scr[rows, :] = (y * (1.0 + m[1:2]) + m[0:1]).astype(BF16)
        cos = cos_ref[rows, :]
        sin = sin_ref[rows, :]

        def chunk(j):
            return jnp.dot(h_scr[rows, :], w_ref[:, 256 * j:256 * (j + 1)], preferred_element_type=F32)

        for j in range(2):
            for t, r in enumerate(halves(chunk(j))):
                c0 = 256 * j + LANES * t
                dq_ref[rows, c0:c0 + LANES] = (_rope(r, cos, sin) * scale).astype(BF16)
        for j in range(2):
            for t, r in enumerate(halves(chunk(2 + j))):
                c0 = 256 * j + LANES * t
                dk_ref[rows, c0:c0 + LANES] = _rope(r, cos, sin).astype(BF16)
        for j in range(2):
            dv_ref[rows, 256 * j:256 * (j + 1)] = chunk(4 + j).astype(BF16)
        for j in range(2):
            for t, r in enumerate(halves(chunk(6 + j))):
                rp = _rope(r, cos, sin) * scale
                c = 2 * j + t
                sq_ref[rows, LANES * c:LANES * (c + 1)] = jnp.where(head_a, rp, 0.0).astype(BF16)
                sq_ref[rows, LANES * (c + 4):LANES * (c + 5)] = jnp.where(head_a, 0.0, rp).astype(BF16)
        kv = chunk(8)
        kk, vv = halves(kv)
        kr = _rope(kk, cos, sin)
        sk_ref[rows, :LANES] = jnp.where(head_a, kr, 0.0).astype(BF16)
        sk_ref[rows, LANES:] = jnp.where(head_a, 0.0, kr).astype(BF16)
        sv_ref[rows, :LANES] = jnp.where(low64, vv, 0.0).astype(BF16)
        sv_ref[rows, LANES:] = jnp.where(low64, pltpu.roll(vv, HD, 1), 0.0).astype(BF16)


def _in_proj(x2, mod3, g_mix, w_in_bf, cos, sin, seq):
    t = x2.shape[0]
    tm = TM_IN
    per_b = seq // tm
    n_in = w_in_bf.shape[1]
    row = lambda i: (i, 0)
    return pl.pallas_call(
        _inproj_kernel,
        grid=(t // tm,),
        in_specs=[pl.BlockSpec((tm, D), row),
                  pl.BlockSpec((1, 6, D), lambda i: (i // per_b, 0, 0)),
                  pl.BlockSpec((1, D), lambda i: (0, 0)),
                  pl.BlockSpec((D, n_in), lambda i: (0, 0)),
                  pl.BlockSpec((tm, LANES), lambda i: (i % per_b, 0)),
                  pl.BlockSpec((tm, LANES), lambda i: (i % per_b, 0))],
        out_specs=[pl.BlockSpec((tm, 512), row), pl.BlockSpec((tm, 512), row),
                   pl.BlockSpec((tm, 512), row), pl.BlockSpec((tm, 1024), row),
                   pl.BlockSpec((tm, 256), row), pl.BlockSpec((tm, 256), row)],
        out_shape=[jax.ShapeDtypeStruct((t, 512), BF16), jax.ShapeDtypeStruct((t, 512), BF16),
                   jax.ShapeDtypeStruct((t, 512), BF16), jax.ShapeDtypeStruct((t, 1024), BF16),
                   jax.ShapeDtypeStruct((t, 256), BF16), jax.ShapeDtypeStruct((t, 256), BF16)],
        scratch_shapes=[pltpu.VMEM((tm, D), BF16)],
        compiler_params=_cparams(("arbitrary",)),
        name="in_proj",
    )(x2, mod3, g_mix, w_in_bf, cos, sin)


_NT_DIMS = (((1,), (1,)), ((), ()))


def _diff_kernel(lam_ref, g_ref, q_ref, k_ref, v_ref, *rest):
    n_cast = (len(rest) - 1) // 2
    o_ref = rest[n_cast]
    for src, dst in zip(rest[:n_cast], rest[n_cast + 1:]):
        dst[...] = src[...].astype(BF16)

    seq = q_ref.shape[0]
    tq = TQ
    lp = lam_ref[...]
    lam = (jnp.exp(jnp.sum(lp[0:1] * lp[1:2], axis=-1, keepdims=True))
           - jnp.exp(jnp.sum(lp[2:3] * lp[3:4], axis=-1, keepdims=True)) + LAMBDA_INIT)
    lane = lax.broadcasted_iota(I32, (tq, LANES), 1)
    map_a = (lane & 32) == 0
    causal = (lax.broadcasted_iota(I32, (tq, tq), 1) <= lax.broadcasted_iota(I32, (tq, tq), 0))
    gain = g_ref[...] * (1.0 - LAMBDA_INIT)

    for i in reversed(range(seq // tq)):
        lo, hi = i * tq, (i + 1) * tq
        q = q_ref[lo:hi, :]
        zero = jnp.zeros_like(q)
        probs = []
        for qm in (jnp.where(map_a, q, zero), jnp.where(map_a, zero, q)):
            s_dg = lax.dot_general(qm, k_ref[lo:hi, :], _NT_DIMS, preferred_element_type=F32)
            s_dg = jnp.where(causal, s_dg, -jnp.inf)
            mx = jnp.max(s_dg, axis=-1, keepdims=True)
            if i > 0:
                s_off = lax.dot_general(qm, k_ref[0:lo, :], _NT_DIMS, preferred_element_type=F32)
                mx = jnp.maximum(mx, jnp.max(s_off, axis=-1, keepdims=True))
                p_off = jnp.exp2(s_off - mx)
            p_dg = jnp.exp2(s_dg - mx)
            l = jnp.sum(p_dg, axis=-1, keepdims=True)
            if i > 0:
                l = l + jnp.sum(p_off, axis=-1, keepdims=True)
                probs.append((p_off, p_dg, l))
            else:
                probs.append((None, p_dg, l))
        (p0_off, p0_dg, l0), (p1_off, p1_dg, l1) = probs
        c = lam * l0 / l1
        o = jnp.dot((p0_dg - c * p1_dg).astype(BF16), v_ref[lo:hi, :], preferred_element_type=F32)
        if i > 0:
            o = o + jnp.dot((p0_off - c * p1_off).astype(BF16), v_ref[0:lo, :],
                            preferred_element_type=F32)
        o = o / l0
        ms = jnp.mean(o * o, axis=-1, keepdims=True)
        o_ref[lo:hi, :] = (o * lax.rsqrt(ms + EPS) * gain).astype(BF16)


def _diff_attn(dq, dk, dv, lam_p, g_sub, bsz, seq, to_bf16):
    t = dq.shape[0]
    heads = 4
    blk = lambda b, h: (b, h)
    slab = lambda b, h: (b * heads + h, 0)
    slab_specs = [pl.BlockSpec((w.shape[0] // (bsz * heads), w.shape[1]), slab) for w in to_bf16]
    return pl.pallas_call(
        _diff_kernel,
        grid=(bsz, heads),
        in_specs=[pl.BlockSpec((4, HD), lambda b, h: (0, 0)),
                  pl.BlockSpec((1, LANES), lambda b, h: (0, 0)),
                  pl.BlockSpec((seq, LANES), blk),
                  pl.BlockSpec((seq, LANES), blk),
                  pl.BlockSpec((seq, LANES), blk)] + slab_specs,
        out_specs=[pl.BlockSpec((seq, LANES), blk)] + slab_specs,
        out_shape=[jax.ShapeDtypeStruct((t, 512), BF16)]
        + [jax.ShapeDtypeStruct(w.shape, BF16) for w in to_bf16],
        compiler_params=_cparams(("arbitrary", "arbitrary")),
        name="diff_attn",
    )(lam_p, g_sub, dq, dk, dv, *to_bf16)


def _swa_kernel(sink_ref, q_ref, k_ref, v_ref, o_ref):
    seq = q_ref.shape[0]
    g = pl.program_id(1)
    rows = 4 * BLK
    hsel = jnp.right_shift(lax.broadcasted_iota(I32, (rows, 8), 0), 7) + 4 * g
    sink_col = jnp.sum(jnp.where(lax.broadcasted_iota(I32, (rows, 8), 1) == hsel,
                                 sink_ref[...] * LOG2E, 0.0), axis=-1, keepdims=True)
    qi = lax.broadcasted_iota(I32, (rows, 2 * BLK), 0) & (BLK - 1)
    kj = lax.broadcasted_iota(I32, (rows, 2 * BLK), 1)
    dist = qi + BLK - kj
    band = (dist >= 0) & (dist < BLK)
    first = (lax.broadcasted_iota(I32, (rows, BLK), 1)
             <= (lax.broadcasted_iota(I32, (rows, BLK), 0) & (BLK - 1)))
    low64 = lax.broadcasted_iota(I32, (BLK, LANES), 1) < HD

    def attend(q_rows, k_rows, mask):
        qs = jnp.concatenate([q_ref[q_rows, LANES * j:LANES * (j + 1)] for j in range(4)], axis=0)
        s = lax.dot_general(qs, k_ref[k_rows, :], _NT_DIMS, preferred_element_type=F32)
        s = jnp.where(mask, s, -jnp.inf)
        mx = jnp.maximum(jnp.max(s, axis=-1, keepdims=True), sink_col)
        p = jnp.exp2(s - mx)
        l = jnp.sum(p, axis=-1, keepdims=True) + jnp.exp2(sink_col - mx)
        o = jnp.dot(p.astype(BF16), v_ref[k_rows, :], preferred_element_type=F32) / l
        for c in range(2):
            even = o[(2 * c) * BLK:(2 * c + 1) * BLK, :]
            odd = o[(2 * c + 1) * BLK:(2 * c + 2) * BLK, :]
            o_ref[q_rows, LANES * c:LANES * (c + 1)] = jnp.where(
                low64, even, pltpu.roll(odd, HD, 1)).astype(BF16)

    attend(pl.ds(0, BLK), pl.ds(0, BLK), first)

    for n in range(1, seq // BLK):
        attend(pl.ds(n * BLK, BLK), pl.ds((n - 1) * BLK, 2 * BLK), band)


def _swa_attn(sq, sk, sv, sinks, bsz, seq):
    t = sq.shape[0]
    return pl.pallas_call(
        _swa_kernel,
        grid=(bsz, 2),
        in_specs=[pl.BlockSpec((1, 8), lambda b, g: (0, 0)),
                  pl.BlockSpec((seq, 512), lambda b, g: (b, g)),
                  pl.BlockSpec((seq, LANES), lambda b, g: (b, g)),
                  pl.BlockSpec((seq, LANES), lambda b, g: (b, g))],
        out_specs=pl.BlockSpec((seq, 256), lambda b, g: (b, g)),
        out_shape=jax.ShapeDtypeStruct((t, 512), BF16),
        compiler_params=_cparams(("arbitrary", "arbitrary")),
        name="swa_attn",
    )(sinks, sq, sk, sv)


def _bf16_pieces(w):
    p0 = w.astype(BF16).astype(F32)
    r1 = w - p0
    p1 = r1.astype(BF16).astype(F32)
    return p0, p1, r1 - p1


def _outproj_kernel(od_hbm, os_hbm, x_hbm, mod_ref, g_ref, wo_ref, wr_ref, br_ref, tri_ref,
                    x1_ref, h2_ref, bucket_ref, rank_ref, cnt_ref,
                    x_ring, x_sems, od_ring, od_sems, os_ring, os_sems, *, n_steps):
    tm = x1_ref.shape[0]
    step = pl.program_id(0)
    x_ref = _ring_fetch(x_hbm, x_ring, x_sems, step, n_steps)
    od_ref = _ring_fetch(od_hbm, od_ring, od_sems, step, n_steps)
    os_ref = _ring_fetch(os_hbm, os_ring, os_sems, step, n_steps)
    m = mod_ref[0]
    logit_rows = []
    for r0 in range(0, tm, TM_OUT_SUB):
        rows = slice(r0, r0 + TM_OUT_SUB)
        mix = (jnp.dot(od_ref[rows, :], wo_ref[0:512, :], preferred_element_type=F32)
               + jnp.dot(os_ref[rows, :], wo_ref[512:1024, :], preferred_element_type=F32))
        x1 = x_ref[rows, :] + m[2:3] * mix
        x1_ref[rows, :] = x1
        ms = jnp.mean(x1 * x1, axis=-1, keepdims=True)
        h2 = (x1 * lax.rsqrt(ms + EPS) * g_ref[...] * (1.0 + m[4:5]) + m[3:4]).astype(BF16)
        h2_ref[rows, 0:D] = h2
        logit_rows.append(jnp.dot(h2, wr_ref[...], preferred_element_type=F32) + br_ref[...])
    lt = jnp.concatenate(logit_rows, axis=0).T
    r = [lt[i:i + 1, :] for i in range(N_GROUPS + N_GROUPS * EPG)]

    gl = r[0:N_GROUPS]
    gmax = jnp.maximum(jnp.maximum(gl[0], gl[1]), jnp.maximum(gl[2], gl[3]))
    gidx = jnp.where(gl[0] == gmax, 0, jnp.where(gl[1] == gmax, 1, jnp.where(gl[2] == gmax, 2, 3)))
    gz = (jnp.exp(gl[0] - gmax) + jnp.exp(gl[1] - gmax)
          + jnp.exp(gl[2] - gmax) + jnp.exp(gl[3] - gmax))
    grp_p = 1.0 / gz

    el = []
    for i in range(EPG):
        e = r[N_GROUPS + 3 * EPG + i]
        for gg in (2, 1, 0):
            e = jnp.where(gidx == gg, r[N_GROUPS + gg * EPG + i], e)
        el.append(e)
    emax = jnp.maximum(jnp.maximum(el[0], el[1]), jnp.maximum(el[2], el[3]))
    ex = [jnp.exp(e - emax) for e in el]
    ez = ex[0] + ex[1] + ex[2] + ex[3]
    pr = [e / ez for e in ex]
    p1 = jnp.maximum(jnp.maximum(pr[0], pr[1]), jnp.maximum(pr[2], pr[3]))
    a = jnp.where(pr[0] == p1, 0, jnp.where(pr[1] == p1, 1, jnp.where(pr[2] == p1, 2, 3)))
    rest = [jnp.where(a == i, -1.0, pr[i]) for i in range(EPG)]
    p2 = jnp.maximum(jnp.maximum(rest[0], rest[1]), jnp.maximum(rest[2], rest[3]))
    b = jnp.where(rest[0] == p2, 0, jnp.where(rest[1] == p2, 1, jnp.where(rest[2] == p2, 2, 3)))
    psum = p1 + p2
    wa = grp_p * (p1 / psum)
    wb = grp_p * (p2 / psum)
    lo = jnp.minimum(a, b)
    hi = jnp.maximum(a, b)
    w_lo = jnp.where(a < b, wa, wb)
    w_hi = jnp.where(a < b, wb, wa)
    pair = jnp.where(lo == 0, hi - 1, jnp.where(lo == 1, hi + 1, 5))
    bucket = gidx * 6 + pair

    rid = lax.broadcasted_iota(I32, (AUG, tm), 0)
    aug = jnp.zeros((AUG, tm), F32)
    for k, piece in enumerate(_bf16_pieces(w_lo) + _bf16_pieces(w_hi)):
        aug = jnp.where(rid == k, piece, aug)
    h2_ref[:, D:D + AUG] = aug.T.astype(BF16)

    oh = (lax.broadcasted_iota(I32, (BUCKET_ROWS, tm), 0) == bucket)
    oh_f = jnp.where(oh, 1.0, 0.0)
    before = jnp.dot(oh_f.astype(BF16), tri_ref[...], preferred_element_type=F32)
    rank = jnp.sum(oh_f * before, axis=0, keepdims=True)
    cnt_ref[0] = jnp.broadcast_to(jnp.sum(oh_f, axis=1, keepdims=True), (BUCKET_ROWS, LANES))
    bucket_ref[0] = bucket
    rank_ref[0] = rank.astype(I32)


def _out_proj(o_diff, o_swa, x2, mod3, g_ffn, w_out_bf, w_r, b_r, tri, seq):
    t = x2.shape[0]
    tm = TM_PROJ
    per_b = seq // tm
    row = lambda i: (i, 0)
    const = lambda i: (0, 0)
    return pl.pallas_call(
        functools.partial(_outproj_kernel, n_steps=t // tm),
        grid=(t // tm,),
        in_specs=[pl.BlockSpec(memory_space=pl.ANY), pl.BlockSpec(memory_space=pl.ANY),
                  pl.BlockSpec(memory_space=pl.ANY),
                  pl.BlockSpec((1, 6, D), lambda i: (i // per_b, 0, 0)),
                  pl.BlockSpec((1, D), const),
                  pl.BlockSpec((D, D), const),
                  pl.BlockSpec((D, LANES), const),
                  pl.BlockSpec((1, LANES), const),
                  pl.BlockSpec((tm, tm), const)],
        out_specs=[pl.BlockSpec((tm, D), row),
                   pl.BlockSpec((tm, D + AUG), row),
                   pl.BlockSpec((1, 1, tm), lambda i: (i, 0, 0)),
                   pl.BlockSpec((1, 1, tm), lambda i: (i, 0, 0)),
                   pl.BlockSpec((1, BUCKET_ROWS, LANES), lambda i: (i, 0, 0))],
        out_shape=[jax.ShapeDtypeStruct((t, D), F32),
                   jax.ShapeDtypeStruct((t, D + AUG), BF16),
                   jax.ShapeDtypeStruct((t // tm, 1, tm), I32),
                   jax.ShapeDtypeStruct((t // tm, 1, tm), I32),
                   jax.ShapeDtypeStruct((t // tm, BUCKET_ROWS, LANES), F32)],
        scratch_shapes=[pltpu.VMEM((RING, tm, D), F32), pltpu.SemaphoreType.DMA((RING,)),
                        pltpu.VMEM((RING, tm, 512), BF16), pltpu.SemaphoreType.DMA((RING,)),
                        pltpu.VMEM((RING, tm, 512), BF16), pltpu.SemaphoreType.DMA((RING,))],
        compiler_params=_cparams(("arbitrary",)),
        name="out_proj",
    )(o_diff, o_swa, x2, mod3, g_ffn, w_out_bf, w_r, b_r, tri)


META_LOCAL, META_GLOBAL, META_ROWS, META_TILE_ROWS = 0, 1, 2, 3


def _local_pos(meta_ref, tile, bucket, rank):
    pos = rank
    for b in range(N_BUCKETS):
        pos = pos + jnp.where(bucket == b, meta_ref[META_LOCAL, tile * N_BUCKETS + b], 0)
    return pos


def _for_each_run(meta_ref, tile, fn):
    for b in range(N_BUCKETS):
        rows = meta_ref[META_ROWS, tile * N_BUCKETS + b]

        @pl.when(rows > 0)
        def _(b=b, rows=rows):
            fn(pl.multiple_of(meta_ref[META_LOCAL, tile * N_BUCKETS + b], SUBLANES),
               pl.multiple_of(meta_ref[META_GLOBAL, tile * N_BUCKETS + b], SUBLANES),
               pl.multiple_of(rows, SUBLANES))


def _dispatch_kernel(meta_ref, ztail_ref, nvalid_ref, h2_hbm, bucket_ref, rank_ref, xs_hbm,
                     sort_scr, zero_scr, sems, zsem, h2_ring, h2_sems, *, n_steps):
    i = pl.program_id(0)
    n_tiles = xs_hbm.shape[0] // TM_MOE
    tm = h2_ring.shape[1]
    slot = i % 2
    h2_ref = _ring_fetch(h2_hbm, h2_ring, h2_sems, i, n_steps)

    @pl.when(i == 0)
    def _():
        zero_scr[...] = jnp.zeros_like(zero_scr)

        def ztail_copy(b):
            rows = pl.multiple_of(ztail_ref[1, b], SUBLANES)
            return pltpu.make_async_copy(
                zero_scr.at[pl.ds(0, rows)],
                xs_hbm.at[pl.ds(pl.multiple_of(ztail_ref[0, b], SUBLANES), rows)], zsem)

        def ztile_copy(tile):
            return pltpu.make_async_copy(
                zero_scr, xs_hbm.at[pl.ds(pl.multiple_of(tile * TM_MOE, TM_MOE), TM_MOE)], zsem)

        for b in range(N_BUCKETS):
            @pl.when(ztail_ref[1, b] > 0)
            def _(b=b):
                ztail_copy(b).start()
        lax.fori_loop(nvalid_ref[0], n_tiles, lambda tile, c: (ztile_copy(tile).start(), c)[1], 0)
        for b in range(N_BUCKETS):
            @pl.when(ztail_ref[1, b] > 0)
            def _(b=b):
                ztail_copy(b).wait()
        lax.fori_loop(nvalid_ref[0], n_tiles, lambda tile, c: (ztile_copy(tile).wait(), c)[1], 0)

    def run_copy(buf, local_row, global_row, rows):
        return pltpu.make_async_copy(sort_scr.at[buf, pl.ds(local_row, rows)],
                                     xs_hbm.at[pl.ds(global_row, rows)], sems.at[buf])

    def wait_runs(tile, buf):
        run_copy(buf, 0, 0, pl.multiple_of(meta_ref[META_TILE_ROWS, tile], SUBLANES)).wait()

    @pl.when(i >= 2)
    def _():
        wait_runs(i - 2, slot)

    lpos = _local_pos(meta_ref, i, bucket_ref[0], rank_ref[0])
    lane = lax.broadcasted_iota(I32, (LS_SUB, AUG), 1)
    for r0 in range(0, LS, LS_SUB):
        row_id = lax.broadcasted_iota(I32, (LS_SUB, tm), 0) + r0
        onehot = jnp.where(row_id == lpos, 1.0, 0.0).astype(BF16)
        srt = jnp.dot(onehot, h2_ref[...], preferred_element_type=F32)
        aug = srt[:, D:D + AUG]
        w_lo = aug[:, 0:1] + aug[:, 1:2] + aug[:, 2:3]
        w_hi = aug[:, 3:4] + aug[:, 4:5] + aug[:, 5:6]
        sort_scr[slot, r0:r0 + LS_SUB, 0:D] = srt[:, 0:D]
        sort_scr[slot, r0:r0 + LS_SUB, D:XW] = jnp.where(lane == 0, w_lo, jnp.where(lane == 1, w_hi, 0.0))
    _for_each_run(meta_ref, i, lambda lr, gr, rows: run_copy(slot, lr, gr, rows).start())

    @pl.when(i == n_steps - 1)
    def _():
        @pl.when(i >= 1)
        def _():
            wait_runs(i - 1, 1 - slot)
        wait_runs(i, slot)


def _dispatch(meta, ztail, n_valid, h2a, bucket, rank, n_rows):
    t, width = h2a.shape
    tm = TM_PROJ
    grid_spec = pltpu.PrefetchScalarGridSpec(
        num_scalar_prefetch=3,
        grid=(t // tm,),
        in_specs=[pl.BlockSpec(memory_space=pl.ANY),
                  pl.BlockSpec((1, 1, tm), lambda i, m, z, nv: (i, 0, 0)),
                  pl.BlockSpec((1, 1, tm), lambda i, m, z, nv: (i, 0, 0))],
        out_specs=pl.BlockSpec(memory_space=pl.ANY),
        scratch_shapes=[pltpu.VMEM((2, LS, XW), F32), pltpu.VMEM((TM_MOE, XW), F32),
                        pltpu.SemaphoreType.DMA((2,)), pltpu.SemaphoreType.DMA(()),
                        pltpu.VMEM((RING, tm, width), BF16), pltpu.SemaphoreType.DMA((RING,))],
    )
    return pl.pallas_call(
        functools.partial(_dispatch_kernel, n_steps=t // tm),
        grid_spec=grid_spec,
        out_shape=jax.ShapeDtypeStruct((n_rows, XW), F32),
        compiler_params=_cparams(("arbitrary",)),
        name="dispatch",
    )(meta, ztail, n_valid, h2a, bucket, rank)


def _moe_kernel(lo_ref, hi_ref, nvalid_ref, xs_ref, wg_lo, wu_lo, wd_lo, wg_hi, wu_hi, wd_hi, y_ref):
    del lo_ref, hi_ref
    t = pl.program_id(0)

    @pl.when(t < nvalid_ref[0])
    def _():
        xb = xs_ref[:, 0:D].astype(BF16)
        y = None
        for k, (wg, wu, wd) in enumerate(((wg_lo, wu_lo, wd_lo), (wg_hi, wu_hi, wd_hi))):
            gt = jnp.dot(xb, wg[0], preferred_element_type=F32)
            up = jnp.dot(xb, wu[0], preferred_element_type=F32)
            act = (gt * (1.0 / (1.0 + jnp.exp(-gt))) * up).astype(BF16)
            dn = jnp.dot(act, wd[0], preferred_element_type=F32)
            term = xs_ref[:, D + k:D + k + 1] * dn
            y = term if y is None else y + term
        y_ref[...] = y

    @pl.when(t >= nvalid_ref[0])
    def _():
        y_ref[...] = jnp.zeros_like(y_ref)


def _moe(tile_lo, tile_hi, n_valid, xs, wg, wu, wd):
    n_rows, width = xs.shape
    n_tiles = n_rows // TM_MOE
    up_lo = lambda t, lo, hi, nv: (lo[t], 0, 0)
    up_hi = lambda t, lo, hi, nv: (hi[t], 0, 0)
    grid_spec = pltpu.PrefetchScalarGridSpec(
        num_scalar_prefetch=3,
        grid=(n_tiles,),
        in_specs=[pl.BlockSpec((TM_MOE, width), lambda t, lo, hi, nv: (jnp.minimum(t, nv[0] - 1), 0)),
                  pl.BlockSpec((1, D, D_EXPERT), up_lo), pl.BlockSpec((1, D, D_EXPERT), up_lo),
                  pl.BlockSpec((1, D_EXPERT, D), up_lo),
                  pl.BlockSpec((1, D, D_EXPERT), up_hi), pl.BlockSpec((1, D, D_EXPERT), up_hi),
                  pl.BlockSpec((1, D_EXPERT, D), up_hi)],
        out_specs=pl.BlockSpec((TM_MOE, D), lambda t, lo, hi, nv: (t, 0)),
    )
    return pl.pallas_call(
        _moe_kernel,
        grid_spec=grid_spec,
        out_shape=jax.ShapeDtypeStruct((n_rows, D), F32),
        compiler_params=_cparams(("arbitrary",)),
        name="moe",
    )(tile_lo, tile_hi, n_valid, xs, wg, wu, wd, wg, wu, wd)


def _final_kernel(meta_ref, ys_hbm, x1_hbm, mod_ref, g_ref, bucket_ref, rank_ref, o_ref, y_scr, sems,
                  x1_ring, x1_sems, *, n_steps):
    i = pl.program_id(0)
    tm = o_ref.shape[0]
    slot = i % RING
    x1_ref = _ring_fetch(x1_hbm, x1_ring, x1_sems, i, n_steps)

    def run_copy(buf, local_row, global_row, rows):
        return pltpu.make_async_copy(ys_hbm.at[pl.ds(global_row, rows)],
                                     y_scr.at[buf, pl.ds(local_row, rows)], sems.at[buf])

    def fetch_runs(tile):
        buf = tile % RING
        _for_each_run(meta_ref, tile, lambda lr, gr, rows: run_copy(buf, lr, gr, rows).start())

    @pl.when(i == 0)
    def _():
        y_scr[...] = jnp.zeros_like(y_scr)
        for s in range(min(RING - 1, n_steps)):
            fetch_runs(s)

    @pl.when(i + RING - 1 < n_steps)
    def _():
        fetch_runs(i + RING - 1)

    run_copy(slot, 0, 0, pl.multiple_of(meta_ref[META_TILE_ROWS, i], SUBLANES)).wait()

    lpos = _local_pos(meta_ref, i, bucket_ref[0], rank_ref[0])
    lpos_col = jnp.broadcast_to(lpos.astype(F32), (LANES, tm)).T[:, 0:1].astype(I32)
    ysrt = y_scr[slot].astype(BF16)
    m = mod_ref[0]
    for r0 in range(0, tm, TM_OUT_SUB):
        rows = slice(r0, r0 + TM_OUT_SUB)
        onehot = jnp.where(lax.broadcasted_iota(I32, (TM_OUT_SUB, LS), 1) == lpos_col[rows], 1.0, 0.0)
        y = jnp.dot(onehot.astype(BF16), ysrt, preferred_element_type=F32)
        x2 = x1_ref[rows, :] + m[5:6] * y
        ms = jnp.mean(x2 * x2, axis=-1, keepdims=True)
        o_ref[rows, :] = x2 * lax.rsqrt(ms + EPS) * g_ref[...]


def _final(meta, ys, x1, mod3, g_final, bucket, rank, seq):
    t = x1.shape[0]
    tm = TM_PROJ
    per_b = seq // tm
    grid_spec = pltpu.PrefetchScalarGridSpec(
        num_scalar_prefetch=1,
        grid=(t // tm,),
        in_specs=[pl.BlockSpec(memory_space=pl.ANY),
                  pl.BlockSpec(memory_space=pl.ANY),
                  pl.BlockSpec((1, 6, D), lambda i, m: (i // per_b, 0, 0)),
                  pl.BlockSpec((1, D), lambda i, m: (0, 0)),
                  pl.BlockSpec((1, 1, tm), lambda i, m: (i, 0, 0)),
                  pl.BlockSpec((1, 1, tm), lambda i, m: (i, 0, 0))],
        out_specs=pl.BlockSpec((tm, D), lambda i, m: (i, 0)),
        scratch_shapes=[pltpu.VMEM((RING, LS, D), F32), pltpu.SemaphoreType.DMA((RING,)),
                        pltpu.VMEM((RING, tm, D), F32), pltpu.SemaphoreType.DMA((RING,))],
    )
    return pl.pallas_call(
        functools.partial(_final_kernel, n_steps=t // tm),
        grid_spec=grid_spec,
        out_shape=jax.ShapeDtypeStruct((t, D), F32),
        compiler_params=_cparams(("arbitrary",)),
        name="final",
    )(meta, ys, x1, mod3, g_final, bucket, rank)


def _rope_tables(seq):
    inv = ROPE_THETA ** (-jnp.arange(0, HD, 2, dtype=F32) / HD)
    ang = jnp.arange(seq, dtype=F32)[:, None] * inv[None, :]
    cos, sin = jnp.cos(ang), jnp.sin(ang)
    return jnp.tile(cos, (1, 4)), jnp.concatenate([-sin, -sin, sin, sin], axis=-1)


def _rotary_column_order():
    half = HD // 2
    lane = np.arange(LANES)
    which, part, f = (lane // half) % 2, lane // HD, lane % half
    cols = []
    for base in (0, 512):
        for h in range(4):
            cols.append(base + h * LANES + which * HD + part * half + f)
    cols.append(np.arange(1024, 1536))
    for c in range(4):
        cols.append(1536 + (c + 4 * which) * HD + part * half + f)
    cols.append(2048 + which * HD + part * half + f)
    cols.append(np.arange(2176, 2304))
    return np.concatenate(cols)


def kernel(x, c, w_ada, b_ada, g_mix, w_in, diff_lambda, g_diff_sub, swa_sinks, w_out, g_ffn,
           w_route_group, b_route_group, w_route_expert, b_route_expert, w_gate, w_up, w_down,
           g_final):
    bsz, seq, _ = x.shape
    t = bsz * seq
    x2 = x.reshape(t, D)
    cos, sin = _rope_tables(seq)

    mod3 = _ada_mod(c, w_ada[0], b_ada[0].reshape(1, -1)).reshape(bsz, 6, D)

    w_in_bf = w_in[0][:, _rotary_column_order()].astype(BF16)
    dq, dk, dv, sq, sk, sv = _in_proj(x2, mod3, g_mix[0].reshape(1, D), w_in_bf, cos, sin, seq)
    n_exp = w_gate.shape[1]
    o_diff, wg_bf, wu_bf, wd_bf = _diff_attn(
        dq, dk, dv, diff_lambda[0], g_diff_sub[0].reshape(1, LANES), bsz, seq,
        (w_gate[0].reshape(n_exp * D, D_EXPERT), w_up[0].reshape(n_exp * D, D_EXPERT),
         w_down[0].reshape(n_exp * D_EXPERT, D)))
    o_swa = _swa_attn(sq, sk, sv, swa_sinks[0].reshape(1, 8), bsz, seq)

    n_r = N_GROUPS + N_GROUPS * EPG
    w_r = jnp.concatenate([w_route_group[0], w_route_expert[0],
                           jnp.zeros((D, LANES - n_r), F32)], axis=1).astype(BF16)
    b_r = jnp.concatenate([b_route_group[0], b_route_expert[0],
                           jnp.zeros((LANES - n_r,), F32)]).reshape(1, LANES)
    tri = (jnp.arange(TM_PROJ)[:, None] < jnp.arange(TM_PROJ)[None, :]).astype(BF16)
    x1, h2a, bucket, rank, counts = _out_proj(o_diff, o_swa, x2, mod3, g_ffn[0].reshape(1, D),
                                              w_out[0].astype(BF16), w_r, b_r, tri, seq)

    n_tok_tiles = t // TM_PROJ
    n_tiles = -(-(t + n_tok_tiles * N_BUCKETS * (SUBLANES - 1)) // TM_MOE) + N_BUCKETS
    cnt = counts[:, :N_BUCKETS, 0].astype(I32)
    run_rows = (cnt + SUBLANES - 1) // SUBLANES * SUBLANES
    local_off = jnp.cumsum(run_rows, axis=1) - run_rows
    tiles_per = (jnp.sum(run_rows, axis=0) + TM_MOE - 1) // TM_MOE
    tile_end = jnp.cumsum(tiles_per)
    bucket_start = (tile_end - tiles_per) * TM_MOE
    global_off = bucket_start[None, :] + jnp.cumsum(run_rows, axis=0) - run_rows
    tile_rows = jnp.pad(jnp.sum(run_rows, axis=1), (0, n_tok_tiles * (N_BUCKETS - 1)))
    meta = jnp.stack([local_off.reshape(-1), global_off.reshape(-1), run_rows.reshape(-1), tile_rows])
    tid = jnp.arange(n_tiles, dtype=I32)
    tile_bucket = jnp.minimum(jnp.sum(tid[:, None] >= tile_end[None, :], axis=1), N_BUCKETS - 1)
    n_valid = tile_end[-1:].astype(I32)
    bucket_rows = jnp.sum(run_rows, axis=0)
    ztail = jnp.stack([bucket_start + bucket_rows, tiles_per * TM_MOE - bucket_rows]).astype(I32)
    grp, pair = tile_bucket // 6, tile_bucket % 6
    tile_lo = (grp * EPG + jnp.asarray(PAIR_LO, I32)[pair]).astype(I32)
    tile_hi = (grp * EPG + jnp.asarray(PAIR_HI, I32)[pair]).astype(I32)

    xs = _dispatch(meta, ztail, n_valid, h2a, bucket, rank, n_tiles * TM_MOE)
    ys = _moe(tile_lo, tile_hi, n_valid, xs, wg_bf.reshape(n_exp, D, D_EXPERT),
              wu_bf.reshape(n_exp, D, D_EXPERT), wd_bf.reshape(n_exp, D_EXPERT, D))
    out = _final(meta, ys, x1, mod3, g_final.reshape(1, D), bucket, rank, seq)
    return out.reshape(bsz, seq, D)
```

```python
import functools
import math

import jax
import jax.numpy as jnp
import numpy as np
from jax import lax
from jax.experimental import pallas as pl
from jax.experimental.pallas import tpu as pltpu

F32 = jnp.float32
BF16 = jnp.bfloat16
I32 = jnp.int32

D = 1024
HD = 64
EPS = 1e-6
ROPE_THETA = 10000.0
LOG2E = math.log2(math.e)
LAMBDA_INIT = 0.8 - 0.6 * math.exp(-0.3 * 0)
N_GROUPS = 4
EPG = 4
N_BUCKETS = 24
BUCKET_ROWS = 32
D_EXPERT = 512
AUG = 128

LANES = 128
VMEM_LIMIT = 48 * 1024 * 1024

TM_IN = 1024
TM_IN_SUB = 256
TM_PROJ = 512
TM_OUT_SUB = 256
TQ = 512
BLK = 128
TM_MOE = 512
SUBLANES = 8
RUN_ROWS = TM_PROJ + N_BUCKETS * (SUBLANES - 1)
LS = -(-RUN_ROWS // 64) * 64
LS_SUB = LS // 2
XW = D + AUG

PAIR_LO = (0, 0, 0, 1, 1, 2)
PAIR_HI = (1, 2, 3, 2, 3, 3)


def _cparams(sem):
    return pltpu.CompilerParams(dimension_semantics=sem, vmem_limit_bytes=VMEM_LIMIT)


RING = 3


def _ring_fetch(src_hbm, ring, sems, step, n_steps):
    rows = ring.shape[1]

    def copy(s):
        return pltpu.make_async_copy(src_hbm.at[pl.ds(pl.multiple_of(s * rows, rows), rows)],
                                     ring.at[s % RING], sems.at[s % RING])

    @pl.when(step == 0)
    def _():
        for s in range(min(RING - 1, n_steps)):
            copy(s).start()

    @pl.when(step + RING - 1 < n_steps)
    def _():
        copy(step + RING - 1).start()

    copy(step).wait()
    return ring.at[step % RING]


def _ada_kernel(c_ref, w_ref, b_ref, o_ref):
    c = c_ref[...]
    ca = c * (1.0 / (1.0 + jnp.exp(-c)))
    o_ref[...] = jnp.dot(ca.astype(BF16), w_ref[...].astype(BF16),
                         preferred_element_type=F32) + b_ref[...]


def _ada_mod(c, w, b):
    bsz = c.shape[0]
    n = w.shape[1]
    tn = 512
    return pl.pallas_call(
        _ada_kernel,
        grid=(n // tn,),
        in_specs=[pl.BlockSpec((bsz, D), lambda j: (0, 0)),
                  pl.BlockSpec((D, tn), lambda j: (0, j)),
                  pl.BlockSpec((1, tn), lambda j: (0, j))],
        out_specs=pl.BlockSpec((bsz, tn), lambda j: (0, j)),
        out_shape=jax.ShapeDtypeStruct((bsz, n), F32),
        compiler_params=_cparams(("arbitrary",)),
        name="ada_mod",
    )(c, w, b)


def _rope(a, cos, sin):
    return a * cos + pltpu.roll(a, HD, 1) * sin


def _inproj_kernel(x_ref, mod_ref, g_ref, w_ref, cos_ref, sin_ref,
                   dq_ref, dk_ref, dv_ref, sq_ref, sk_ref, sv_ref, h_scr):
    tm = x_ref.shape[0]
    sub = TM_IN_SUB
    m = mod_ref[0]
    lane = lax.broadcasted_iota(I32, (sub, LANES), 1)
    head_a = (lane & 32) == 0
    low64 = lane < HD
    scale = (HD ** -0.5) * LOG2E

    def halves(a):
        return a[:, :LANES], a[:, LANES:]

    for r0 in range(0, tm, sub):
        rows = slice(r0, r0 + sub)
        x = x_ref[rows, :]
        ms = jnp.mean(x * x, axis=-1, keepdims=True)
        y = x * lax.rsqrt(ms + EPS) * g_ref[...]
        h_scr[rows, :] = (y * (1.0 + m[1:2]) + m[0:1]).astype(BF16)
        cos = cos_ref[rows, :]
        sin = sin_ref[rows, :]

        def chunk(j):
            return jnp.dot(h_scr[rows, :], w_ref[:, 256 * j:256 * (j + 1)], preferred_element_type=F32)

        for j in range(2):
            for t, r in enumerate(halves(chunk(j))):
                c0 = 256 * j + LANES * t
                dq_ref[rows, c0:c0 + LANES] = (_rope(r, cos, sin) * scale).astype(BF16)
        for j in range(2):
            for t, r in enumerate(halves(chunk(2 + j))):
                c0 = 256 * j + LANES * t
                dk_ref[rows, c0:c0 + LANES] = _rope(r, cos, sin).astype(BF16)
        for j in range(2):
            dv_ref[rows, 256 * j:256 * (j + 1)] = chunk(4 + j).astype(BF16)
        for j in range(2):
            for t, r in enumerate(halves(chunk(6 + j))):
                rp = _rope(r, cos, sin) * scale
                c = 2 * j + t
                sq_ref[rows, LANES * c:LANES * (c + 1)] = jnp.where(head_a, rp, 0.0).astype(BF16)
                sq_ref[rows, LANES * (c + 4):LANES * (c + 5)] = jnp.where(head_a, 0.0, rp).astype(BF16)
        kv = chunk(8)
        kk, vv = halves(kv)
        kr = _rope(kk, cos, sin)
        sk_ref[rows, :LANES] = jnp.where(head_a, kr, 0.0).astype(BF16)
        sk_ref[rows, LANES:] = jnp.where(head_a, 0.0, kr).astype(BF16)
        sv_ref[rows, :LANES] = jnp.where(low64, vv, 0.0).astype(BF16)
        sv_ref[rows, LANES:] = jnp.where(low64, pltpu.roll(vv, HD, 1), 0.0).astype(BF16)


def _in_proj(x2, mod3, g_mix, w_in_bf, cos, sin, seq):
    t = x2.shape[0]
    tm = TM_IN
    per_b = seq // tm
    n_in = w_in_bf.shape[1]
    row = lambda i: (i, 0)
    return pl.pallas_call(
        _inproj_kernel,
        grid=(t // tm,),
        in_specs=[pl.BlockSpec((tm, D), row),
                  pl.BlockSpec((1, 6, D), lambda i: (i // per_b, 0, 0)),
                  pl.BlockSpec((1, D), lambda i: (0, 0)),
                  pl.BlockSpec((D, n_in), lambda i: (0, 0)),
                  pl.BlockSpec((tm, LANES), lambda i: (i % per_b, 0)),
                  pl.BlockSpec((tm, LANES), lambda i: (i % per_b, 0))],
        out_specs=[pl.BlockSpec((tm, 512), row), pl.BlockSpec((tm, 512), row),
                   pl.BlockSpec((tm, 512), row), pl.BlockSpec((tm, 1024), row),
                   pl.BlockSpec((tm, 256), row), pl.BlockSpec((tm, 256), row)],
        out_shape=[jax.ShapeDtypeStruct((t, 512), BF16), jax.ShapeDtypeStruct((t, 512), BF16),
                   jax.ShapeDtypeStruct((t, 512), BF16), jax.ShapeDtypeStruct((t, 1024), BF16),
                   jax.ShapeDtypeStruct((t, 256), BF16), jax.ShapeDtypeStruct((t, 256), BF16)],
        scratch_shapes=[pltpu.VMEM((tm, D), BF16)],
        compiler_params=_cparams(("arbitrary",)),
        name="in_proj",
    )(x2, mod3, g_mix, w_in_bf, cos, sin)


_NT_DIMS = (((1,), (1,)), ((), ()))


def _diff_kernel(lam_ref, g_ref, q_ref, k_ref, v_ref, *rest):
    n_cast = (len(rest) - 1) // 2
    o_ref = rest[n_cast]
    for src, dst in zip(rest[:n_cast], rest[n_cast + 1:]):
        dst[...] = src[...].astype(BF16)

    seq = q_ref.shape[0]
    tq = TQ
    lp = lam_ref[...]
    lam = (jnp.exp(jnp.sum(lp[0:1] * lp[1:2], axis=-1, keepdims=True))
           - jnp.exp(jnp.sum(lp[2:3] * lp[3:4], axis=-1, keepdims=True)) + LAMBDA_INIT)
    lane = lax.broadcasted_iota(I32, (tq, LANES), 1)
    map_a = (lane & 32) == 0
    causal = (lax.broadcasted_iota(I32, (tq, tq), 1) <= lax.broadcasted_iota(I32, (tq, tq), 0))
    gain = g_ref[...] * (1.0 - LAMBDA_INIT)

    for i in reversed(range(seq // tq)):
        lo, hi = i * tq, (i + 1) * tq
        q = q_ref[lo:hi, :]
        zero = jnp.zeros_like(q)
        probs = []
        for qm in (jnp.where(map_a, q, zero), jnp.where(map_a, zero, q)):
            s_dg = lax.dot_general(qm, k_ref[lo:hi, :], _NT_DIMS, preferred_element_type=F32)
            s_dg = jnp.where(causal, s_dg, -jnp.inf)
            mx = jnp.max(s_dg, axis=-1, keepdims=True)
            if i > 0:
                s_off = lax.dot_general(qm, k_ref[0:lo, :], _NT_DIMS, preferred_element_type=F32)
                mx = jnp.maximum(mx, jnp.max(s_off, axis=-1, keepdims=True))
                p_off = jnp.exp2(s_off - mx)
            p_dg = jnp.exp2(s_dg - mx)
            l = jnp.sum(p_dg, axis=-1, keepdims=True)
            if i > 0:
                l = l + jnp.sum(p_off, axis=-1, keepdims=True)
                probs.append((p_off, p_dg, l))
            else:
                probs.append((None, p_dg, l))
        (p0_off, p0_dg, l0), (p1_off, p1_dg, l1) = probs
        c = lam * l0 / l1
        o = jnp.dot((p0_dg - c * p1_dg).astype(BF16), v_ref[lo:hi, :], preferred_element_type=F32)
        if i > 0:
            o = o + jnp.dot((p0_off - c * p1_off).astype(BF16), v_ref[0:lo, :],
                            preferred_element_type=F32)
        o = o / l0
        ms = jnp.mean(o * o, axis=-1, keepdims=True)
        o_ref[lo:hi, :] = (o * lax.rsqrt(ms + EPS) * gain).astype(BF16)


def _diff_attn(dq, dk, dv, lam_p, g_sub, bsz, seq, to_bf16):
    t = dq.shape[0]
    heads = 4
    blk = lambda b, h: (b, h)
    slab = lambda b, h: (b * heads + h, 0)
    slab_specs = [pl.BlockSpec((w.shape[0] // (bsz * heads), w.shape[1]), slab) for w in to_bf16]
    return pl.pallas_call(
        _diff_kernel,
        grid=(bsz, heads),
        in_specs=[pl.BlockSpec((4, HD), lambda b, h: (0, 0)),
                  pl.BlockSpec((1, LANES), lambda b, h: (0, 0)),
                  pl.BlockSpec((seq, LANES), blk),
                  pl.BlockSpec((seq, LANES), blk),
                  pl.BlockSpec((seq, LANES), blk)] + slab_specs,
        out_specs=[pl.BlockSpec((seq, LANES), blk)] + slab_specs,
        out_shape=[jax.ShapeDtypeStruct((t, 512), BF16)]
        + [jax.ShapeDtypeStruct(w.shape, BF16) for w in to_bf16],
        compiler_params=_cparams(("arbitrary", "arbitrary")),
        name="diff_attn",
    )(lam_p, g_sub, dq, dk, dv, *to_bf16)


def _swa_kernel(sink_ref, q_ref, k_ref, v_ref, o_ref):
    seq = q_ref.shape[0]
    g = pl.program_id(1)
    rows = 4 * BLK
    hsel = jnp.right_shift(lax.broadcasted_iota(I32, (rows, 8), 0), 7) + 4 * g
    sink_col = jnp.sum(jnp.where(lax.broadcasted_iota(I32, (rows, 8), 1) == hsel,
                                 sink_ref[...] * LOG2E, 0.0), axis=-1, keepdims=True)
    qi = lax.broadcasted_iota(I32, (rows, 2 * BLK), 0) & (BLK - 1)
    kj = lax.broadcasted_iota(I32, (rows, 2 * BLK), 1)
    dist = qi + BLK - kj
    band = (dist >= 0) & (dist < BLK)
    first = (lax.broadcasted_iota(I32, (rows, BLK), 1)
             <= (lax.broadcasted_iota(I32, (rows, BLK), 0) & (BLK - 1)))
    low64 = lax.broadcasted_iota(I32, (BLK, LANES), 1) < HD

    def attend(q_rows, k_rows, mask):
        qs = jnp.concatenate([q_ref[q_rows, LANES * j:LANES * (j + 1)] for j in range(4)], axis=0)
        s = lax.dot_general(qs, k_ref[k_rows, :], _NT_DIMS, preferred_element_type=F32)
        s = jnp.where(mask, s, -jnp.inf)
        mx = jnp.maximum(jnp.max(s, axis=-1, keepdims=True), sink_col)
        p = jnp.exp2(s - mx)
        l = jnp.sum(p, axis=-1, keepdims=True) + jnp.exp2(sink_col - mx)
        o = jnp.dot(p.astype(BF16), v_ref[k_rows, :], preferred_element_type=F32) / l
        for c in range(2):
            even = o[(2 * c) * BLK:(2 * c + 1) * BLK, :]
            odd = o[(2 * c + 1) * BLK:(2 * c + 2) * BLK, :]
            o_ref[q_rows, LANES * c:LANES * (c + 1)] = jnp.where(
                low64, even, pltpu.roll(odd, HD, 1)).astype(BF16)

    attend(pl.ds(0, BLK), pl.ds(0, BLK), first)

    for n in range(1, seq // BLK):
        attend(pl.ds(n * BLK, BLK), pl.ds((n - 1) * BLK, 2 * BLK), band)


def _swa_attn(sq, sk, sv, sinks, bsz, seq):
    t = sq.shape[0]
    return pl.pallas_call(
        _swa_kernel,
        grid=(bsz, 2),
        in_specs=[pl.BlockSpec((1, 8), lambda b, g: (0, 0)),
                  pl.BlockSpec((seq, 512), lambda b, g: (b, g)),
                  pl.BlockSpec((seq, LANES), lambda b, g: (b, g)),
                  pl.BlockSpec((seq, LANES), lambda b, g: (b, g))],
        out_specs=pl.BlockSpec((seq, 256), lambda b, g: (b, g)),
        out_shape=jax.ShapeDtypeStruct((t, 512), BF16),
        compiler_params=_cparams(("arbitrary", "arbitrary")),
        name="swa_attn",
    )(sinks, sq, sk, sv)


def _bf16_pieces(w):
    p0 = w.astype(BF16).astype(F32)
    r1 = w - p0
    p1 = r1.astype(BF16).astype(F32)
    return p0, p1, r1 - p1


def _outproj_kernel(od_hbm, os_hbm, x_hbm, mod_ref, g_ref, wo_ref, wr_ref, br_ref, tri_ref,
                    x1_ref, h2_ref, bucket_ref, rank_ref, cnt_ref,
                    x_ring, x_sems, od_ring, od_sems, os_ring, os_sems, *, n_steps):
    tm = x1_ref.shape[0]
    step = pl.program_id(0)
    x_ref = _ring_fetch(x_hbm, x_ring, x_sems, step, n_steps)
    od_ref = _ring_fetch(od_hbm, od_ring, od_sems, step, n_steps)
    os_ref = _ring_fetch(os_hbm, os_ring, os_sems, step, n_steps)
    m = mod_ref[0]
    logit_rows = []
    for r0 in range(0, tm, TM_OUT_SUB):
        rows = slice(r0, r0 + TM_OUT_SUB)
        mix = (jnp.dot(od_ref[rows, :], wo_ref[0:512, :], preferred_element_type=F32)
               + jnp.dot(os_ref[rows, :], wo_ref[512:1024, :], preferred_element_type=F32))
        x1 = x_ref[rows, :] + m[2:3] * mix
        x1_ref[rows, :] = x1
        ms = jnp.mean(x1 * x1, axis=-1, keepdims=True)
        h2 = (x1 * lax.rsqrt(ms + EPS) * g_ref[...] * (1.0 + m[4:5]) + m[3:4]).astype(BF16)
        h2_ref[rows, 0:D] = h2
        logit_rows.append(jnp.dot(h2, wr_ref[...], preferred_element_type=F32) + br_ref[...])
    lt = jnp.concatenate(logit_rows, axis=0).T
    r = [lt[i:i + 1, :] for i in range(N_GROUPS + N_GROUPS * EPG)]

    gl = r[0:N_GROUPS]
    gmax = jnp.maximum(jnp.maximum(gl[0], gl[1]), jnp.maximum(gl[2], gl[3]))
    gidx = jnp.where(gl[0] == gmax, 0, jnp.where(gl[1] == gmax, 1, jnp.where(gl[2] == gmax, 2, 3)))
    gz = (jnp.exp(gl[0] - gmax) + jnp.exp(gl[1] - gmax)
          + jnp.exp(gl[2] - gmax) + jnp.exp(gl[3] - gmax))
    grp_p = 1.0 / gz

    el = []
    for i in range(EPG):
        e = r[N_GROUPS + 3 * EPG + i]
        for gg in (2, 1, 0):
            e = jnp.where(gidx == gg, r[N_GROUPS + gg * EPG + i], e)
        el.append(e)
    emax = jnp.maximum(jnp.maximum(el[0], el[1]), jnp.maximum(el[2], el[3]))
    ex = [jnp.exp(e - emax) for e in el]
    ez = ex[0] + ex[1] + ex[2] + ex[3]
    pr = [e / ez for e in ex]
    p1 = jnp.maximum(jnp.maximum(pr[0], pr[1]), jnp.maximum(pr[2], pr[3]))
    a = jnp.where(pr[0] == p1, 0, jnp.where(pr[1] == p1, 1, jnp.where(pr[2] == p1, 2, 3)))
    rest = [jnp.where(a == i, -1.0, pr[i]) for i in range(EPG)]
    p2 = jnp.maximum(jnp.maximum(rest[0], rest[1]), jnp.maximum(rest[2], rest[3]))
    b = jnp.where(rest[0] == p2, 0, jnp.where(rest[1] == p2, 1, jnp.where(rest[2] == p2, 2, 3)))
    psum = p1 + p2
    wa = grp_p * (p1 / psum)
    wb = grp_p * (p2 / psum)
    lo = jnp.minimum(a, b)
    hi = jnp.maximum(a, b)
    w_lo = jnp.where(a < b, wa, wb)
    w_hi = jnp.where(a < b, wb, wa)
    pair = jnp.where(lo == 0, hi - 1, jnp.where(lo == 1, hi + 1, 5))
    bucket = gidx * 6 + pair

    rid = lax.broadcasted_iota(I32, (AUG, tm), 0)
    aug = jnp.zeros((AUG, tm), F32)
    for k, piece in enumerate(_bf16_pieces(w_lo) + _bf16_pieces(w_hi)):
        aug = jnp.where(rid == k, piece, aug)
    h2_ref[:, D:D + AUG] = aug.T.astype(BF16)

    oh = (lax.broadcasted_iota(I32, (BUCKET_ROWS, tm), 0) == bucket)
    oh_f = jnp.where(oh, 1.0, 0.0)
    before = jnp.dot(oh_f.astype(BF16), tri_ref[...], preferred_element_type=F32)
    rank = jnp.sum(oh_f * before, axis=0, keepdims=True)
    cnt_ref[0] = jnp.broadcast_to(jnp.sum(oh_f, axis=1, keepdims=True), (BUCKET_ROWS, LANES))
    bucket_ref[0] = bucket
    rank_ref[0] = rank.astype(I32)


def _out_proj(o_diff, o_swa, x2, mod3, g_ffn, w_out_bf, w_r, b_r, tri, seq):
    t = x2.shape[0]
    tm = TM_PROJ
    per_b = seq // tm
    row = lambda i: (i, 0)
    const = lambda i: (0, 0)
    return pl.pallas_call(
        functools.partial(_outproj_kernel, n_steps=t // tm),
        grid=(t // tm,),
        in_specs=[pl.BlockSpec(memory_space=pl.ANY), pl.BlockSpec(memory_space=pl.ANY),
                  pl.BlockSpec(memory_space=pl.ANY),
                  pl.BlockSpec((1, 6, D), lambda i: (i // per_b, 0, 0)),
                  pl.BlockSpec((1, D), const),
                  pl.BlockSpec((D, D), const),
                  pl.BlockSpec((D, LANES), const),
                  pl.BlockSpec((1, LANES), const),
                  pl.BlockSpec((tm, tm), const)],
        out_specs=[pl.BlockSpec((tm, D), row),
                   pl.BlockSpec((tm, D + AUG), row),
                   pl.BlockSpec((1, 1, tm), lambda i: (i, 0, 0)),
                   pl.BlockSpec((1, 1, tm), lambda i: (i, 0, 0)),
                   pl.BlockSpec((1, BUCKET_ROWS, LANES), lambda i: (i, 0, 0))],
        out_shape=[jax.ShapeDtypeStruct((t, D), F32),
                   jax.ShapeDtypeStruct((t, D + AUG), BF16),
                   jax.ShapeDtypeStruct((t // tm, 1, tm), I32),
                   jax.ShapeDtypeStruct((t // tm, 1, tm), I32),
                   jax.ShapeDtypeStruct((t // tm, BUCKET_ROWS, LANES), F32)],
        scratch_shapes=[pltpu.VMEM((RING, tm, D), F32), pltpu.SemaphoreType.DMA((RING,)),
                        pltpu.VMEM((RING, tm, 512), BF16), pltpu.SemaphoreType.DMA((RING,)),
                        pltpu.VMEM((RING, tm, 512), BF16), pltpu.SemaphoreType.DMA((RING,))],
        compiler_params=_cparams(("arbitrary",)),
        name="out_proj",
    )(o_diff, o_swa, x2, mod3, g_ffn, w_out_bf, w_r, b_r, tri)


META_LOCAL, META_GLOBAL, META_ROWS, META_TILE_ROWS = 0, 1, 2, 3


def _local_pos(meta_ref, tile, bucket, rank):
    pos = rank
    for b in range(N_BUCKETS):
        pos = pos + jnp.where(bucket == b, meta_ref[META_LOCAL, tile * N_BUCKETS + b], 0)
    return pos


def _for_each_run(meta_ref, tile, fn):
    for b in range(N_BUCKETS):
        rows = meta_ref[META_ROWS, tile * N_BUCKETS + b]

        @pl.when(rows > 0)
        def _(b=b, rows=rows):
            fn(pl.multiple_of(meta_ref[META_LOCAL, tile * N_BUCKETS + b], SUBLANES),
               pl.multiple_of(meta_ref[META_GLOBAL, tile * N_BUCKETS + b], SUBLANES),
               pl.multiple_of(rows, SUBLANES), b)


def _dispatch_kernel(meta_ref, ztail_ref, nvalid_ref, h2_hbm, bucket_ref, rank_ref, xs_hbm,
                     sort_scr, zero_scr, sems, zsem, h2_ring, h2_sems, *, n_steps):
    i = pl.program_id(0)
    n_tiles = xs_hbm.shape[0] // TM_MOE
    tm = h2_ring.shape[1]
    slot = i % 2
    h2_ref = _ring_fetch(h2_hbm, h2_ring, h2_sems, i, n_steps)

    @pl.when(i == 0)
    def _():
        zero_scr[...] = jnp.zeros_like(zero_scr)

        def ztail_copy(b):
            rows = pl.multiple_of(ztail_ref[1, b], SUBLANES)
            return pltpu.make_async_copy(
                zero_scr.at[pl.ds(0, rows)],
                xs_hbm.at[pl.ds(pl.multiple_of(ztail_ref[0, b], SUBLANES), rows)], zsem)

        def ztile_copy(tile):
            return pltpu.make_async_copy(
                zero_scr, xs_hbm.at[pl.ds(pl.multiple_of(tile * TM_MOE, TM_MOE), TM_MOE)], zsem)

        for b in range(N_BUCKETS):
            @pl.when(ztail_ref[1, b] > 0)
            def _(b=b):
                ztail_copy(b).start()
        lax.fori_loop(nvalid_ref[0], n_tiles, lambda tile, c: (ztile_copy(tile).start(), c)[1], 0)
        for b in range(N_BUCKETS):
            @pl.when(ztail_ref[1, b] > 0)
            def _(b=b):
                ztail_copy(b).wait()
        lax.fori_loop(nvalid_ref[0], n_tiles, lambda tile, c: (ztile_copy(tile).wait(), c)[1], 0)

    def run_copy(buf, local_row, global_row, rows):
        return pltpu.make_async_copy(sort_scr.at[buf, pl.ds(local_row, rows)],
                                     xs_hbm.at[pl.ds(global_row, rows)], sems.at[buf])

    def wait_runs(tile, buf):
        run_copy(buf, 0, 0, pl.multiple_of(meta_ref[META_TILE_ROWS, tile], SUBLANES)).wait()

    @pl.when(i >= 2)
    def _():
        wait_runs(i - 2, slot)

    lpos = _local_pos(meta_ref, i, bucket_ref[0], rank_ref[0])
    lane = lax.broadcasted_iota(I32, (LS_SUB, AUG), 1)
    for r0 in range(0, LS, LS_SUB):
        row_id = lax.broadcasted_iota(I32, (LS_SUB, tm), 0) + r0
        onehot = jnp.where(row_id == lpos, 1.0, 0.0).astype(BF16)
        srt = jnp.dot(onehot, h2_ref[...], preferred_element_type=F32)
        aug = srt[:, D:D + AUG]
        w_lo = aug[:, 0:1] + aug[:, 1:2] + aug[:, 2:3]
        w_hi = aug[:, 3:4] + aug[:, 4:5] + aug[:, 5:6]
        sort_scr[slot, r0:r0 + LS_SUB, 0:D] = srt[:, 0:D]
        sort_scr[slot, r0:r0 + LS_SUB, D:XW] = jnp.where(lane == 0, w_lo, jnp.where(lane == 1, w_hi, 0.0))
    _for_each_run(meta_ref, i, lambda lr, gr, rows, b: run_copy(slot, lr, gr, rows).start(priority=b % 2))

    @pl.when(i == n_steps - 1)
    def _():
        @pl.when(i >= 1)
        def _():
            wait_runs(i - 1, 1 - slot)
        wait_runs(i, slot)


def _dispatch(meta, ztail, n_valid, h2a, bucket, rank, n_rows):
    t, width = h2a.shape
    tm = TM_PROJ
    grid_spec = pltpu.PrefetchScalarGridSpec(
        num_scalar_prefetch=3,
        grid=(t // tm,),
        in_specs=[pl.BlockSpec(memory_space=pl.ANY),
                  pl.BlockSpec((1, 1, tm), lambda i, m, z, nv: (i, 0, 0)),
                  pl.BlockSpec((1, 1, tm), lambda i, m, z, nv: (i, 0, 0))],
        out_specs=pl.BlockSpec(memory_space=pl.ANY),
        scratch_shapes=[pltpu.VMEM((2, LS, XW), F32), pltpu.VMEM((TM_MOE, XW), F32),
                        pltpu.SemaphoreType.DMA((2,)), pltpu.SemaphoreType.DMA(()),
                        pltpu.VMEM((RING, tm, width), BF16), pltpu.SemaphoreType.DMA((RING,))],
    )
    return pl.pallas_call(
        functools.partial(_dispatch_kernel, n_steps=t // tm),
        grid_spec=grid_spec,
        out_shape=jax.ShapeDtypeStruct((n_rows, XW), F32),
        compiler_params=_cparams(("arbitrary",)),
        name="dispatch",
    )(meta, ztail, n_valid, h2a, bucket, rank)


def _moe_kernel(lo_ref, hi_ref, nvalid_ref, xs_ref, wg_lo, wu_lo, wd_lo, wg_hi, wu_hi, wd_hi, y_ref):
    del lo_ref, hi_ref
    t = pl.program_id(0)

    @pl.when(t < nvalid_ref[0])
    def _():
        xb = xs_ref[:, 0:D].astype(BF16)
        y = None
        for k, (wg, wu, wd) in enumerate(((wg_lo, wu_lo, wd_lo), (wg_hi, wu_hi, wd_hi))):
            gt = jnp.dot(xb, wg[0], preferred_element_type=F32)
            up = jnp.dot(xb, wu[0], preferred_element_type=F32)
            act = (gt * (1.0 / (1.0 + jnp.exp(-gt))) * up).astype(BF16)
            dn = jnp.dot(act, wd[0], preferred_element_type=F32)
            term = xs_ref[:, D + k:D + k + 1] * dn
            y = term if y is None else y + term
        y_ref[...] = y

    @pl.when(t >= nvalid_ref[0])
    def _():
        y_ref[...] = jnp.zeros_like(y_ref)


def _moe(tile_lo, tile_hi, n_valid, xs, wg, wu, wd):
    n_rows, width = xs.shape
    n_tiles = n_rows // TM_MOE
    up_lo = lambda t, lo, hi, nv: (lo[t], 0, 0)
    up_hi = lambda t, lo, hi, nv: (hi[t], 0, 0)
    grid_spec = pltpu.PrefetchScalarGridSpec(
        num_scalar_prefetch=3,
        grid=(n_tiles,),
        in_specs=[pl.BlockSpec((TM_MOE, width), lambda t, lo, hi, nv: (jnp.minimum(t, nv[0] - 1), 0)),
                  pl.BlockSpec((1, D, D_EXPERT), up_lo), pl.BlockSpec((1, D, D_EXPERT), up_lo),
                  pl.BlockSpec((1, D_EXPERT, D), up_lo),
                  pl.BlockSpec((1, D, D_EXPERT), up_hi), pl.BlockSpec((1, D, D_EXPERT), up_hi),
                  pl.BlockSpec((1, D_EXPERT, D), up_hi)],
        out_specs=pl.BlockSpec((TM_MOE, D), lambda t, lo, hi, nv: (t, 0)),
    )
    return pl.pallas_call(
        _moe_kernel,
        grid_spec=grid_spec,
        out_shape=jax.ShapeDtypeStruct((n_rows, D), F32),
        compiler_params=_cparams(("arbitrary",)),
        name="moe",
    )(tile_lo, tile_hi, n_valid, xs, wg, wu, wd, wg, wu, wd)


def _final_kernel(meta_ref, ys_hbm, x1_hbm, mod_ref, g_ref, bucket_ref, rank_ref, o_ref, y_scr, sems,
                  x1_ring, x1_sems, *, n_steps):
    i = pl.program_id(0)
    tm = o_ref.shape[0]
    slot = i % RING
    x1_ref = _ring_fetch(x1_hbm, x1_ring, x1_sems, i, n_steps)

    def run_copy(buf, local_row, global_row, rows):
        return pltpu.make_async_copy(ys_hbm.at[pl.ds(global_row, rows)],
                                     y_scr.at[buf, pl.ds(local_row, rows)], sems.at[buf])

    def fetch_runs(tile):
        buf = tile % RING
        _for_each_run(meta_ref, tile,
                      lambda lr, gr, rows, b: run_copy(buf, lr, gr, rows).start(priority=b % 2))

    @pl.when(i == 0)
    def _():
        y_scr[...] = jnp.zeros_like(y_scr)
        for s in range(min(RING - 1, n_steps)):
            fetch_runs(s)

    @pl.when(i + RING - 1 < n_steps)
    def _():
        fetch_runs(i + RING - 1)

    run_copy(slot, 0, 0, pl.multiple_of(meta_ref[META_TILE_ROWS, i], SUBLANES)).wait()

    lpos = _local_pos(meta_ref, i, bucket_ref[0], rank_ref[0])
    lpos_col = jnp.broadcast_to(lpos.astype(F32), (LANES, tm)).T[:, 0:1].astype(I32)
    ysrt = y_scr[slot].astype(BF16)
    m = mod_ref[0]
    for r0 in range(0, tm, TM_OUT_SUB):
        rows = slice(r0, r0 + TM_OUT_SUB)
        onehot = jnp.where(lax.broadcasted_iota(I32, (TM_OUT_SUB, LS), 1) == lpos_col[rows], 1.0, 0.0)
        y = jnp.dot(onehot.astype(BF16), ysrt, preferred_element_type=F32)
        x2 = x1_ref[rows, :] + m[5:6] * y
        ms = jnp.mean(x2 * x2, axis=-1, keepdims=True)
        o_ref[rows, :] = x2 * lax.rsqrt(ms + EPS) * g_ref[...]


def _final(meta, ys, x1, mod3, g_final, bucket, rank, seq):
    t = x1.shape[0]
    tm = TM_PROJ
    per_b = seq // tm
    grid_spec = pltpu.PrefetchScalarGridSpec(
        num_scalar_prefetch=1,
        grid=(t // tm,),
        in_specs=[pl.BlockSpec(memory_space=pl.ANY),
                  pl.BlockSpec(memory_space=pl.ANY),
                  pl.BlockSpec((1, 6, D), lambda i, m: (i // per_b, 0, 0)),
                  pl.BlockSpec((1, D), lambda i, m: (0, 0)),
                  pl.BlockSpec((1, 1, tm), lambda i, m: (i, 0, 0)),
                  pl.BlockSpec((1, 1, tm), lambda i, m: (i, 0, 0))],
        out_specs=pl.BlockSpec((tm, D), lambda i, m: (i, 0)),
        scratch_shapes=[pltpu.VMEM((RING, LS, D), F32), pltpu.SemaphoreType.DMA((RING,)),
                        pltpu.VMEM((RING, tm, D), F32), pltpu.SemaphoreType.DMA((RING,))],
    )
    return pl.pallas_call(
        functools.partial(_final_kernel, n_steps=t // tm),
        grid_spec=grid_spec,
        out_shape=jax.ShapeDtypeStruct((t, D), F32),
        compiler_params=_cparams(("arbitrary",)),
        name="final",
    )(meta, ys, x1, mod3, g_final, bucket, rank)


def _rope_tables(seq):
    inv = ROPE_THETA ** (-jnp.arange(0, HD, 2, dtype=F32) / HD)
    ang = jnp.arange(seq, dtype=F32)[:, None] * inv[None, :]
    cos, sin = jnp.cos(ang), jnp.sin(ang)
    return jnp.tile(cos, (1, 4)), jnp.concatenate([-sin, -sin, sin, sin], axis=-1)


def _rotary_column_order():
    half = HD // 2
    lane = np.arange(LANES)
    which, part, f = (lane // half) % 2, lane // HD, lane % half
    cols = []
    for base in (0, 512):
        for h in range(4):
            cols.append(base + h * LANES + which * HD + part * half + f)
    cols.append(np.arange(1024, 1536))
    for c in range(4):
        cols.append(1536 + (c + 4 * which) * HD + part * half + f)
    cols.append(2048 + which * HD + part * half + f)
    cols.append(np.arange(2176, 2304))
    return np.concatenate(cols)


def kernel(x, c, w_ada, b_ada, g_mix, w_in, diff_lambda, g_diff_sub, swa_sinks, w_out, g_ffn,
           w_route_group, b_route_group, w_route_expert, b_route_expert, w_gate, w_up, w_down,
           g_final):
    bsz, seq, _ = x.shape
    t = bsz * seq
    x2 = x.reshape(t, D)
    cos, sin = _rope_tables(seq)

    mod3 = _ada_mod(c, w_ada[0], b_ada[0].reshape(1, -1)).reshape(bsz, 6, D)

    w_in_bf = w_in[0][:, _rotary_column_order()].astype(BF16)
    dq, dk, dv, sq, sk, sv = _in_proj(x2, mod3, g_mix[0].reshape(1, D), w_in_bf, cos, sin, seq)
    n_exp = w_gate.shape[1]
    o_diff, wg_bf, wu_bf, wd_bf = _diff_attn(
        dq, dk, dv, diff_lambda[0], g_diff_sub[0].reshape(1, LANES), bsz, seq,
        (w_gate[0].reshape(n_exp * D, D_EXPERT), w_up[0].reshape(n_exp * D, D_EXPERT),
         w_down[0].reshape(n_exp * D_EXPERT, D)))
    o_swa = _swa_attn(sq, sk, sv, swa_sinks[0].reshape(1, 8), bsz, seq)

    n_r = N_GROUPS + N_GROUPS * EPG
    w_r = jnp.concatenate([w_route_group[0], w_route_expert[0],
                           jnp.zeros((D, LANES - n_r), F32)], axis=1).astype(BF16)
    b_r = jnp.concatenate([b_route_group[0], b_route_expert[0],
                           jnp.zeros((LANES - n_r,), F32)]).reshape(1, LANES)
    tri = (jnp.arange(TM_PROJ)[:, None] < jnp.arange(TM_PROJ)[None, :]).astype(BF16)
    x1, h2a, bucket, rank, counts = _out_proj(o_diff, o_swa, x2, mod3, g_ffn[0].reshape(1, D),
                                              w_out[0].astype(BF16), w_r, b_r, tri, seq)

    n_tok_tiles = t // TM_PROJ
    n_tiles = -(-(t + n_tok_tiles * N_BUCKETS * (SUBLANES - 1)) // TM_MOE) + N_BUCKETS
    cnt = counts[:, :N_BUCKETS, 0].astype(I32)
    run_rows = (cnt + SUBLANES - 1) // SUBLANES * SUBLANES
    local_off = jnp.cumsum(run_rows, axis=1) - run_rows
    tiles_per = (jnp.sum(run_rows, axis=0) + TM_MOE - 1) // TM_MOE
    tile_end = jnp.cumsum(tiles_per)
    bucket_start = (tile_end - tiles_per) * TM_MOE
    global_off = bucket_start[None, :] + jnp.cumsum(run_rows, axis=0) - run_rows
    tile_rows = jnp.pad(jnp.sum(run_rows, axis=1), (0, n_tok_tiles * (N_BUCKETS - 1)))
    meta = jnp.stack([local_off.reshape(-1), global_off.reshape(-1), run_rows.reshape(-1), tile_rows])
    tid = jnp.arange(n_tiles, dtype=I32)
    tile_bucket = jnp.minimum(jnp.sum(tid[:, None] >= tile_end[None, :], axis=1), N_BUCKETS - 1)
    n_valid = tile_end[-1:].astype(I32)
    bucket_rows = jnp.sum(run_rows, axis=0)
    ztail = jnp.stack([bucket_start + bucket_rows, tiles_per * TM_MOE - bucket_rows]).astype(I32)
    grp, pair = tile_bucket // 6, tile_bucket % 6
    tile_lo = (grp * EPG + jnp.asarray(PAIR_LO, I32)[pair]).astype(I32)
    tile_hi = (grp * EPG + jnp.asarray(PAIR_HI, I32)[pair]).astype(I32)

    xs = _dispatch(meta, ztail, n_valid, h2a, bucket, rank, n_tiles * TM_MOE)
    ys = _moe(tile_lo, tile_hi, n_valid, xs, wg_bf.reshape(n_exp, D, D_EXPERT),
              wu_bf.reshape(n_exp, D, D_EXPERT), wd_bf.reshape(n_exp, D_EXPERT, D))
    out = _final(meta, ys, x1, mod3, g_final.reshape(1, D), bucket, rank, seq)
    return out.reshape(bsz, seq, D)
```
